```python
import functools
import jax, jax.numpy as jnp
from jax import lax
import numpy as np

D_MODEL = 1024
BATCH = 4
SEQ = 8192
DEPTH = 2
DEC_BATCH = 32
DEC_SEQ = 16
PAST_LEN = 2048

CHUNK = 64
N_MIXERS = 2
N_RET_LAYERS = (DEPTH + 1) // 2
N_LRU_LAYERS = DEPTH // 2
RET_HEADS = 4
RET_DK = D_MODEL // RET_HEADS
RET_DV = 2 * D_MODEL // RET_HEADS
RET_QK = RET_HEADS * RET_DK
RET_V = RET_HEADS * RET_DV
ROPE_BASE = 10000.0
D_RNN = D_MODEL
LRU_BLOCKS = 4
LRU_BW = D_RNN // LRU_BLOCKS
CONV_W = 4
LRU_C = 8.0
MOE_GROUPS = 4
MOE_EXPERTS_PER_GROUP = 4
MOE_EXPERTS = MOE_GROUPS * MOE_EXPERTS_PER_GROUP
MOE_TOP_K = 2
MOE_FF = 512
NORM_EPS = 1e-6

kernel_name = 'hybrid_retention_rglru_hmoe_stream_step'

F32 = jnp.float32


def rms_norm(x, g):
    xf = x.astype(F32)
    y = xf * lax.rsqrt(jnp.mean(xf * xf, axis=-1, keepdims=True) + NORM_EPS)
    return (y * g.astype(F32)).astype(x.dtype)


def rotary(t, pos):
    half = RET_DK // 2
    inv = 1.0 / (ROPE_BASE ** jnp.linspace(0.0, 1.0, half, dtype=F32))
    ang = pos.astype(F32)[:, None] * inv[None, :]
    cos, sin = jnp.cos(ang), jnp.sin(ang)
    t1, t2 = t[..., :half], t[..., half:]
    return jnp.concatenate([t1 * cos - t2 * sin, t1 * sin + t2 * cos], axis=-1)


def retention_block(S0, qkv, log_gamma):
    q, k, v = qkv
    L = q.shape[2]
    idx = jnp.arange(L, dtype=F32)
    diff = idx[:, None] - idx[None, :]
    decay = jnp.where(diff >= 0, jnp.exp(jnp.maximum(diff, 0.0)[None] * log_gamma[:, None, None]), 0.0)
    scores = jnp.einsum('bhid,bhjd->bhij', q, k) * decay[None]
    q_dec = jnp.exp((idx + 1.0)[None, :] * log_gamma[:, None])
    k_dec = jnp.exp((L - 1.0 - idx)[None, :] * log_gamma[:, None])
    o = (jnp.einsum('bhij,bhje->bhie', scores, v)
         + jnp.einsum('bhid,bhde->bhie', q * q_dec[None, :, :, None], S0))
    S = (jnp.exp(L * log_gamma)[None, :, None, None] * S0
         + jnp.einsum('bhjd,bhje->bhde', k * k_dec[None, :, :, None], v))
    return S, o


def retention_mixer(h, pos0, S0, w_in, w_out, norm_g):
    B, L, _ = h.shape
    proj = h @ w_in
    q, k, v, g = jnp.split(proj, [RET_QK, 2 * RET_QK, 2 * RET_QK + RET_V], axis=-1)

    def heads(t, d):
        return t.reshape(B, L, RET_HEADS, d).transpose(0, 2, 1, 3).astype(F32)

    pos = pos0 + jnp.arange(L)
    q = rotary(heads(q, RET_DK), pos)
    k = rotary(heads(k, RET_DK), pos) * (RET_DK ** -0.5)
    v = heads(v, RET_DV)
    log_gamma = jnp.log1p(-jnp.exp2(-5.0 - jnp.arange(RET_HEADS, dtype=F32)))
    cl = min(L, CHUNK)
    n = L // cl

    def to_chunks(t):
        return t.reshape(B, RET_HEADS, n, cl, t.shape[-1]).transpose(2, 0, 1, 3, 4)

    S, o = lax.scan(functools.partial(retention_block, log_gamma=log_gamma), S0.astype(F32),
                    (to_chunks(q), to_chunks(k), to_chunks(v)))
    o = o.transpose(1, 2, 0, 3, 4).reshape(B, RET_HEADS, L, RET_DV)
    o = o * lax.rsqrt(jnp.mean(o * o, axis=-1, keepdims=True) + NORM_EPS)
    o = o.transpose(0, 2, 1, 3).reshape(B, L, RET_V) * norm_g.astype(F32)
    y = (jax.nn.silu(g.astype(F32)) * o).astype(h.dtype) @ w_out
    return y, S


def lin_rec_combine(e1, e2):
    a1, b1 = e1
    a2, b2 = e2
    return a1 * a2, a2 * b1 + b2


def rglru_mixer(h, pos0, conv_state, h0, w_in, conv_w, conv_b, w_a, b_a, w_x, b_x, lam, w_out):
    B, L, _ = h.shape
    proj = h @ w_in
    gate_br, x_br = jnp.split(proj, 2, axis=-1)
    gate = jax.nn.gelu(gate_br, approximate=True)
    xc = jnp.concatenate([conv_state.astype(x_br.dtype), x_br], axis=1)
    conv = conv_b + xc[:, 0:L] * conv_w[0]
    for j in range(1, CONV_W):
        conv = conv + xc[:, j:j + L] * conv_w[j]
    new_conv = xc[:, L:]
    xb = conv.reshape(B, L, LRU_BLOCKS, LRU_BW).astype(F32)
    r = jax.nn.sigmoid(jnp.einsum('blnc,ncd->blnd', xb, w_a.astype(F32)).reshape(B, L, D_RNN) + b_a.astype(F32))
    i = jax.nn.sigmoid(jnp.einsum('blnc,ncd->blnd', xb, w_x.astype(F32)).reshape(B, L, D_RNN) + b_x.astype(F32))
    log_a = -LRU_C * r * jax.nn.softplus(-lam.astype(F32))
    a = jnp.exp(log_a)
    pos = pos0 + jnp.arange(L)
    mult = jnp.where((pos == 0)[None, :, None], 1.0, jnp.sqrt(-jnp.expm1(2.0 * log_a)))
    b = mult * (i * conv.astype(F32))
    b = b.at[:, 0].add(a[:, 0] * h0.astype(F32))
    _, hs = lax.associative_scan(lin_rec_combine, (a, b), axis=1)
    y = (hs.astype(h.dtype) * gate) @ w_out
    return y, new_conv, hs[:, -1]


def hier_moe(t, w_group, b_group, w_router, b_router, w_gate, w_up, w_down):
    T = t.shape[0]
    tf = t.astype(F32)
    g_prob = jax.nn.softmax(tf @ w_group.astype(F32) + b_group.astype(F32), axis=-1)
    p_g, g_idx = lax.top_k(g_prob, 1)
    e_logit = (tf @ w_router.astype(F32) + b_router.astype(F32)).reshape(T, MOE_GROUPS, MOE_EXPERTS_PER_GROUP)
    e_logit = jnp.take_along_axis(e_logit, g_idx[:, :, None], axis=1)[:, 0]
    e_prob = jax.nn.softmax(e_logit, axis=-1)
    top_p, top_i = lax.top_k(e_prob, MOE_TOP_K)
    w = p_g * top_p / jnp.sum(top_p, axis=-1, keepdims=True)
    e_idx = g_idx * MOE_EXPERTS_PER_GROUP + top_i
    combine = jnp.sum(jax.nn.one_hot(e_idx, MOE_EXPERTS, dtype=F32) * w[..., None], axis=1)
    out = jnp.zeros((T, D_MODEL), F32)
    for e in range(MOE_EXPERTS):
        hdn = jax.nn.silu(t @ w_gate[e]) * (t @ w_up[e])
        out = out + combine[:, e:e + 1] * (hdn @ w_down[e]).astype(F32)
    return out.astype(t.dtype)


def setup_inputs(seed: int = 0) -> dict:
    key = jax.random.key(seed)
    ks = jax.random.split(key, 32)
    nr, nl = N_RET_LAYERS, N_LRU_LAYERS

    def nrm(k, shape, scale):
        return jax.random.normal(k, shape, F32) * scale

    u = jax.random.uniform(ks[17], (nl, D_RNN), F32, minval=0.9, maxval=0.999)
    s = u ** (1.0 / LRU_C)
    lam = jnp.log(s) - jnp.log1p(-s)
    return {
        'x_prompt': nrm(ks[0], (BATCH, SEQ, D_MODEL), 1.0),
        'x_sample': nrm(ks[1], (DEC_BATCH, DEC_SEQ, D_MODEL), 1.0),
        'state_ret': nrm(ks[2], (nr, DEC_BATCH, RET_HEADS, RET_DK, RET_DV), 0.1),
        'state_conv': nrm(ks[3], (nl, DEC_BATCH, CONV_W - 1, D_RNN), 1.0),
        'state_lru': nrm(ks[4], (nl, DEC_BATCH, D_RNN), 0.5),
        'norm_mix_g': 1.0 + nrm(ks[5], (DEPTH, D_MODEL), 0.02),
        'norm_ffn_g': 1.0 + nrm(ks[6], (DEPTH, D_MODEL), 0.02),
        'norm_final_g': 1.0 + nrm(ks[7], (D_MODEL,), 0.02),
        'ret_w_in': nrm(ks[8], (nr, D_MODEL, 2 * RET_QK + 2 * RET_V), D_MODEL ** -0.5),
        'ret_w_out': nrm(ks[9], (nr, RET_V, D_MODEL), RET_V ** -0.5),
        'ret_norm_g': 1.0 + nrm(ks[10], (nr, RET_V), 0.02),
        'lru_w_in': nrm(ks[11], (nl, D_MODEL, 2 * D_RNN), D_MODEL ** -0.5),
        'lru_conv_w': nrm(ks[12], (nl, CONV_W, D_RNN), CONV_W ** -0.5),
        'lru_conv_b': nrm(ks[13], (nl, D_RNN), 0.01),
        'lru_w_a': nrm(ks[14], (nl, LRU_BLOCKS, LRU_BW, LRU_BW), LRU_BW ** -0.5),
        'lru_b_a': nrm(ks[15], (nl, D_RNN), 0.01),
        'lru_w_x': nrm(ks[16], (nl, LRU_BLOCKS, LRU_BW, LRU_BW), LRU_BW ** -0.5),
        'lru_b_x': nrm(ks[18], (nl, D_RNN), 0.01),
        'lru_lambda': lam,
        'lru_w_out': nrm(ks[19], (nl, D_RNN, D_MODEL), D_RNN ** -0.5),
        'moe_w_group': nrm(ks[20], (DEPTH, D_MODEL, MOE_GROUPS), D_MODEL ** -0.5),
        'moe_b_group': nrm(ks[21], (DEPTH, MOE_GROUPS), 0.01),
        'moe_w_router': nrm(ks[22], (DEPTH, D_MODEL, MOE_EXPERTS), D_MODEL ** -0.5),
        'moe_b_router': nrm(ks[23], (DEPTH, MOE_EXPERTS), 0.01),
        'moe_w_gate': nrm(ks[24], (DEPTH, MOE_EXPERTS, D_MODEL, MOE_FF), D_MODEL ** -0.5),
        'moe_w_up': nrm(ks[25], (DEPTH, MOE_EXPERTS, D_MODEL, MOE_FF), D_MODEL ** -0.5),
        'moe_w_down': nrm(ks[26], (DEPTH, MOE_EXPERTS, MOE_FF, D_MODEL), MOE_FF ** -0.5),
    }


def reference(x_prompt, x_sample, state_ret, state_conv, state_lru, norm_mix_g, norm_ffn_g, norm_final_g,
              ret_w_in, ret_w_out, ret_norm_g, lru_w_in, lru_conv_w, lru_conv_b, lru_w_a, lru_b_a,
              lru_w_x, lru_b_x, lru_lambda, lru_w_out, moe_w_group, moe_b_group, moe_w_router,
              moe_b_router, moe_w_gate, moe_w_up, moe_w_down):
    B, S, _ = x_prompt.shape
    yp, ys = x_prompt, x_sample
    ret_p, ret_s, conv_p, conv_s, lru_p, lru_s = [], [], [], [], [], []
    for i in range(DEPTH):
        j = i // N_MIXERS
        hp = rms_norm(yp, norm_mix_g[i])
        hs = rms_norm(ys, norm_mix_g[i])
        if i % N_MIXERS == 0:
            s0 = jnp.zeros((B, RET_HEADS, RET_DK, RET_DV), F32)
            mp, sp = retention_mixer(hp, 0, s0, ret_w_in[j], ret_w_out[j], ret_norm_g[j])
            ms, ss = retention_mixer(hs, PAST_LEN, state_ret[j], ret_w_in[j], ret_w_out[j], ret_norm_g[j])
            ret_p.append(sp.astype(state_ret.dtype))
            ret_s.append(ss.astype(state_ret.dtype))
        else:
            lru_args = (lru_w_in[j], lru_conv_w[j], lru_conv_b[j], lru_w_a[j], lru_b_a[j],
                        lru_w_x[j], lru_b_x[j], lru_lambda[j], lru_w_out[j])
            c0 = jnp.zeros((B, CONV_W - 1, D_RNN), hp.dtype)
            h0 = jnp.zeros((B, D_RNN), F32)
            mp, cp, lp = rglru_mixer(hp, 0, c0, h0, *lru_args)
            ms, cs, ls = rglru_mixer(hs, PAST_LEN, state_conv[j], state_lru[j], *lru_args)
            conv_p.append(cp.astype(state_conv.dtype))
            conv_s.append(cs.astype(state_conv.dtype))
            lru_p.append(lp.astype(state_lru.dtype))
            lru_s.append(ls.astype(state_lru.dtype))
        yp = yp + mp
        ys = ys + ms
        tok = jnp.concatenate([yp.reshape(-1, D_MODEL), ys.reshape(-1, D_MODEL)], axis=0)
        f = hier_moe(rms_norm(tok, norm_ffn_g[i]), moe_w_group[i], moe_b_group[i], moe_w_router[i],
                     moe_b_router[i], moe_w_gate[i], moe_w_up[i], moe_w_down[i])
        yp = yp + f[:B * S].reshape(yp.shape)
        ys = ys + f[B * S:].reshape(ys.shape)
    y_prompt = rms_norm(yp, norm_final_g)
    y_sample = rms_norm(ys, norm_final_g)
    new_ret_p = jnp.stack(ret_p)
    new_conv_p = jnp.stack(conv_p)
    new_lru_p = jnp.stack(lru_p)
    new_ret_s = jnp.stack(ret_s)
    new_conv_s = jnp.stack(conv_s)
    new_lru_s = jnp.stack(lru_s)
    return (y_prompt, y_sample, new_ret_p, new_conv_p, new_lru_p, new_ret_s, new_conv_s, new_lru_s)
```

```python
import functools
import math

import numpy as np
import jax
import jax.numpy as jnp
from jax import lax
from jax.experimental import pallas as pl
from jax.experimental.pallas import tpu as pltpu

F32 = jnp.float32
BF16 = jnp.bfloat16

RET_HEADS = 4
RET_CHUNK = 64
ROPE_BASE = 10000.0
CONV_W = 4
LRU_BLOCKS = 4
LRU_C = 8.0
MOE_GROUPS = 4
MOE_PER_GROUP = 4
MOE_EXPERTS = MOE_GROUPS * MOE_PER_GROUP
NORM_EPS = 1e-6
PAST_LEN = 2048

LANES = 128
SUBLANES = 8
VMEM_LIMIT_BYTES = 56 * 1024 * 1024

RET_TILE = 256
LRU_TILE = 256
ROUTE_TILE = 512
MOVE_TILE = 256
FFN_TILE = 256
N_PAIRS = 6
N_CLASSES = MOE_GROUPS * N_PAIRS
EXTRA = LANES
NEG_BIG = -1e30

_PAIR_LO = np.array([0, 0, 0, 1, 1, 2], np.int32)
_PAIR_HI = np.array([1, 2, 3, 2, 3, 3], np.int32)


def _log_gammas():
    return [math.log1p(-(2.0 ** (-5.0 - h))) for h in range(RET_HEADS)]


def _rms(x, g):
    return x * lax.rsqrt(jnp.mean(x * x, axis=-1, keepdims=True) + NORM_EPS) * g


def _dot(a, b):
    return jnp.dot(a, b, preferred_element_type=F32)


def _dot_nt(a, b):
    return lax.dot_general(a, b, (((1,), (1,)), ((), ())), preferred_element_type=F32)


def _dot_tn(a, b):
    return lax.dot_general(a, b, (((0,), (0,)), ((), ())), preferred_element_type=F32)


def _const_spec(shape):
    nd = len(shape)
    return pl.BlockSpec(shape, lambda *_: (0,) * nd, pipeline_mode=pl.Buffered(1))


def _params(sem):
    return pltpu.CompilerParams(dimension_semantics=sem, vmem_limit_bytes=VMEM_LIMIT_BYTES)


def _pick_tile(n, pref):
    t = pref
    while n % t:
        t //= 2
    assert t >= SUBLANES, (n, pref)
    return t


def _decay_matrix(L, chunk, lg):
    ii = lax.broadcasted_iota(jnp.int32, (L, L), 0)
    jj = lax.broadcasted_iota(jnp.int32, (L, L), 1)
    diff = (ii - jj).astype(F32)
    same_chunk = (ii // chunk) == (jj // chunk)
    return jnp.where(same_chunk & (ii >= jj), jnp.exp(jnp.maximum(diff, 0.0) * lg), 0.0)


def _rotary(t, cos, sin):
    half = t.shape[-1] // 2
    t1, t2 = t[:, :half], t[:, half:]
    return jnp.concatenate([t1 * cos - t2 * sin, t1 * sin + t2 * cos], axis=-1)


def _ret_head(hn, cos, sin, decay, s0, win_ref, ng_ref, hd, L, chunk, dk, dv, lg):
    qk, vv = RET_HEADS * dk, RET_HEADS * dv
    q = _dot(hn, win_ref[:, hd * dk:(hd + 1) * dk])
    k = _dot(hn, win_ref[:, qk + hd * dk:qk + (hd + 1) * dk])
    v = _dot(hn, win_ref[:, 2 * qk + hd * dv:2 * qk + (hd + 1) * dv])
    gt = _dot(hn, win_ref[:, 2 * qk + vv + hd * dv:2 * qk + vv + (hd + 1) * dv])
    qr = _rotary(q, cos, sin)
    kr = _rotary(k, cos, sin) * (dk ** -0.5)
    idx = (lax.broadcasted_iota(jnp.int32, (L, 1), 0) % chunk).astype(F32)
    qd = (qr * jnp.exp((idx + 1.0) * lg)).astype(BF16)
    kd = (kr * jnp.exp((chunk - 1.0 - idx) * lg)).astype(BF16)
    vb = v.astype(BF16)
    scores = _dot_nt(qr.astype(BF16), kr.astype(BF16)) * decay
    intra = _dot(scores.astype(BF16), vb)
    s, parts = s0, []
    for c in range(L // chunk):
        rows = slice(c * chunk, (c + 1) * chunk)
        parts.append(intra[rows] + _dot(qd[rows], s.astype(BF16)))
        s = math.exp(chunk * lg) * s + _dot_tn(kd[rows], vb[rows])
    o = parts[0] if len(parts) == 1 else jnp.concatenate(parts, axis=0)
    o = o * lax.rsqrt(jnp.mean(o * o, axis=-1, keepdims=True) + NORM_EPS)
    o = o * ng_ref[:, hd * dv:(hd + 1) * dv]
    og = (gt * jax.nn.sigmoid(gt)) * o
    return og, s


def _ret_prompt_body(x_ref, g_ref, win_ref, wout_ref, ng_ref, cos_ref, sin_ref,
                     y_ref, s_ref, decay_scr, og_scr, *, L, chunk, dk, dv):
    b, c = pl.program_id(0), pl.program_id(1)
    lgs = _log_gammas()

    @pl.when((b == 0) & (c == 0))
    def _():
        for hd in range(RET_HEADS):
            decay_scr[hd] = _decay_matrix(L, chunk, lgs[hd])

    @pl.when(c == 0)
    def _():
        s_ref[...] = jnp.zeros_like(s_ref)

    x = x_ref[...]
    hn = _rms(x, g_ref[...]).astype(BF16)
    cos, sin = cos_ref[...], sin_ref[...]
    for hd in range(RET_HEADS):
        og, s_new = _ret_head(hn, cos, sin, decay_scr[hd], s_ref[0, hd], win_ref, ng_ref,
                              hd, L, chunk, dk, dv, lgs[hd])
        s_ref[0, hd] = s_new
        og_scr[:, hd * dv:(hd + 1) * dv] = og.astype(BF16)
    y_ref[...] = x + _dot(og_scr[...], wout_ref[...])


def _ret_prompt(x_p, B, S, g, win, wout, ng, cos, sin):
    Tp, D = x_p.shape
    dk, dv = D // RET_HEADS, 2 * D // RET_HEADS
    chunk = min(S, RET_CHUNK)
    L = _pick_tile(S, RET_TILE)
    assert L % chunk == 0
    nc = S // L
    body = functools.partial(_ret_prompt_body, L=L, chunk=chunk, dk=dk, dv=dv)
    return pl.pallas_call(
        body,
        out_shape=(jax.ShapeDtypeStruct((Tp, D), F32),
                   jax.ShapeDtypeStruct((B, RET_HEADS, dk, dv), F32)),
        grid=(B, nc),
        in_specs=[
            pl.BlockSpec((L, D), lambda b, c: (b * nc + c, 0)),
            _const_spec((1, D)),
            _const_spec(win.shape),
            _const_spec(wout.shape),
            _const_spec((1, RET_HEADS * dv)),
            pl.BlockSpec((L, dk // 2), lambda b, c: (c, 0)),
            pl.BlockSpec((L, dk // 2), lambda b, c: (c, 0)),
        ],
        out_specs=(
            pl.BlockSpec((L, D), lambda b, c: (b * nc + c, 0)),
            pl.BlockSpec((1, RET_HEADS, dk, dv), lambda b, c: (b, 0, 0, 0)),
        ),
        scratch_shapes=[pltpu.VMEM((RET_HEADS, L, L), F32),
                        pltpu.VMEM((L, RET_HEADS * dv), BF16)],
        compiler_params=_params(("arbitrary", "arbitrary")),
        name="ret_prompt",
    )(x_p, g, win, wout, ng, cos, sin)


def _ret_sample_body(x_ref, g_ref, win_ref, wout_ref, ng_ref, cos_ref, sin_ref, s0_ref,
                     y_ref, s_ref, hn_scr, og_scr, *, L, dk, dv):
    b, nb = pl.program_id(0), pl.num_programs(0)
    lgs = _log_gammas()

    @pl.when(b == 0)
    def _():
        hn_scr[...] = _rms(x_ref[...], g_ref[...]).astype(BF16)

    row0 = pl.multiple_of(b * L, L)
    hn = hn_scr[pl.ds(row0, L), :]
    cos, sin = cos_ref[...], sin_ref[...]
    for hd in range(RET_HEADS):
        og, s_new = _ret_head(hn, cos, sin, _decay_matrix(L, L, lgs[hd]), s0_ref[0, hd], win_ref,
                              ng_ref, hd, L, L, dk, dv, lgs[hd])
        s_ref[0, hd] = s_new
        og_scr[pl.ds(row0, L), hd * dv:(hd + 1) * dv] = og.astype(BF16)

    @pl.when(b == nb - 1)
    def _():
        y_ref[...] = x_ref[...] + _dot(og_scr[...], wout_ref[...])


def _ret_sample(x_s, Bs, Ls, g, win, wout, ng, cos, sin, s0):
    Ts, D = x_s.shape
    dk, dv = D // RET_HEADS, 2 * D // RET_HEADS
    assert Ls <= RET_CHUNK
    body = functools.partial(_ret_sample_body, L=Ls, dk=dk, dv=dv)
    return pl.pallas_call(
        body,
        out_shape=(jax.ShapeDtypeStruct((Ts, D), F32),
                   jax.ShapeDtypeStruct((Bs, RET_HEADS, dk, dv), F32)),
        grid=(Bs,),
        in_specs=[
            _const_spec((Ts, D)),
            _const_spec((1, D)),
            _const_spec(win.shape),
            _const_spec(wout.shape),
            _const_spec((1, RET_HEADS * dv)),
            _const_spec((Ls, dk // 2)),
            _const_spec((Ls, dk // 2)),
            pl.BlockSpec((1, RET_HEADS, dk, dv), lambda b: (b, 0, 0, 0)),
        ],
        out_specs=(
            pl.BlockSpec((Ts, D), lambda b: (0, 0)),
            pl.BlockSpec((1, RET_HEADS, dk, dv), lambda b: (b, 0, 0, 0)),
        ),
        scratch_shapes=[pltpu.VMEM((Ts, D), BF16), pltpu.VMEM((Ts, RET_HEADS * dv), BF16)],
        compiler_params=_params(("arbitrary",)),
        name="ret_sample",
    )(x_s, g, win, wout, ng, cos, sin, s0)


def _gelu_tanh(x):
    return 0.5 * x * (1.0 + jnp.tanh(math.sqrt(2.0 / math.pi) * (x + 0.044715 * (x * x * x))))


def _softplus(z):
    return jnp.maximum(z, 0.0) + jnp.log1p(jnp.exp(-jnp.abs(z)))


def _lru_gates(conv, wa_ref, wx_ref, ba, bx, lam):
    D = conv.shape[1]
    bw = D // LRU_BLOCKS
    ra, ix = [], []
    for n in range(LRU_BLOCKS):
        cb = conv[:, n * bw:(n + 1) * bw].astype(BF16)
        ra.append(_dot(cb, wa_ref[n]))
        ix.append(_dot(cb, wx_ref[n]))
    r = jax.nn.sigmoid(jnp.concatenate(ra, axis=-1) + ba)
    i = jax.nn.sigmoid(jnp.concatenate(ix, axis=-1) + bx)
    log_a = (-LRU_C) * r * _softplus(-lam)
    a = jnp.exp(log_a)
    mult = jnp.sqrt(jnp.tanh(-log_a) * (1.0 + a * a))
    return a, mult, i * conv


def _seg_scan(a, b, seg):
    R = a.shape[0]
    row = lax.broadcasted_iota(jnp.int32, (R, 1), 0) % seg
    k = 1
    while k < seg:
        keep = row >= k
        a_sh = jnp.where(keep, pltpu.roll(a, k, axis=0), 1.0)
        b_sh = jnp.where(keep, pltpu.roll(b, k, axis=0), 0.0)
        b = a * b_sh + b
        a = a * a_sh
        k *= 2
    return b


def _lru_prompt_body(x_ref, g_ref, win_ref, cw_ref, cb_ref, wa_ref, ba_ref, wx_ref, bx_ref,
                     lam_ref, wout_ref, y_ref, conv_ref, h_ref, xc_scr, hprev_scr, *, L):
    c = pl.program_id(1)
    D = x_ref.shape[1]

    @pl.when(c == 0)
    def _():
        xc_scr[0:8, :] = jnp.zeros((8, D), F32)
        hprev_scr[...] = jnp.zeros_like(hprev_scr)

    x = x_ref[...]
    hn = _rms(x, g_ref[...]).astype(BF16)
    gate = _gelu_tanh(_dot(hn, win_ref[:, :D]))
    x_br = _dot(hn, win_ref[:, D:])
    xc_scr[8:8 + L, :] = x_br
    conv = cb_ref[...] + xc_scr[5:5 + L, :] * cw_ref[0:1, :]
    for j in range(1, CONV_W):
        conv = conv + xc_scr[5 + j:5 + j + L, :] * cw_ref[j:j + 1, :]
    a, mult, gi = _lru_gates(conv, wa_ref, wx_ref, ba_ref[...], bx_ref[...], lam_ref[...])
    row = lax.broadcasted_iota(jnp.int32, (L, 1), 0)
    mult = jnp.where((row == 0) & (c == 0), 1.0, mult)
    bvec = mult * gi
    bvec = bvec + jnp.where(row == 0, a * hprev_scr[7:8, :], 0.0)
    hs = _seg_scan(a, bvec, L)
    y_ref[...] = x + _dot((hs * gate).astype(BF16), wout_ref[...])
    hprev_scr[...] = hs[L - 8:L, :]
    h_ref[0] = hs[L - 8:L, :]
    conv_ref[0] = x_br[L - 8:L, :]
    xc_scr[0:8, :] = x_br[L - 8:L, :]


def _lru_prompt(x_p, B, S, g, win, cw, cb, wa, ba, wx, bx, lam, wout):
    Tp, D = x_p.shape
    L = _pick_tile(S, LRU_TILE)
    nc = S // L
    body = functools.partial(_lru_prompt_body, L=L)
    return pl.pallas_call(
        body,
        out_shape=(jax.ShapeDtypeStruct((Tp, D), F32),
                   jax.ShapeDtypeStruct((B, 8, D), F32),
                   jax.ShapeDtypeStruct((B, 8, D), F32)),
        grid=(B, nc),
        in_specs=[
            pl.BlockSpec((L, D), lambda b, c: (b * nc + c, 0)),
            _const_spec((1, D)), _const_spec(win.shape), _const_spec(cw.shape), _const_spec((1, D)),
            _const_spec(wa.shape), _const_spec((1, D)), _const_spec(wx.shape), _const_spec((1, D)),
            _const_spec((1, D)), _const_spec(wout.shape),
        ],
        out_specs=(
            pl.BlockSpec((L, D), lambda b, c: (b * nc + c, 0)),
            pl.BlockSpec((1, 8, D), lambda b, c: (b, 0, 0)),
            pl.BlockSpec((1, 8, D), lambda b, c: (b, 0, 0)),
        ),
        scratch_shapes=[pltpu.VMEM((L + 8, D), F32), pltpu.VMEM((8, D), F32)],
        compiler_params=_params(("arbitrary", "arbitrary")),
        name="lru_prompt",
    )(x_p, g, win, cw, cb, wa, ba, wx, bx, lam, wout)


def _lru_sample_body(x_ref, g_ref, win_ref, cw_ref, cb_ref, wa_ref, ba_ref, wx_ref, bx_ref,
                     lam_ref, wout_ref, cs_ref, h0_ref, y_ref, conv_ref, h_ref,
                     xc_scr, *, Bs, Ls):
    Ts, D = x_ref.shape
    x = x_ref[...]
    hn = _rms(x, g_ref[...]).astype(BF16)
    gate = _gelu_tanh(_dot(hn, win_ref[:, :D]))
    x_br = _dot(hn, win_ref[:, D:])
    xc_scr[:, 0:8, :] = cs_ref[...]
    xc_scr[:, 8:8 + Ls, :] = x_br.reshape(Bs, Ls, D)
    conv = cb_ref[...] + xc_scr[:, 5:5 + Ls, :] * cw_ref[0:1, :]
    for j in range(1, CONV_W):
        conv = conv + xc_scr[:, 5 + j:5 + j + Ls, :] * cw_ref[j:j + 1, :]
    conv = conv.reshape(Ts, D)
    a, mult, gi = _lru_gates(conv, wa_ref, wx_ref, ba_ref[...], bx_ref[...], lam_ref[...])
    bvec = mult * gi + a * h0_ref[...].reshape(Ts, D)
    hs = _seg_scan(a, bvec, Ls)
    y_ref[...] = x + _dot((hs * gate).astype(BF16), wout_ref[...])
    h_ref[...] = hs.reshape(Bs, Ls, D)[:, Ls - 8:Ls, :]
    conv_ref[...] = xc_scr[:, Ls:Ls + 8, :]


def _lru_sample(x_s, Bs, Ls, g, win, cw, cb, wa, ba, wx, bx, lam, wout, cs, h0):
    Ts, D = x_s.shape
    assert Ls % 8 == 0
    body = functools.partial(_lru_sample_body, Bs=Bs, Ls=Ls)
    return pl.pallas_call(
        body,
        out_shape=(jax.ShapeDtypeStruct((Ts, D), F32),
                   jax.ShapeDtypeStruct((Bs, 8, D), F32),
                   jax.ShapeDtypeStruct((Bs, 8, D), F32)),
        grid=(1,),
        in_specs=[
            _const_spec((Ts, D)),
            _const_spec((1, D)), _const_spec(win.shape), _const_spec(cw.shape), _const_spec((1, D)),
            _const_spec(wa.shape), _const_spec((1, D)), _const_spec(wx.shape), _const_spec((1, D)),
            _const_spec((1, D)), _const_spec(wout.shape),
            _const_spec((Bs, 8, D)), _const_spec((Bs, Ls, D)),
        ],
        out_specs=(
            pl.BlockSpec((Ts, D), lambda i: (0, 0)),
            pl.BlockSpec((Bs, 8, D), lambda i: (0, 0, 0)),
            pl.BlockSpec((Bs, 8, D), lambda i: (0, 0, 0)),
        ),
        scratch_shapes=[pltpu.VMEM((Bs, Ls + 8, D), F32)],
        compiler_params=_params(("arbitrary",)),
        name="lru_sample",
    )(x_s, g, win, cw, cb, wa, ba, wx, bx, lam, wout, cs, h0)


def _router_body(x_ref, g_ref, wc_ref, bias_ref, cin_ref, text_ref, meta_ref, cnt_ref, run_scr):
    i = pl.program_id(0)
    tm, D = x_ref.shape

    @pl.when(i == 0)
    def _():
        run_scr[...] = cin_ref[...]

    t = _rms(x_ref[...], g_ref[...])
    logits = _dot(t.astype(BF16), wc_ref[...]) + bias_ref[...]
    lane_i = lax.broadcasted_iota(jnp.int32, (tm, LANES), 1)
    lane = lane_i.astype(F32)

    def first_argmax(v):
        m = jnp.max(v, axis=-1, keepdims=True)
        return m, jnp.min(jnp.where(v == m, lane, 1e6), axis=-1, keepdims=True)

    gmask = lane_i < MOE_GROUPS
    gl = jnp.where(gmask, logits, NEG_BIG)
    mg, g_idx = first_argmax(gl)
    p_g = 1.0 / jnp.sum(jnp.where(gmask, jnp.exp(gl - mg), 0.0), axis=-1, keepdims=True)
    e_lane = lane_i - MOE_GROUPS
    lane_grp = (e_lane >> 2).astype(F32)
    emask = (e_lane >= 0) & (e_lane < MOE_EXPERTS) & (lane_grp == g_idx)
    el = jnp.where(emask, logits, NEG_BIG)
    m1, i1 = first_argmax(el)
    m2, i2 = first_argmax(jnp.where(lane == i1, NEG_BIG, el))
    e21 = jnp.exp(m2 - m1)
    w1 = p_g / (1.0 + e21)
    w2 = p_g * e21 / (1.0 + e21)
    first_lower = i1 < i2
    j_lo = jnp.minimum(i1, i2) - (MOE_GROUPS + MOE_PER_GROUP * g_idx)
    j_hi = jnp.maximum(i1, i2) - (MOE_GROUPS + MOE_PER_GROUP * g_idx)
    pair = 0.5 * (j_lo * (7.0 - j_lo)) + (j_hi - j_lo - 1.0)
    cls = g_idx * N_PAIRS + pair
    w_lo = jnp.where(first_lower, w1, w2)
    w_hi = jnp.where(first_lower, w2, w1)

    onehot = (lane == cls).astype(F32)
    ri = lax.broadcasted_iota(jnp.int32, (tm, tm), 0)
    ci = lax.broadcasted_iota(jnp.int32, (tm, tm), 1)
    tri = (ci < ri).astype(BF16)
    prefix = _dot(tri, onehot.astype(BF16))
    rank = jnp.sum(onehot * (prefix + run_scr[...]), axis=-1, keepdims=True)
    run_scr[...] = run_scr[...] + jnp.sum(onehot, axis=0, keepdims=True)

    text_ref[:, :D] = t
    text_ref[:, D:] = jnp.where(lane_i == 0, w_lo, jnp.where(lane_i == 1, w_hi, 0.0))
    meta_ref[...] = jnp.where(lane_i == 0, cls, jnp.where(lane_i == 1, rank, 0.0))
    cnt_ref[...] = run_scr[...]


def _router(x, g, wc, bias, cnt_in):
    T, D = x.shape
    tm = _pick_tile(T, ROUTE_TILE)
    return pl.pallas_call(
        _router_body,
        out_shape=(jax.ShapeDtypeStruct((T, D + EXTRA), F32),
                   jax.ShapeDtypeStruct((T, LANES), F32),
                   jax.ShapeDtypeStruct((1, LANES), F32)),
        grid=(T // tm,),
        in_specs=[
            pl.BlockSpec((tm, D), lambda i: (i, 0)),
            _const_spec((1, D)), _const_spec(wc.shape), _const_spec((1, LANES)),
            _const_spec((1, LANES)),
        ],
        out_specs=(
            pl.BlockSpec((tm, D + EXTRA), lambda i: (i, 0)),
            pl.BlockSpec((tm, LANES), lambda i: (i, 0)),
            pl.BlockSpec((1, LANES), lambda i: (0, 0)),
        ),
        scratch_shapes=[pltpu.VMEM((1, LANES), F32)],
        compiler_params=_params(("arbitrary",)),
        name="moe_router",
    )(x, g, wc, bias, cnt_in)


def _dispatch_body(dest_ref, t_ref, xs_in_ref, xs_ref, sem):
    del xs_in_ref
    td = t_ref.shape[0]

    def row_copy(r):
        return pltpu.make_async_copy(t_ref.at[pl.ds(r, 1), :],
                                     xs_ref.at[pl.ds(dest_ref[0, 0, r], 1), :], sem)

    def start(r, carry):
        row_copy(r).start()
        return carry

    def wait(r, carry):
        row_copy(r).wait()
        return carry

    lax.fori_loop(0, td, start, 0)
    lax.fori_loop(0, td, wait, 0)


def _dispatch(t_ext, dest3, xs):
    nb, _, td = dest3.shape
    W = t_ext.shape[1]
    return pl.pallas_call(
        _dispatch_body,
        out_shape=jax.ShapeDtypeStruct(xs.shape, F32),
        grid=(nb,),
        in_specs=[
            pl.BlockSpec((1, 1, td), lambda i: (i, 0, 0), memory_space=pltpu.SMEM),
            pl.BlockSpec((td, W), lambda i: (i, 0)),
            pl.BlockSpec(memory_space=pl.ANY),
        ],
        out_specs=pl.BlockSpec(memory_space=pl.ANY),
        scratch_shapes=[pltpu.SemaphoreType.DMA(())],
        input_output_aliases={2: 0},
        compiler_params=_params(("arbitrary",)),
        name="moe_dispatch",
    )(dest3, t_ext, xs)


def _ffn_body(ea_ref, eb_ref, nused_ref, xs_ref, wga_ref, wua_ref, wda_ref,
              wgb_ref, wub_ref, wdb_ref, ys_ref):
    del ea_ref, eb_ref
    i = pl.program_id(0)
    D = ys_ref.shape[1]

    @pl.when(i < nused_ref[0])
    def _():
        x = xs_ref[:, :D].astype(BF16)
        w_a = xs_ref[:, D:D + 1]
        w_b = xs_ref[:, D + 1:D + 2]

        def expert(wg, wu, wd):
            gt = _dot(x, wg[0])
            hdn = (gt * jax.nn.sigmoid(gt)) * _dot(x, wu[0])
            return _dot(hdn.astype(BF16), wd[0])

        ys_ref[...] = (w_a * expert(wga_ref, wua_ref, wda_ref)
                       + w_b * expert(wgb_ref, wub_ref, wdb_ref))

    @pl.when(i >= nused_ref[0])
    def _():
        ys_ref[...] = jnp.zeros_like(ys_ref)


def _ffn(xs, ea, eb, nused, wg, wu, wd):
    P, W = xs.shape
    D = W - EXTRA
    ff = wg.shape[2]
    tg = FFN_TILE
    wspec_a = lambda shape: pl.BlockSpec(shape, lambda i, ea, eb, nu: (ea[i], 0, 0))
    wspec_b = lambda shape: pl.BlockSpec(shape, lambda i, ea, eb, nu: (eb[i], 0, 0))
    grid_spec = pltpu.PrefetchScalarGridSpec(
        num_scalar_prefetch=3,
        grid=(P // tg,),
        in_specs=[
            pl.BlockSpec((tg, W), lambda i, ea, eb, nu: (i, 0)),
            wspec_a((1, D, ff)), wspec_a((1, D, ff)), wspec_a((1, ff, D)),
            wspec_b((1, D, ff)), wspec_b((1, D, ff)), wspec_b((1, ff, D)),
        ],
        out_specs=pl.BlockSpec((tg, D), lambda i, ea, eb, nu: (i, 0)),
    )
    return pl.pallas_call(
        _ffn_body,
        out_shape=jax.ShapeDtypeStruct((P, D), F32),
        grid_spec=grid_spec,
        compiler_params=_params(("arbitrary",)),
        name="moe_ffn",
    )(ea, eb, nused, xs, wg, wu, wd, wg, wu, wd)


def _combine_body(dest_ref, x_ref, gfin_ref, ys_ref, o_ref, ybuf, sem, *, final_norm):
    tc = x_ref.shape[0]

    def row_copy(r):
        return pltpu.make_async_copy(ys_ref.at[pl.ds(dest_ref[0, 0, r], 1), :],
                                     ybuf.at[pl.ds(r, 1), :], sem)

    def start(r, carry):
        row_copy(r).start()
        return carry

    def wait(r, carry):
        row_copy(r).wait()
        return carry

    lax.fori_loop(0, tc, start, 0)
    lax.fori_loop(0, tc, wait, 0)
    y = x_ref[...] + ybuf[...]
    if final_norm:
        y = _rms(y, gfin_ref[...])
    o_ref[...] = y


def _combine(x, dest3, ys, gfin, final_norm):
    T, D = x.shape
    nb, _, tc = dest3.shape
    body = functools.partial(_combine_body, final_norm=final_norm)
    return pl.pallas_call(
        body,
        out_shape=jax.ShapeDtypeStruct((T, D), F32),
        grid=(nb,),
        in_specs=[
            pl.BlockSpec((1, 1, tc), lambda i: (i, 0, 0), memory_space=pltpu.SMEM),
            pl.BlockSpec((tc, D), lambda i: (i, 0)),
            _const_spec((1, D)),
            pl.BlockSpec(memory_space=pl.ANY),
        ],
        out_specs=pl.BlockSpec((tc, D), lambda i: (i, 0)),
        scratch_shapes=[pltpu.VMEM((tc, D), F32), pltpu.SemaphoreType.DMA(())],
        compiler_params=_params(("arbitrary",)),
        name="moe_combine",
    )(dest3, x, gfin, ys)


def _moe_tables(metas, counts, P):
    tg = FFN_TILE
    cnt = counts[0, :N_CLASSES].astype(jnp.int32)
    padded = ((cnt + tg - 1) // tg) * tg
    ends = jnp.cumsum(padded)
    offs = ends - padded
    dests = [jnp.take(offs, m[:, 0].astype(jnp.int32)) + m[:, 1].astype(jnp.int32) for m in metas]
    nused = (ends[-1] // tg).astype(jnp.int32)
    tile_start = jnp.arange(P // tg, dtype=jnp.int32) * tg
    tile_cls = jnp.minimum(jnp.searchsorted(ends, tile_start, side="right"),
                           N_CLASSES - 1).astype(jnp.int32)
    grp = tile_cls // N_PAIRS
    pair = tile_cls % N_PAIRS
    ea = grp * MOE_PER_GROUP + jnp.take(jnp.asarray(_PAIR_LO), pair)
    eb = grp * MOE_PER_GROUP + jnp.take(jnp.asarray(_PAIR_HI), pair)
    return dests, ea, eb, nused.reshape(1)


def _moe(xs_in, g, w_group, b_group, w_router, b_router, wg, wu, wd, gfin, final_norm):
    D = xs_in[0].shape[1]
    T = sum(x.shape[0] for x in xs_in)
    wcat = jnp.zeros((D, LANES), F32)
    wcat = wcat.at[:, :MOE_GROUPS].set(w_group).at[:, MOE_GROUPS:MOE_GROUPS + MOE_EXPERTS].set(w_router)
    bias = jnp.zeros((1, LANES), F32)
    bias = bias.at[0, :MOE_GROUPS].set(b_group).at[0, MOE_GROUPS:MOE_GROUPS + MOE_EXPERTS].set(b_router)
    wc = wcat.astype(BF16)

    counts = jnp.zeros((1, LANES), F32)
    texts, metas = [], []
    for x in xs_in:
        t_ext, meta, counts = _router(x, g, wc, bias, counts)
        texts.append(t_ext)
        metas.append(meta)

    tg = FFN_TILE
    P = ((T + tg - 1) // tg + N_CLASSES) * tg
    dests, ea, eb, nused = _moe_tables(metas, counts, P)
    dest3s = []
    xs = jnp.zeros((P, D + EXTRA), F32)
    for x, t_ext, dest in zip(xs_in, texts, dests):
        tmove = _pick_tile(x.shape[0], MOVE_TILE)
        dest3s.append(dest.reshape(x.shape[0] // tmove, 1, tmove))
        xs = _dispatch(t_ext, dest3s[-1], xs)
    ys = _ffn(xs, ea, eb, nused, wg, wu, wd)
    return [_combine(x, dest3, ys, gfin, final_norm) for x, dest3 in zip(xs_in, dest3s)]


def _rope_tables(pos, dk):
    half = dk // 2
    inv = 1.0 / (ROPE_BASE ** jnp.linspace(0.0, 1.0, half, dtype=F32))
    ang = pos.astype(F32)[:, None] * inv[None, :]
    return jnp.cos(ang), jnp.sin(ang)


def kernel(x_prompt, x_sample, state_ret, state_conv, state_lru, norm_mix_g, norm_ffn_g, norm_final_g, ret_w_in, ret_w_out, ret_norm_g, lru_w_in, lru_conv_w, lru_conv_b, lru_w_a, lru_b_a, lru_w_x, lru_b_x, lru_lambda, lru_w_out, moe_w_group, moe_b_group, moe_w_router, moe_b_router, moe_w_gate, moe_w_up, moe_w_down):
    B, S, D = x_prompt.shape
    Bs, Ls, _ = x_sample.shape
    Tp, Ts = B * S, Bs * Ls
    assert norm_mix_g.shape[0] == 2
    dk = D // RET_HEADS
    row = lambda v: v.reshape(1, -1)
    gfin = row(norm_final_g)

    cos_p, sin_p = _rope_tables(jnp.arange(S), dk)
    cos_s, sin_s = _rope_tables(PAST_LEN + jnp.arange(Ls), dk)
    ret_args = (row(norm_mix_g[0]), ret_w_in[0].astype(BF16), ret_w_out[0].astype(BF16),
                row(ret_norm_g[0]))
    xp, ret_p = _ret_prompt(x_prompt.reshape(Tp, D), B, S, *ret_args, cos_p, sin_p)
    xs, ret_s = _ret_sample(x_sample.reshape(Ts, D), Bs, Ls, *ret_args, cos_s, sin_s, state_ret[0])
    xp, xs = _moe([xp, xs], row(norm_ffn_g[0]), moe_w_group[0], moe_b_group[0], moe_w_router[0],
                  moe_b_router[0], moe_w_gate[0].astype(BF16), moe_w_up[0].astype(BF16),
                  moe_w_down[0].astype(BF16), gfin, False)

    lru_args = (row(norm_mix_g[1]), lru_w_in[0].astype(BF16), lru_conv_w[0], row(lru_conv_b[0]),
                lru_w_a[0].astype(BF16), row(lru_b_a[0]), lru_w_x[0].astype(BF16), row(lru_b_x[0]),
                row(lru_lambda[0]), lru_w_out[0].astype(BF16))
    cs8 = jnp.pad(state_conv[0], ((0, 0), (8 - (CONV_W - 1), 0), (0, 0)))
    h0_rows = jnp.pad(state_lru[0].reshape(Bs, 1, D), ((0, 0), (0, Ls - 1), (0, 0)))
    xp, conv_p, lru_p = _lru_prompt(xp, B, S, *lru_args)
    xs, conv_s, lru_s = _lru_sample(xs, Bs, Ls, *lru_args, cs8, h0_rows)
    y_p, y_s = _moe([xp, xs], row(norm_ffn_g[1]), moe_w_group[1], moe_b_group[1], moe_w_router[1],
                    moe_b_router[1], moe_w_gate[1].astype(BF16), moe_w_up[1].astype(BF16),
                    moe_w_down[1].astype(BF16), gfin, True)

    return (y_p.reshape(B, S, D), y_s.reshape(Bs, Ls, D),
            ret_p[None], conv_p[None, :, 8 - (CONV_W - 1):], lru_p[None, :, 7],
            ret_s[None], conv_s[None, :, 8 - (CONV_W - 1):], lru_s[None, :, 7])
```

```python
import functools
import math

import numpy as np
import jax
import jax.numpy as jnp
from jax import lax
from jax.experimental import pallas as pl
from jax.experimental.pallas import tpu as pltpu

F32 = jnp.float32
BF16 = jnp.bfloat16

RET_HEADS = 4
RET_CHUNK = 64
ROPE_BASE = 10000.0
CONV_W = 4
LRU_BLOCKS = 4
LRU_C = 8.0
MOE_GROUPS = 4
MOE_PER_GROUP = 4
MOE_EXPERTS = MOE_GROUPS * MOE_PER_GROUP
NORM_EPS = 1e-6
PAST_LEN = 2048

LANES = 128
SUBLANES = 8
VMEM_LIMIT_BYTES = 56 * 1024 * 1024

RET_TILE = 512
RET_SCORE_BLOCK = 256
LRU_TILE = 512
ROUTE_TILE = 512
MOVE_TILE = 256
ROW_UNROLL = 8
FFN_TILE = 256
N_PAIRS = 6
N_CLASSES = MOE_GROUPS * N_PAIRS
EXTRA = LANES
NEG_BIG = -1e30

_PAIR_LO = np.array([0, 0, 0, 1, 1, 2], np.int32)
_PAIR_HI = np.array([1, 2, 3, 2, 3, 3], np.int32)


def _log_gammas():
    return [math.log1p(-(2.0 ** (-5.0 - h))) for h in range(RET_HEADS)]


def _rms(x, g):
    return x * lax.rsqrt(jnp.mean(x * x, axis=-1, keepdims=True) + NORM_EPS) * g


def _dot(a, b):
    return jnp.dot(a, b, preferred_element_type=F32)


def _dot_nt(a, b):
    return lax.dot_general(a, b, (((1,), (1,)), ((), ())), preferred_element_type=F32)


def _dot_tn(a, b):
    return lax.dot_general(a, b, (((0,), (0,)), ((), ())), preferred_element_type=F32)


def _const_spec(shape):
    nd = len(shape)
    return pl.BlockSpec(shape, lambda *_: (0,) * nd, pipeline_mode=pl.Buffered(1))


def _params(sem):
    return pltpu.CompilerParams(dimension_semantics=sem, vmem_limit_bytes=VMEM_LIMIT_BYTES)


def _pick_tile(n, pref):
    t = pref
    while n % t:
        t //= 2
    assert t >= SUBLANES, (n, pref)
    return t


def _decay_matrix(L, chunk, lg):
    ii = lax.broadcasted_iota(jnp.int32, (L, L), 0)
    jj = lax.broadcasted_iota(jnp.int32, (L, L), 1)
    diff = (ii - jj).astype(F32)
    same_chunk = (ii // chunk) == (jj // chunk)
    return jnp.where(same_chunk & (ii >= jj), jnp.exp(jnp.maximum(diff, 0.0) * lg), 0.0)


def _rotary(t, cos, sin):
    half = t.shape[-1] // 2
    t1, t2 = t[:, :half], t[:, half:]
    return jnp.concatenate([t1 * cos - t2 * sin, t1 * sin + t2 * cos], axis=-1)


def _ret_head(hn, cos, sin, decay, s0, win_ref, ng_ref, hd, L, chunk, dk, dv, lg):
    qk, vv = RET_HEADS * dk, RET_HEADS * dv
    q = _dot(hn, win_ref[:, hd * dk:(hd + 1) * dk])
    k = _dot(hn, win_ref[:, qk + hd * dk:qk + (hd + 1) * dk])
    v = _dot(hn, win_ref[:, 2 * qk + hd * dv:2 * qk + (hd + 1) * dv])
    gt = _dot(hn, win_ref[:, 2 * qk + vv + hd * dv:2 * qk + vv + (hd + 1) * dv])
    qr = _rotary(q, cos, sin)
    kr = _rotary(k, cos, sin) * (dk ** -0.5)
    idx = (lax.broadcasted_iota(jnp.int32, (L, 1), 0) % chunk).astype(F32)
    qd = (qr * jnp.exp((idx + 1.0) * lg)).astype(BF16)
    kd = (kr * jnp.exp((chunk - 1.0 - idx) * lg)).astype(BF16)
    qb, kb, vb = qr.astype(BF16), kr.astype(BF16), v.astype(BF16)
    core = decay.shape[0]
    s, parts = s0, []
    for r0 in range(0, L, core):
        blk = slice(r0, r0 + core)
        scores = _dot_nt(qb[blk], kb[blk]) * decay
        intra = _dot(scores.astype(BF16), vb[blk])
        for c in range(core // chunk):
            rows = slice(r0 + c * chunk, r0 + (c + 1) * chunk)
            parts.append(intra[c * chunk:(c + 1) * chunk] + _dot(qd[rows], s.astype(BF16)))
            s = math.exp(chunk * lg) * s + _dot_tn(kd[rows], vb[rows])
    o = parts[0] if len(parts) == 1 else jnp.concatenate(parts, axis=0)
    o = o * lax.rsqrt(jnp.mean(o * o, axis=-1, keepdims=True) + NORM_EPS)
    o = o * ng_ref[:, hd * dv:(hd + 1) * dv]
    og = (gt * jax.nn.sigmoid(gt)) * o
    return og, s


def _ret_prompt_body(x_ref, g_ref, win_ref, wout_ref, ng_ref, cos_ref, sin_ref,
                     y_ref, s_ref, decay_scr, og_scr, *, L, chunk, dk, dv):
    b, c = pl.program_id(0), pl.program_id(1)
    lgs = _log_gammas()

    @pl.when((b == 0) & (c == 0))
    def _():
        for hd in range(RET_HEADS):
            decay_scr[hd] = _decay_matrix(decay_scr.shape[1], chunk, lgs[hd])

    @pl.when(c == 0)
    def _():
        s_ref[...] = jnp.zeros_like(s_ref)

    x = x_ref[...]
    hn = _rms(x, g_ref[...]).astype(BF16)
    cos, sin = cos_ref[...], sin_ref[...]
    for hd in range(RET_HEADS):
        og, s_new = _ret_head(hn, cos, sin, decay_scr[hd], s_ref[0, hd], win_ref, ng_ref,
                              hd, L, chunk, dk, dv, lgs[hd])
        s_ref[0, hd] = s_new
        og_scr[:, hd * dv:(hd + 1) * dv] = og.astype(BF16)
    y_ref[...] = x + _dot(og_scr[...], wout_ref[...])


def _ret_prompt(x_p, B, S, g, win, wout, ng, cos, sin):
    Tp, D = x_p.shape
    dk, dv = D // RET_HEADS, 2 * D // RET_HEADS
    chunk = min(S, RET_CHUNK)
    L = _pick_tile(S, RET_TILE)
    core = min(L, RET_SCORE_BLOCK)
    assert L % core == 0 and core % chunk == 0
    nc = S // L
    body = functools.partial(_ret_prompt_body, L=L, chunk=chunk, dk=dk, dv=dv)
    return pl.pallas_call(
        body,
        out_shape=(jax.ShapeDtypeStruct((Tp, D), F32),
                   jax.ShapeDtypeStruct((B, RET_HEADS, dk, dv), F32)),
        grid=(B, nc),
        in_specs=[
            pl.BlockSpec((L, D), lambda b, c: (b * nc + c, 0)),
            _const_spec((1, D)),
            _const_spec(win.shape),
            _const_spec(wout.shape),
            _const_spec((1, RET_HEADS * dv)),
            pl.BlockSpec((L, dk // 2), lambda b, c: (c, 0)),
            pl.BlockSpec((L, dk // 2), lambda b, c: (c, 0)),
        ],
        out_specs=(
            pl.BlockSpec((L, D), lambda b, c: (b * nc + c, 0)),
            pl.BlockSpec((1, RET_HEADS, dk, dv), lambda b, c: (b, 0, 0, 0)),
        ),
        scratch_shapes=[pltpu.VMEM((RET_HEADS, core, core), F32),
                        pltpu.VMEM((L, RET_HEADS * dv), BF16)],
        compiler_params=_params(("arbitrary", "arbitrary")),
        name="ret_prompt",
    )(x_p, g, win, wout, ng, cos, sin)


def _ret_sample_body(x_ref, g_ref, win_ref, wout_ref, ng_ref, cos_ref, sin_ref, s0_ref,
                     y_ref, s_ref, hn_scr, og_scr, *, L, dk, dv):
    b, nb = pl.program_id(0), pl.num_programs(0)
    lgs = _log_gammas()

    @pl.when(b == 0)
    def _():
        hn_scr[...] = _rms(x_ref[...], g_ref[...]).astype(BF16)

    row0 = pl.multiple_of(b * L, L)
    hn = hn_scr[pl.ds(row0, L), :]
    cos, sin = cos_ref[...], sin_ref[...]
    for hd in range(RET_HEADS):
        og, s_new = _ret_head(hn, cos, sin, _decay_matrix(L, L, lgs[hd]), s0_ref[0, hd], win_ref,
                              ng_ref, hd, L, L, dk, dv, lgs[hd])
        s_ref[0, hd] = s_new
        og_scr[pl.ds(row0, L), hd * dv:(hd + 1) * dv] = og.astype(BF16)

    @pl.when(b == nb - 1)
    def _():
        y_ref[...] = x_ref[...] + _dot(og_scr[...], wout_ref[...])


def _ret_sample(x_s, Bs, Ls, g, win, wout, ng, cos, sin, s0):
    Ts, D = x_s.shape
    dk, dv = D // RET_HEADS, 2 * D // RET_HEADS
    assert Ls <= RET_CHUNK
    body = functools.partial(_ret_sample_body, L=Ls, dk=dk, dv=dv)
    return pl.pallas_call(
        body,
        out_shape=(jax.ShapeDtypeStruct((Ts, D), F32),
                   jax.ShapeDtypeStruct((Bs, RET_HEADS, dk, dv), F32)),
        grid=(Bs,),
        in_specs=[
            _const_spec((Ts, D)),
            _const_spec((1, D)),
            _const_spec(win.shape),
            _const_spec(wout.shape),
            _const_spec((1, RET_HEADS * dv)),
            _const_spec((Ls, dk // 2)),
            _const_spec((Ls, dk // 2)),
            pl.BlockSpec((1, RET_HEADS, dk, dv), lambda b: (b, 0, 0, 0)),
        ],
        out_specs=(
            pl.BlockSpec((Ts, D), lambda b: (0, 0)),
            pl.BlockSpec((1, RET_HEADS, dk, dv), lambda b: (b, 0, 0, 0)),
        ),
        scratch_shapes=[pltpu.VMEM((Ts, D), BF16), pltpu.VMEM((Ts, RET_HEADS * dv), BF16)],
        compiler_params=_params(("arbitrary",)),
        name="ret_sample",
    )(x_s, g, win, wout, ng, cos, sin, s0)


def _gelu_tanh(x):
    return 0.5 * x * (1.0 + jnp.tanh(math.sqrt(2.0 / math.pi) * (x + 0.044715 * (x * x * x))))


def _softplus(z):
    return jnp.maximum(z, 0.0) + jnp.log1p(jnp.exp(-jnp.abs(z)))


def _lru_gates(conv, wa_ref, wx_ref, ba, bx, lam):
    D = conv.shape[1]
    bw = D // LRU_BLOCKS
    ra, ix = [], []
    for n in range(LRU_BLOCKS):
        cb = conv[:, n * bw:(n + 1) * bw].astype(BF16)
        ra.append(_dot(cb, wa_ref[n]))
        ix.append(_dot(cb, wx_ref[n]))
    r = jax.nn.sigmoid(jnp.concatenate(ra, axis=-1) + ba)
    i = jax.nn.sigmoid(jnp.concatenate(ix, axis=-1) + bx)
    log_a = (-LRU_C) * r * _softplus(-lam)
    a = jnp.exp(log_a)
    mult = jnp.sqrt(jnp.tanh(-log_a) * (1.0 + a * a))
    return a, mult, i * conv


def _seg_scan(a, b, seg):
    R = a.shape[0]
    row = lax.broadcasted_iota(jnp.int32, (R, 1), 0) % seg
    k = 1
    while k < seg:
        keep = row >= k
        a_sh = jnp.where(keep, pltpu.roll(a, k, axis=0), 1.0)
        b_sh = jnp.where(keep, pltpu.roll(b, k, axis=0), 0.0)
        b = a * b_sh + b
        a = a * a_sh
        k *= 2
    return b


def _carry_scan(a, b, h_prev):
    L, D = a.shape
    G = L // SUBLANES
    a3, b3 = a.reshape(G, SUBLANES, D), b.reshape(G, SUBLANES, D)
    sub = lax.broadcasted_iota(jnp.int32, (SUBLANES, D), 0)[None]
    k = 1
    while k < SUBLANES:
        keep = sub >= k
        a_sh = jnp.where(keep, pltpu.roll(a3, k, axis=1), 1.0)
        b_sh = jnp.where(keep, pltpu.roll(b3, k, axis=1), 0.0)
        b3 = a3 * b_sh + b3
        a3 = a3 * a_sh
        k *= 2
    carry, outs = h_prev, []
    for j in range(G):
        r = a3[j] * carry + b3[j]
        outs.append(r)
        carry = r[SUBLANES - 1:SUBLANES, :]
    return jnp.concatenate(outs, axis=0)


def _lru_prompt_body(x_ref, g_ref, win_ref, cw_ref, cb_ref, wa_ref, ba_ref, wx_ref, bx_ref,
                     lam_ref, wout_ref, y_ref, conv_ref, h_ref, xc_scr, hprev_scr, *, L):
    c = pl.program_id(1)
    D = x_ref.shape[1]

    @pl.when(c == 0)
    def _():
        xc_scr[0:8, :] = jnp.zeros((8, D), F32)
        hprev_scr[...] = jnp.zeros_like(hprev_scr)

    x = x_ref[...]
    hn = _rms(x, g_ref[...]).astype(BF16)
    gate = _gelu_tanh(_dot(hn, win_ref[:, :D]))
    x_br = _dot(hn, win_ref[:, D:])
    xc_scr[8:8 + L, :] = x_br
    conv = cb_ref[...] + xc_scr[5:5 + L, :] * cw_ref[0:1, :]
    for j in range(1, CONV_W):
        conv = conv + xc_scr[5 + j:5 + j + L, :] * cw_ref[j:j + 1, :]
    a, mult, gi = _lru_gates(conv, wa_ref, wx_ref, ba_ref[...], bx_ref[...], lam_ref[...])
    row = lax.broadcasted_iota(jnp.int32, (L, 1), 0)
    mult = jnp.where((row == 0) & (c == 0), 1.0, mult)
    hs = _carry_scan(a, mult * gi, hprev_scr[7:8, :])
    y_ref[...] = x + _dot((hs * gate).astype(BF16), wout_ref[...])
    hprev_scr[...] = hs[L - 8:L, :]
    h_ref[0] = hs[L - 8:L, :]
    conv_ref[0] = x_br[L - 8:L, :]
    xc_scr[0:8, :] = x_br[L - 8:L, :]


def _lru_prompt(x_p, B, S, g, win, cw, cb, wa, ba, wx, bx, lam, wout):
    Tp, D = x_p.shape
    L = _pick_tile(S, LRU_TILE)
    nc = S // L
    body = functools.partial(_lru_prompt_body, L=L)
    return pl.pallas_call(
        body,
        out_shape=(jax.ShapeDtypeStruct((Tp, D), F32),
                   jax.ShapeDtypeStruct((B, 8, D), F32),
                   jax.ShapeDtypeStruct((B, 8, D), F32)),
        grid=(B, nc),
        in_specs=[
            pl.BlockSpec((L, D), lambda b, c: (b * nc + c, 0)),
            _const_spec((1, D)), _const_spec(win.shape), _const_spec(cw.shape), _const_spec((1, D)),
            _const_spec(wa.shape), _const_spec((1, D)), _const_spec(wx.shape), _const_spec((1, D)),
            _const_spec((1, D)), _const_spec(wout.shape),
        ],
        out_specs=(
            pl.BlockSpec((L, D), lambda b, c: (b * nc + c, 0)),
            pl.BlockSpec((1, 8, D), lambda b, c: (b, 0, 0)),
            pl.BlockSpec((1, 8, D), lambda b, c: (b, 0, 0)),
        ),
        scratch_shapes=[pltpu.VMEM((L + 8, D), F32), pltpu.VMEM((8, D), F32)],
        compiler_params=_params(("arbitrary", "arbitrary")),
        name="lru_prompt",
    )(x_p, g, win, cw, cb, wa, ba, wx, bx, lam, wout)


def _lru_sample_body(x_ref, g_ref, win_ref, cw_ref, cb_ref, wa_ref, ba_ref, wx_ref, bx_ref,
                     lam_ref, wout_ref, cs_ref, h0_ref, y_ref, conv_ref, h_ref,
                     xc_scr, *, Bs, Ls):
    Ts, D = x_ref.shape
    x = x_ref[...]
    hn = _rms(x, g_ref[...]).astype(BF16)
    gate = _gelu_tanh(_dot(hn, win_ref[:, :D]))
    x_br = _dot(hn, win_ref[:, D:])
    xc_scr[:, 0:8, :] = cs_ref[...]
    xc_scr[:, 8:8 + Ls, :] = x_br.reshape(Bs, Ls, D)
    conv = cb_ref[...] + xc_scr[:, 5:5 + Ls, :] * cw_ref[0:1, :]
    for j in range(1, CONV_W):
        conv = conv + xc_scr[:, 5 + j:5 + j + Ls, :] * cw_ref[j:j + 1, :]
    conv = conv.reshape(Ts, D)
    a, mult, gi = _lru_gates(conv, wa_ref, wx_ref, ba_ref[...], bx_ref[...], lam_ref[...])
    bvec = mult * gi + a * h0_ref[...].reshape(Ts, D)
    hs = _seg_scan(a, bvec, Ls)
    y_ref[...] = x + _dot((hs * gate).astype(BF16), wout_ref[...])
    h_ref[...] = hs.reshape(Bs, Ls, D)[:, Ls - 8:Ls, :]
    conv_ref[...] = xc_scr[:, Ls:Ls + 8, :]


def _lru_sample(x_s, Bs, Ls, g, win, cw, cb, wa, ba, wx, bx, lam, wout, cs, h0):
    Ts, D = x_s.shape
    assert Ls % 8 == 0
    body = functools.partial(_lru_sample_body, Bs=Bs, Ls=Ls)
    return pl.pallas_call(
        body,
        out_shape=(jax.ShapeDtypeStruct((Ts, D), F32),
                   jax.ShapeDtypeStruct((Bs, 8, D), F32),
                   jax.ShapeDtypeStruct((Bs, 8, D), F32)),
        grid=(1,),
        in_specs=[
            _const_spec((Ts, D)),
            _const_spec((1, D)), _const_spec(win.shape), _const_spec(cw.shape), _const_spec((1, D)),
            _const_spec(wa.shape), _const_spec((1, D)), _const_spec(wx.shape), _const_spec((1, D)),
            _const_spec((1, D)), _const_spec(wout.shape),
            _const_spec((Bs, 8, D)), _const_spec((Bs, Ls, D)),
        ],
        out_specs=(
            pl.BlockSpec((Ts, D), lambda i: (0, 0)),
            pl.BlockSpec((Bs, 8, D), lambda i: (0, 0, 0)),
            pl.BlockSpec((Bs, 8, D), lambda i: (0, 0, 0)),
        ),
        scratch_shapes=[pltpu.VMEM((Bs, Ls + 8, D), F32)],
        compiler_params=_params(("arbitrary",)),
        name="lru_sample",
    )(x_s, g, win, cw, cb, wa, ba, wx, bx, lam, wout, cs, h0)


def _router_body(x_ref, g_ref, wc_ref, bias_ref, cin_ref, text_ref, meta_ref, cnt_ref, run_scr):
    i = pl.program_id(0)
    tm, D = x_ref.shape

    @pl.when(i == 0)
    def _():
        run_scr[...] = cin_ref[...]

    t = _rms(x_ref[...], g_ref[...])
    logits = _dot(t.astype(BF16), wc_ref[...]) + bias_ref[...]
    lane_i = lax.broadcasted_iota(jnp.int32, (tm, LANES), 1)
    lane = lane_i.astype(F32)

    def first_argmax(v):
        m = jnp.max(v, axis=-1, keepdims=True)
        return m, jnp.min(jnp.where(v == m, lane, 1e6), axis=-1, keepdims=True)

    gmask = lane_i < MOE_GROUPS
    gl = jnp.where(gmask, logits, NEG_BIG)
    mg, g_idx = first_argmax(gl)
    p_g = 1.0 / jnp.sum(jnp.where(gmask, jnp.exp(gl - mg), 0.0), axis=-1, keepdims=True)
    e_lane = lane_i - MOE_GROUPS
    lane_grp = (e_lane >> 2).astype(F32)
    emask = (e_lane >= 0) & (e_lane < MOE_EXPERTS) & (lane_grp == g_idx)
    el = jnp.where(emask, logits, NEG_BIG)
    m1, i1 = first_argmax(el)
    m2, i2 = first_argmax(jnp.where(lane == i1, NEG_BIG, el))
    e21 = jnp.exp(m2 - m1)
    w1 = p_g / (1.0 + e21)
    w2 = p_g * e21 / (1.0 + e21)
    first_lower = i1 < i2
    j_lo = jnp.minimum(i1, i2) - (MOE_GROUPS + MOE_PER_GROUP * g_idx)
    j_hi = jnp.maximum(i1, i2) - (MOE_GROUPS + MOE_PER_GROUP * g_idx)
    pair = 0.5 * (j_lo * (7.0 - j_lo)) + (j_hi - j_lo - 1.0)
    cls = g_idx * N_PAIRS + pair
    w_lo = jnp.where(first_lower, w1, w2)
    w_hi = jnp.where(first_lower, w2, w1)

    onehot = (lane == cls).astype(F32)
    ri = lax.broadcasted_iota(jnp.int32, (tm, tm), 0)
    ci = lax.broadcasted_iota(jnp.int32, (tm, tm), 1)
    tri = (ci < ri).astype(BF16)
    prefix = _dot(tri, onehot.astype(BF16))
    rank = jnp.sum(onehot * (prefix + run_scr[...]), axis=-1, keepdims=True)
    run_scr[...] = run_scr[...] + jnp.sum(onehot, axis=0, keepdims=True)

    text_ref[:, :D] = t
    text_ref[:, D:] = jnp.where(lane_i == 0, w_lo, jnp.where(lane_i == 1, w_hi, 0.0))
    meta_ref[...] = jnp.where(lane_i == 0, cls, jnp.where(lane_i == 1, rank, 0.0))
    cnt_ref[...] = run_scr[...]


def _router(x, g, wc, bias, cnt_in):
    T, D = x.shape
    tm = _pick_tile(T, ROUTE_TILE)
    return pl.pallas_call(
        _router_body,
        out_shape=(jax.ShapeDtypeStruct((T, D + EXTRA), F32),
                   jax.ShapeDtypeStruct((T, LANES), F32),
                   jax.ShapeDtypeStruct((1, LANES), F32)),
        grid=(T // tm,),
        in_specs=[
            pl.BlockSpec((tm, D), lambda i: (i, 0)),
            _const_spec((1, D)), _const_spec(wc.shape), _const_spec((1, LANES)),
            _const_spec((1, LANES)),
        ],
        out_specs=(
            pl.BlockSpec((tm, D + EXTRA), lambda i: (i, 0)),
            pl.BlockSpec((tm, LANES), lambda i: (i, 0)),
            pl.BlockSpec((1, LANES), lambda i: (0, 0)),
        ),
        scratch_shapes=[pltpu.VMEM((1, LANES), F32)],
        compiler_params=_params(("arbitrary",)),
        name="moe_router",
    )(x, g, wc, bias, cnt_in)


def _start_rows(n, row_copy):
    assert n % ROW_UNROLL == 0

    def body(j, carry):
        for u in range(ROW_UNROLL):
            row_copy(j * ROW_UNROLL + u).start(priority=u % 2)
        return carry

    lax.fori_loop(0, n // ROW_UNROLL, body, 0)


def _dispatch_body(dest_ref, t_ref, xs_in_ref, xs_ref, sem):
    del xs_in_ref
    td = t_ref.shape[0]

    def row_copy(r):
        return pltpu.make_async_copy(t_ref.at[pl.ds(r, 1), :],
                                     xs_ref.at[pl.ds(dest_ref[0, 0, r], 1), :], sem)

    _start_rows(td, row_copy)
    pltpu.make_async_copy(t_ref, xs_ref.at[pl.ds(0, td), :], sem).wait()


def _dispatch(t_ext, dest3, xs):
    nb, _, td = dest3.shape
    W = t_ext.shape[1]
    return pl.pallas_call(
        _dispatch_body,
        out_shape=jax.ShapeDtypeStruct(xs.shape, F32),
        grid=(nb,),
        in_specs=[
            pl.BlockSpec((1, 1, td), lambda i: (i, 0, 0), memory_space=pltpu.SMEM),
            pl.BlockSpec((td, W), lambda i: (i, 0)),
            pl.BlockSpec(memory_space=pl.ANY),
        ],
        out_specs=pl.BlockSpec(memory_space=pl.ANY),
        scratch_shapes=[pltpu.SemaphoreType.DMA(())],
        input_output_aliases={2: 0},
        compiler_params=_params(("arbitrary",)),
        name="moe_dispatch",
    )(dest3, t_ext, xs)


def _ffn_body(ea_ref, eb_ref, nused_ref, xs_ref, wga_ref, wua_ref, wda_ref,
              wgb_ref, wub_ref, wdb_ref, ys_ref):
    del ea_ref, eb_ref
    i = pl.program_id(0)
    D = ys_ref.shape[1]

    @pl.when(i < nused_ref[0])
    def _():
        x = xs_ref[:, :D].astype(BF16)
        w_a = xs_ref[:, D:D + 1]
        w_b = xs_ref[:, D + 1:D + 2]

        def expert(wg, wu, wd):
            gt = _dot(x, wg[0])
            hdn = (gt * jax.nn.sigmoid(gt)) * _dot(x, wu[0])
            return _dot(hdn.astype(BF16), wd[0])

        ys_ref[...] = (w_a * expert(wga_ref, wua_ref, wda_ref)
                       + w_b * expert(wgb_ref, wub_ref, wdb_ref))

    @pl.when(i >= nused_ref[0])
    def _():
        ys_ref[...] = jnp.zeros_like(ys_ref)


def _ffn(xs, ea, eb, nused, wg, wu, wd):
    P, W = xs.shape
    D = W - EXTRA
    ff = wg.shape[2]
    tg = FFN_TILE
    wspec_a = lambda shape: pl.BlockSpec(shape, lambda i, ea, eb, nu: (ea[i], 0, 0))
    wspec_b = lambda shape: pl.BlockSpec(shape, lambda i, ea, eb, nu: (eb[i], 0, 0))
    grid_spec = pltpu.PrefetchScalarGridSpec(
        num_scalar_prefetch=3,
        grid=(P // tg,),
        in_specs=[
            pl.BlockSpec((tg, W), lambda i, ea, eb, nu: (i, 0)),
            wspec_a((1, D, ff)), wspec_a((1, D, ff)), wspec_a((1, ff, D)),
            wspec_b((1, D, ff)), wspec_b((1, D, ff)), wspec_b((1, ff, D)),
        ],
        out_specs=pl.BlockSpec((tg, D), lambda i, ea, eb, nu: (i, 0)),
    )
    return pl.pallas_call(
        _ffn_body,
        out_shape=jax.ShapeDtypeStruct((P, D), F32),
        grid_spec=grid_spec,
        compiler_params=_params(("arbitrary",)),
        name="moe_ffn",
    )(ea, eb, nused, xs, wg, wu, wd, wg, wu, wd)


def _combine_body(dest_ref, dnext_ref, x_ref, gfin_ref, ys_ref, o_ref, ybuf, sems, *, final_norm):
    i, n = pl.program_id(0), pl.num_programs(0)
    tc = x_ref.shape[0]

    def gather(idx_ref, slot):
        def row_copy(r):
            return pltpu.make_async_copy(ys_ref.at[pl.ds(idx_ref[0, 0, r], 1), :],
                                         ybuf.at[slot, pl.ds(r, 1), :], sems.at[slot])
        _start_rows(tc, row_copy)

    @pl.when(i == 0)
    def _():
        gather(dest_ref, 0)

    for slot in range(2):
        @pl.when((i + 1 < n) & ((i + 1) % 2 == slot))
        def _():
            gather(dnext_ref, slot)

    for slot in range(2):
        @pl.when(i % 2 == slot)
        def _():
            pltpu.make_async_copy(ys_ref.at[pl.ds(0, tc), :], ybuf.at[slot], sems.at[slot]).wait()
            y = x_ref[...] + ybuf[slot]
            if final_norm:
                y = _rms(y, gfin_ref[...])
            o_ref[...] = y


def _combine(x, dest3, ys, gfin, final_norm):
    T, D = x.shape
    nb, _, tc = dest3.shape
    body = functools.partial(_combine_body, final_norm=final_norm)
    return pl.pallas_call(
        body,
        out_shape=jax.ShapeDtypeStruct((T, D), F32),
        grid=(nb,),
        in_specs=[
            pl.BlockSpec((1, 1, tc), lambda i: (i, 0, 0), memory_space=pltpu.SMEM),
            pl.BlockSpec((1, 1, tc), lambda i: (jnp.minimum(i + 1, nb - 1), 0, 0),
                         memory_space=pltpu.SMEM),
            pl.BlockSpec((tc, D), lambda i: (i, 0)),
            _const_spec((1, D)),
            pl.BlockSpec(memory_space=pl.ANY),
        ],
        out_specs=pl.BlockSpec((tc, D), lambda i: (i, 0)),
        scratch_shapes=[pltpu.VMEM((2, tc, D), F32), pltpu.SemaphoreType.DMA((2,))],
        compiler_params=_params(("arbitrary",)),
        name="moe_combine",
    )(dest3, dest3, x, gfin, ys)


def _moe_tables(metas, counts, P):
    tg = FFN_TILE
    cnt = counts[0, :N_CLASSES].astype(jnp.int32)
    padded = ((cnt + tg - 1) // tg) * tg
    ends = jnp.cumsum(padded)
    offs = ends - padded
    dests = [jnp.take(offs, m[:, 0].astype(jnp.int32)) + m[:, 1].astype(jnp.int32) for m in metas]
    nused = (ends[-1] // tg).astype(jnp.int32)
    tile_start = jnp.arange(P // tg, dtype=jnp.int32) * tg
    tile_cls = jnp.minimum(jnp.sum((tile_start[:, None] >= ends[None, :]).astype(jnp.int32), axis=1),
                           N_CLASSES - 1)
    grp = tile_cls // N_PAIRS
    pair = tile_cls % N_PAIRS
    ea = grp * MOE_PER_GROUP + jnp.take(jnp.asarray(_PAIR_LO), pair)
    eb = grp * MOE_PER_GROUP + jnp.take(jnp.asarray(_PAIR_HI), pair)
    return dests, ea, eb, nused.reshape(1)


def _moe(xs_in, g, w_group, b_group, w_router, b_router, wg, wu, wd, gfin, final_norm):
    D = xs_in[0].shape[1]
    T = sum(x.shape[0] for x in xs_in)
    wcat = jnp.zeros((D, LANES), F32)
    wcat = wcat.at[:, :MOE_GROUPS].set(w_group).at[:, MOE_GROUPS:MOE_GROUPS + MOE_EXPERTS].set(w_router)
    bias = jnp.zeros((1, LANES), F32)
    bias = bias.at[0, :MOE_GROUPS].set(b_group).at[0, MOE_GROUPS:MOE_GROUPS + MOE_EXPERTS].set(b_router)
    wc = wcat.astype(BF16)

    counts = jnp.zeros((1, LANES), F32)
    texts, metas = [], []
    for x in xs_in:
        t_ext, meta, counts = _router(x, g, wc, bias, counts)
        texts.append(t_ext)
        metas.append(meta)

    tg = FFN_TILE
    P = ((T + tg - 1) // tg + N_CLASSES) * tg
    dests, ea, eb, nused = _moe_tables(metas, counts, P)
    dest3s = []
    xs = jnp.zeros((P, D + EXTRA), F32)
    for x, t_ext, dest in zip(xs_in, texts, dests):
        tmove = _pick_tile(x.shape[0], MOVE_TILE)
        dest3s.append(dest.reshape(x.shape[0] // tmove, 1, tmove))
        xs = _dispatch(t_ext, dest3s[-1], xs)
    ys = _ffn(xs, ea, eb, nused, wg, wu, wd)
    return [_combine(x, dest3, ys, gfin, final_norm) for x, dest3 in zip(xs_in, dest3s)]


def _rope_tables(pos, dk):
    half = dk // 2
    inv = 1.0 / (ROPE_BASE ** jnp.linspace(0.0, 1.0, half, dtype=F32))
    ang = pos.astype(F32)[:, None] * inv[None, :]
    return jnp.cos(ang), jnp.sin(ang)


def kernel(x_prompt, x_sample, state_ret, state_conv, state_lru, norm_mix_g, norm_ffn_g, norm_final_g, ret_w_in, ret_w_out, ret_norm_g, lru_w_in, lru_conv_w, lru_conv_b, lru_w_a, lru_b_a, lru_w_x, lru_b_x, lru_lambda, lru_w_out, moe_w_group, moe_b_group, moe_w_router, moe_b_router, moe_w_gate, moe_w_up, moe_w_down):
    B, S, D = x_prompt.shape
    Bs, Ls, _ = x_sample.shape
    Tp, Ts = B * S, Bs * Ls
    assert norm_mix_g.shape[0] == 2
    dk = D // RET_HEADS
    row = lambda v: v.reshape(1, -1)
    gfin = row(norm_final_g)

    cos_p, sin_p = _rope_tables(jnp.arange(S), dk)
    cos_s, sin_s = _rope_tables(PAST_LEN + jnp.arange(Ls), dk)
    ret_args = (row(norm_mix_g[0]), ret_w_in[0].astype(BF16), ret_w_out[0].astype(BF16),
                row(ret_norm_g[0]))
    xp, ret_p = _ret_prompt(x_prompt.reshape(Tp, D), B, S, *ret_args, cos_p, sin_p)
    xs, ret_s = _ret_sample(x_sample.reshape(Ts, D), Bs, Ls, *ret_args, cos_s, sin_s, state_ret[0])
    xp, xs = _moe([xp, xs], row(norm_ffn_g[0]), moe_w_group[0], moe_b_group[0], moe_w_router[0],
                  moe_b_router[0], moe_w_gate[0].astype(BF16), moe_w_up[0].astype(BF16),
                  moe_w_down[0].astype(BF16), gfin, False)

    lru_args = (row(norm_mix_g[1]), lru_w_in[0].astype(BF16), lru_conv_w[0], row(lru_conv_b[0]),
                lru_w_a[0].astype(BF16), row(lru_b_a[0]), lru_w_x[0].astype(BF16), row(lru_b_x[0]),
                row(lru_lambda[0]), lru_w_out[0].astype(BF16))
    cs8 = jnp.pad(state_conv[0], ((0, 0), (8 - (CONV_W - 1), 0), (0, 0)))
    h0_rows = jnp.pad(state_lru[0].reshape(Bs, 1, D), ((0, 0), (0, Ls - 1), (0, 0)))
    xp, conv_p, lru_p = _lru_prompt(xp, B, S, *lru_args)
    xs, conv_s, lru_s = _lru_sample(xs, Bs, Ls, *lru_args, cs8, h0_rows)
    y_p, y_s = _moe([xp, xs], row(norm_ffn_g[1]), moe_w_group[1], moe_b_group[1], moe_w_router[1],
                    moe_b_router[1], moe_w_gate[1].astype(BF16), moe_w_up[1].astype(BF16),
                    moe_w_down[1].astype(BF16), gfin, True)

    return (y_p.reshape(B, S, D), y_s.reshape(Bs, Ls, D),
            ret_p[None], conv_p[None, :, 8 - (CONV_W - 1):], lru_p[None, :, 7],
            ret_s[None], conv_s[None, :, 8 - (CONV_W - 1):], lru_s[None, :, 7])
```

```python
import functools
import math

import numpy as np
import jax
import jax.numpy as jnp
from jax import lax
from jax.experimental import pallas as pl
from jax.experimental.pallas import tpu as pltpu

F32 = jnp.float32
BF16 = jnp.bfloat16

RET_HEADS = 4
RET_CHUNK = 64
ROPE_BASE = 10000.0
CONV_W = 4
LRU_BLOCKS = 4
LRU_C = 8.0
MOE_GROUPS = 4
MOE_PER_GROUP = 4
MOE_EXPERTS = MOE_GROUPS * MOE_PER_GROUP
NORM_EPS = 1e-6
PAST_LEN = 2048

LANES = 128
SUBLANES = 8
VMEM_LIMIT_BYTES = 56 * 1024 * 1024

RET_TILE = 512
RET_SCORE_BLOCK = 256
LRU_TILE = 512
ROUTE_TILE = 512
MOVE_TILE = 512
FFN_TILE = 512
FFN_HALF = 256
N_PAIRS = 6
N_CLASSES = MOE_GROUPS * N_PAIRS
EXTRA = LANES
META_W = 8
NEG_BIG = -1e30

_PAIR_LO = np.array([0, 0, 0, 1, 1, 2], np.int32)
_PAIR_HI = np.array([1, 2, 3, 2, 3, 3], np.int32)


def _log_gammas():
    return [math.log1p(-(2.0 ** (-5.0 - h))) for h in range(RET_HEADS)]


def _rms(x, g):
    return x * lax.rsqrt(jnp.mean(x * x, axis=-1, keepdims=True) + NORM_EPS) * g


def _dot(a, b):
    return jnp.dot(a, b, preferred_element_type=F32)


def _dot_nt(a, b):
    return lax.dot_general(a, b, (((1,), (1,)), ((), ())), preferred_element_type=F32)


def _dot_tn(a, b):
    return lax.dot_general(a, b, (((0,), (0,)), ((), ())), preferred_element_type=F32)


def _const_spec(shape):
    nd = len(shape)
    return pl.BlockSpec(shape, lambda *_: (0,) * nd, pipeline_mode=pl.Buffered(1))


def _params(sem):
    return pltpu.CompilerParams(dimension_semantics=sem, vmem_limit_bytes=VMEM_LIMIT_BYTES)


def _pick_tile(n, pref):
    t = pref
    while n % t:
        t //= 2
    assert t >= SUBLANES, (n, pref)
    return t


def _decay_matrix(L, chunk, lg):
    ii = lax.broadcasted_iota(jnp.int32, (L, L), 0)
    jj = lax.broadcasted_iota(jnp.int32, (L, L), 1)
    diff = (ii - jj).astype(F32)
    same_chunk = (ii // chunk) == (jj // chunk)
    return jnp.where(same_chunk & (ii >= jj), jnp.exp(jnp.maximum(diff, 0.0) * lg), 0.0)


def _rotary(t, cos, sin):
    half = t.shape[-1] // 2
    t1, t2 = t[:, :half], t[:, half:]
    return jnp.concatenate([t1 * cos - t2 * sin, t1 * sin + t2 * cos], axis=-1)


def _ret_proj(hn, win_ref, hd, dk, dv):
    qk, vv = RET_HEADS * dk, RET_HEADS * dv
    q = _dot(hn, win_ref[:, hd * dk:(hd + 1) * dk])
    k = _dot(hn, win_ref[:, qk + hd * dk:qk + (hd + 1) * dk])
    v = _dot(hn, win_ref[:, 2 * qk + hd * dv:2 * qk + (hd + 1) * dv])
    gt = _dot(hn, win_ref[:, 2 * qk + vv + hd * dv:2 * qk + vv + (hd + 1) * dv])
    return q, k, v, gt


def _ret_head(q, k, v, gt, cos, sin, decay, s0, ng, chunk, lg):
    L, dk = q.shape
    qr = _rotary(q, cos, sin)
    kr = _rotary(k, cos, sin) * (dk ** -0.5)
    idx = (lax.broadcasted_iota(jnp.int32, (L, 1), 0) % chunk).astype(F32)
    qd = (qr * jnp.exp((idx + 1.0) * lg)).astype(BF16)
    kd = (kr * jnp.exp((chunk - 1.0 - idx) * lg)).astype(BF16)
    qb, kb, vb = qr.astype(BF16), kr.astype(BF16), v.astype(BF16)
    core = decay.shape[0]
    s, parts = s0, []
    for r0 in range(0, L, core):
        blk = slice(r0, r0 + core)
        scores = _dot_nt(qb[blk], kb[blk]) * decay
        intra = _dot(scores.astype(BF16), vb[blk])
        for c in range(core // chunk):
            rows = slice(r0 + c * chunk, r0 + (c + 1) * chunk)
            parts.append(intra[c * chunk:(c + 1) * chunk] + _dot(qd[rows], s.astype(BF16)))
            s = math.exp(chunk * lg) * s + _dot_tn(kd[rows], vb[rows])
    o = parts[0] if len(parts) == 1 else jnp.concatenate(parts, axis=0)
    o = o * lax.rsqrt(jnp.mean(o * o, axis=-1, keepdims=True) + NORM_EPS)
    og = (gt * jax.nn.sigmoid(gt)) * (o * ng)
    return og, s


def _ret_prompt_body(x_ref, g_ref, win_ref, wout_ref, ng_ref, cos_ref, sin_ref,
                     y_ref, s_ref, decay_scr, og_scr, *, L, chunk, dk, dv):
    b, c = pl.program_id(0), pl.program_id(1)
    lgs = _log_gammas()

    @pl.when((b == 0) & (c == 0))
    def _():
        for hd in range(RET_HEADS):
            decay_scr[hd] = _decay_matrix(decay_scr.shape[1], chunk, lgs[hd])

    @pl.when(c == 0)
    def _():
        s_ref[...] = jnp.zeros_like(s_ref)

    x = x_ref[...]
    hn = _rms(x, g_ref[...]).astype(BF16)
    cos, sin = cos_ref[...], sin_ref[...]
    for hd in range(RET_HEADS):
        og, s_new = _ret_head(*_ret_proj(hn, win_ref, hd, dk, dv), cos, sin, decay_scr[hd],
                              s_ref[0, hd], ng_ref[:, hd * dv:(hd + 1) * dv], chunk, lgs[hd])
        s_ref[0, hd] = s_new
        og_scr[:, hd * dv:(hd + 1) * dv] = og.astype(BF16)
    y_ref[...] = x + _dot(og_scr[...], wout_ref[...])


def _ret_prompt(x_p, B, S, g, win, wout, ng, cos, sin):
    Tp, D = x_p.shape
    dk, dv = D // RET_HEADS, 2 * D // RET_HEADS
    chunk = min(S, RET_CHUNK)
    L = _pick_tile(S, RET_TILE)
    core = min(L, RET_SCORE_BLOCK)
    assert L % core == 0 and core % chunk == 0
    nc = S // L
    body = functools.partial(_ret_prompt_body, L=L, chunk=chunk, dk=dk, dv=dv)
    return pl.pallas_call(
        body,
        out_shape=(jax.ShapeDtypeStruct((Tp, D), F32),
                   jax.ShapeDtypeStruct((B, RET_HEADS, dk, dv), F32)),
        grid=(B, nc),
        in_specs=[
            pl.BlockSpec((L, D), lambda b, c: (b * nc + c, 0)),
            _const_spec((1, D)),
            _const_spec(win.shape),
            _const_spec(wout.shape),
            _const_spec((1, RET_HEADS * dv)),
            pl.BlockSpec((L, dk // 2), lambda b, c: (c, 0)),
            pl.BlockSpec((L, dk // 2), lambda b, c: (c, 0)),
        ],
        out_specs=(
            pl.BlockSpec((L, D), lambda b, c: (b * nc + c, 0)),
            pl.BlockSpec((1, RET_HEADS, dk, dv), lambda b, c: (b, 0, 0, 0)),
        ),
        scratch_shapes=[pltpu.VMEM((RET_HEADS, core, core), F32),
                        pltpu.VMEM((L, RET_HEADS * dv), BF16)],
        compiler_params=_params(("arbitrary", "arbitrary")),
        name="ret_prompt",
    )(x_p, g, win, wout, ng, cos, sin)


def _ret_sample_body(x_ref, g_ref, win_ref, wout_ref, ng_ref, cos_ref, sin_ref, s0_ref,
                     y_ref, s_ref, q_scr, k_scr, v_scr, gt_scr, og_scr, *, L, dk, dv):
    b, nb = pl.program_id(0), pl.num_programs(0)
    lgs = _log_gammas()

    @pl.when(b == 0)
    def _():
        hn = _rms(x_ref[...], g_ref[...]).astype(BF16)
        for hd in range(RET_HEADS):
            q, k, v, gt = _ret_proj(hn, win_ref, hd, dk, dv)
            q_scr[hd], k_scr[hd], v_scr[hd], gt_scr[hd] = q, k, v, gt

    rows = pl.ds(pl.multiple_of(b * L, L), L)
    cos, sin = cos_ref[...], sin_ref[...]
    for hd in range(RET_HEADS):
        og, s_new = _ret_head(q_scr[hd, rows, :], k_scr[hd, rows, :], v_scr[hd, rows, :],
                              gt_scr[hd, rows, :], cos, sin, _decay_matrix(L, L, lgs[hd]),
                              s0_ref[0, hd], ng_ref[:, hd * dv:(hd + 1) * dv], L, lgs[hd])
        s_ref[0, hd] = s_new
        og_scr[rows, hd * dv:(hd + 1) * dv] = og.astype(BF16)

    @pl.when(b == nb - 1)
    def _():
        y_ref[...] = x_ref[...] + _dot(og_scr[...], wout_ref[...])


def _ret_sample(x_s, Bs, Ls, g, win, wout, ng, cos, sin, s0):
    Ts, D = x_s.shape
    dk, dv = D // RET_HEADS, 2 * D // RET_HEADS
    assert Ls <= RET_CHUNK
    body = functools.partial(_ret_sample_body, L=Ls, dk=dk, dv=dv)
    return pl.pallas_call(
        body,
        out_shape=(jax.ShapeDtypeStruct((Ts, D), F32),
                   jax.ShapeDtypeStruct((Bs, RET_HEADS, dk, dv), F32)),
        grid=(Bs,),
        in_specs=[
            _const_spec((Ts, D)),
            _const_spec((1, D)),
            _const_spec(win.shape),
            _const_spec(wout.shape),
            _const_spec((1, RET_HEADS * dv)),
            _const_spec((Ls, dk // 2)),
            _const_spec((Ls, dk // 2)),
            pl.BlockSpec((1, RET_HEADS, dk, dv), lambda b: (b, 0, 0, 0)),
        ],
        out_specs=(
            pl.BlockSpec((Ts, D), lambda b: (0, 0)),
            pl.BlockSpec((1, RET_HEADS, dk, dv), lambda b: (b, 0, 0, 0)),
        ),
        scratch_shapes=[pltpu.VMEM((RET_HEADS, Ts, dk), F32), pltpu.VMEM((RET_HEADS, Ts, dk), F32),
                        pltpu.VMEM((RET_HEADS, Ts, dv), F32), pltpu.VMEM((RET_HEADS, Ts, dv), F32),
                        pltpu.VMEM((Ts, RET_HEADS * dv), BF16)],
        compiler_params=_params(("arbitrary",)),
        name="ret_sample",
    )(x_s, g, win, wout, ng, cos, sin, s0)


def _gelu_tanh(x):
    return 0.5 * x * (1.0 + jnp.tanh(math.sqrt(2.0 / math.pi) * (x + 0.044715 * (x * x * x))))


def _softplus(z):
    return jnp.maximum(z, 0.0) + jnp.log1p(jnp.exp(-jnp.abs(z)))


def _lru_gates(conv, wa_ref, wx_ref, ba, bx, lam):
    D = conv.shape[1]
    bw = D // LRU_BLOCKS
    ra, ix = [], []
    for n in range(LRU_BLOCKS):
        cb = conv[:, n * bw:(n + 1) * bw].astype(BF16)
        ra.append(_dot(cb, wa_ref[n]))
        ix.append(_dot(cb, wx_ref[n]))
    r = jax.nn.sigmoid(jnp.concatenate(ra, axis=-1) + ba)
    i = jax.nn.sigmoid(jnp.concatenate(ix, axis=-1) + bx)
    log_a = (-LRU_C) * r * _softplus(-lam)
    a = jnp.exp(log_a)
    mult = jnp.sqrt(jnp.tanh(-log_a) * (1.0 + a * a))
    return a, mult, i * conv


def _seg_scan(a, b, seg):
    R = a.shape[0]
    row = lax.broadcasted_iota(jnp.int32, (R, 1), 0) % seg
    k = 1
    while k < seg:
        keep = row >= k
        a_sh = jnp.where(keep, pltpu.roll(a, k, axis=0), 1.0)
        b_sh = jnp.where(keep, pltpu.roll(b, k, axis=0), 0.0)
        b = a * b_sh + b
        a = a * a_sh
        k *= 2
    return b


def _carry_scan(a, b, h_prev):
    L, D = a.shape
    G = L // SUBLANES
    a3, b3 = a.reshape(G, SUBLANES, D), b.reshape(G, SUBLANES, D)
    sub = lax.broadcasted_iota(jnp.int32, (SUBLANES, D), 0)[None]
    k = 1
    while k < SUBLANES:
        keep = sub >= k
        a_sh = jnp.where(keep, pltpu.roll(a3, k, axis=1), 1.0)
        b_sh = jnp.where(keep, pltpu.roll(b3, k, axis=1), 0.0)
        b3 = a3 * b_sh + b3
        a3 = a3 * a_sh
        k *= 2
    carry, outs = h_prev, []
    for j in range(G):
        r = a3[j] * carry + b3[j]
        outs.append(r)
        carry = r[SUBLANES - 1:SUBLANES, :]
    return jnp.concatenate(outs, axis=0)


def _lru_prompt_body(x_ref, g_ref, win_ref, cw_ref, cb_ref, wa_ref, ba_ref, wx_ref, bx_ref,
                     lam_ref, wout_ref, y_ref, conv_ref, h_ref, xc_scr, hprev_scr, *, L):
    c = pl.program_id(1)
    D = x_ref.shape[1]

    @pl.when(c == 0)
    def _():
        xc_scr[0:8, :] = jnp.zeros((8, D), F32)
        hprev_scr[...] = jnp.zeros_like(hprev_scr)

    x = x_ref[...]
    hn = _rms(x, g_ref[...]).astype(BF16)
    gate = _gelu_tanh(_dot(hn, win_ref[:, :D]))
    x_br = _dot(hn, win_ref[:, D:])
    xc_scr[8:8 + L, :] = x_br
    conv = cb_ref[...] + xc_scr[5:5 + L, :] * cw_ref[0:1, :]
    for j in range(1, CONV_W):
        conv = conv + xc_scr[5 + j:5 + j + L, :] * cw_ref[j:j + 1, :]
    a, mult, gi = _lru_gates(conv, wa_ref, wx_ref, ba_ref[...], bx_ref[...], lam_ref[...])
    row = lax.broadcasted_iota(jnp.int32, (L, 1), 0)
    mult = jnp.where((row == 0) & (c == 0), 1.0, mult)
    hs = _carry_scan(a, mult * gi, hprev_scr[7:8, :])
    y_ref[...] = x + _dot((hs * gate).astype(BF16), wout_ref[...])
    hprev_scr[...] = hs[L - 8:L, :]
    h_ref[0] = hs[L - 8:L, :]
    conv_ref[0] = x_br[L - 8:L, :]
    xc_scr[0:8, :] = x_br[L - 8:L, :]


def _lru_prompt(x_p, B, S, g, win, cw, cb, wa, ba, wx, bx, lam, wout):
    Tp, D = x_p.shape
    L = _pick_tile(S, LRU_TILE)
    nc = S // L
    body = functools.partial(_lru_prompt_body, L=L)
    return pl.pallas_call(
        body,
        out_shape=(jax.ShapeDtypeStruct((Tp, D), F32),
                   jax.ShapeDtypeStruct((B, 8, D), F32),
                   jax.ShapeDtypeStruct((B, 8, D), F32)),
        grid=(B, nc),
        in_specs=[
            pl.BlockSpec((L, D), lambda b, c: (b * nc + c, 0)),
            _const_spec((1, D)), _const_spec(win.shape), _const_spec(cw.shape), _const_spec((1, D)),
            _const_spec(wa.shape), _const_spec((1, D)), _const_spec(wx.shape), _const_spec((1, D)),
            _const_spec((1, D)), _const_spec(wout.shape),
        ],
        out_specs=(
            pl.BlockSpec((L, D), lambda b, c: (b * nc + c, 0)),
            pl.BlockSpec((1, 8, D), lambda b, c: (b, 0, 0)),
            pl.BlockSpec((1, 8, D), lambda b, c: (b, 0, 0)),
        ),
        scratch_shapes=[pltpu.VMEM((L + 8, D), F32), pltpu.VMEM((8, D), F32)],
        compiler_params=_params(("arbitrary", "arbitrary")),
        name="lru_prompt",
    )(x_p, g, win, cw, cb, wa, ba, wx, bx, lam, wout)


def _lru_sample_body(x_ref, g_ref, win_ref, cw_ref, cb_ref, wa_ref, ba_ref, wx_ref, bx_ref,
                     lam_ref, wout_ref, cs_ref, h0_ref, y_ref, conv_ref, h_ref,
                     xc_scr, *, Bs, Ls):
    Ts, D = x_ref.shape
    x = x_ref[...]
    hn = _rms(x, g_ref[...]).astype(BF16)
    gate = _gelu_tanh(_dot(hn, win_ref[:, :D]))
    x_br = _dot(hn, win_ref[:, D:])
    xc_scr[:, 0:8, :] = cs_ref[...]
    xc_scr[:, 8:8 + Ls, :] = x_br.reshape(Bs, Ls, D)
    conv = cb_ref[...] + xc_scr[:, 5:5 + Ls, :] * cw_ref[0:1, :]
    for j in range(1, CONV_W):
        conv = conv + xc_scr[:, 5 + j:5 + j + Ls, :] * cw_ref[j:j + 1, :]
    conv = conv.reshape(Ts, D)
    a, mult, gi = _lru_gates(conv, wa_ref, wx_ref, ba_ref[...], bx_ref[...], lam_ref[...])
    bvec = mult * gi + a * h0_ref[...].reshape(Ts, D)
    hs = _seg_scan(a, bvec, Ls)
    y_ref[...] = x + _dot((hs * gate).astype(BF16), wout_ref[...])
    h_ref[...] = hs.reshape(Bs, Ls, D)[:, Ls - 8:Ls, :]
    conv_ref[...] = xc_scr[:, Ls:Ls + 8, :]


def _lru_sample(x_s, Bs, Ls, g, win, cw, cb, wa, ba, wx, bx, lam, wout, cs, h0):
    Ts, D = x_s.shape
    assert Ls % 8 == 0
    body = functools.partial(_lru_sample_body, Bs=Bs, Ls=Ls)
    return pl.pallas_call(
        body,
        out_shape=(jax.ShapeDtypeStruct((Ts, D), F32),
                   jax.ShapeDtypeStruct((Bs, 8, D), F32),
                   jax.ShapeDtypeStruct((Bs, 8, D), F32)),
        grid=(1,),
        in_specs=[
            _const_spec((Ts, D)),
            _const_spec((1, D)), _const_spec(win.shape), _const_spec(cw.shape), _const_spec((1, D)),
            _const_spec(wa.shape), _const_spec((1, D)), _const_spec(wx.shape), _const_spec((1, D)),
            _const_spec((1, D)), _const_spec(wout.shape),
            _const_spec((Bs, 8, D)), _const_spec((Bs, Ls, D)),
        ],
        out_specs=(
            pl.BlockSpec((Ts, D), lambda i: (0, 0)),
            pl.BlockSpec((Bs, 8, D), lambda i: (0, 0, 0)),
            pl.BlockSpec((Bs, 8, D), lambda i: (0, 0, 0)),
        ),
        scratch_shapes=[pltpu.VMEM((Bs, Ls + 8, D), F32)],
        compiler_params=_params(("arbitrary",)),
        name="lru_sample",
    )(x_s, g, win, cw, cb, wa, ba, wx, bx, lam, wout, cs, h0)


def _router_body(x_ref, g_ref, wc_ref, bias_ref, cin_ref, text_ref, meta_ref, cnt_ref, run_scr):
    i = pl.program_id(0)
    tm, D = x_ref.shape

    @pl.when(i == 0)
    def _():
        run_scr[...] = cin_ref[...]

    t = _rms(x_ref[...], g_ref[...])
    logits = _dot(t.astype(BF16), wc_ref[...]) + bias_ref[...]
    lane_i = lax.broadcasted_iota(jnp.int32, (tm, LANES), 1)
    lane = lane_i.astype(F32)

    def first_argmax(v):
        m = jnp.max(v, axis=-1, keepdims=True)
        return m, jnp.min(jnp.where(v == m, lane, 1e6), axis=-1, keepdims=True)

    gmask = lane_i < MOE_GROUPS
    gl = jnp.where(gmask, logits, NEG_BIG)
    mg, g_idx = first_argmax(gl)
    p_g = 1.0 / jnp.sum(jnp.where(gmask, jnp.exp(gl - mg), 0.0), axis=-1, keepdims=True)
    e_lane = lane_i - MOE_GROUPS
    lane_grp = (e_lane >> 2).astype(F32)
    emask = (e_lane >= 0) & (e_lane < MOE_EXPERTS) & (lane_grp == g_idx)
    el = jnp.where(emask, logits, NEG_BIG)
    m1, i1 = first_argmax(el)
    m2, i2 = first_argmax(jnp.where(lane == i1, NEG_BIG, el))
    e21 = jnp.exp(m2 - m1)
    w1 = p_g / (1.0 + e21)
    w2 = p_g * e21 / (1.0 + e21)
    first_lower = i1 < i2
    j_lo = jnp.minimum(i1, i2) - (MOE_GROUPS + MOE_PER_GROUP * g_idx)
    j_hi = jnp.maximum(i1, i2) - (MOE_GROUPS + MOE_PER_GROUP * g_idx)
    pair = 0.5 * (j_lo * (7.0 - j_lo)) + (j_hi - j_lo - 1.0)
    cls = g_idx * N_PAIRS + pair
    w_lo = jnp.where(first_lower, w1, w2)
    w_hi = jnp.where(first_lower, w2, w1)

    onehot = (lane == cls).astype(F32)
    ri = lax.broadcasted_iota(jnp.int32, (tm, tm), 0)
    ci = lax.broadcasted_iota(jnp.int32, (tm, tm), 1)
    tri = (ci < ri).astype(BF16)
    prefix = _dot(tri, onehot.astype(BF16))
    rank = jnp.sum(onehot * (prefix + run_scr[...]), axis=-1, keepdims=True)
    run_scr[...] = run_scr[...] + jnp.sum(onehot, axis=0, keepdims=True)

    text_ref[:, :D] = t
    text_ref[:, D:] = jnp.where(lane_i == 0, w_lo, jnp.where(lane_i == 1, w_hi, 0.0))
    meta = jnp.where(lane_i == 0, cls, jnp.where(lane_i == 1, rank, 0.0))
    meta_ref[...] = meta[:, :META_W]
    cnt_ref[...] = run_scr[...]


def _router(x, g, wc, bias, cnt_in):
    T, D = x.shape
    tm = _pick_tile(T, ROUTE_TILE)
    return pl.pallas_call(
        _router_body,
        out_shape=(jax.ShapeDtypeStruct((T, D + EXTRA), F32),
                   jax.ShapeDtypeStruct((T, META_W), F32),
                   jax.ShapeDtypeStruct((1, LANES), F32)),
        grid=(T // tm,),
        in_specs=[
            pl.BlockSpec((tm, D), lambda i: (i, 0)),
            _const_spec((1, D)), _const_spec(wc.shape), _const_spec((1, LANES)),
            _const_spec((1, LANES)),
        ],
        out_specs=(
            pl.BlockSpec((tm, D + EXTRA), lambda i: (i, 0)),
            pl.BlockSpec((tm, META_W), lambda i: (i, 0)),
            pl.BlockSpec((1, LANES), lambda i: (0, 0)),
        ),
        scratch_shapes=[pltpu.VMEM((1, LANES), F32)],
        compiler_params=_params(("arbitrary",)),
        name="moe_router",
    )(x, g, wc, bias, cnt_in)


def _start_rows(n, idx_ref, row_copy):
    assert n % SUBLANES == 0

    def body(j, carry):
        for u in range(SUBLANES):
            pos = idx_ref[0, 0, j * SUBLANES + u]
            row_copy(j, u, pos >> 3, pos & (SUBLANES - 1)).start(priority=u % 2)
        return carry

    lax.fori_loop(0, n // SUBLANES, body, 0)


def _dispatch_body(dest_ref, t_ref, xs_in_ref, xs_ref, sem):
    del xs_in_ref
    groups = t_ref.shape[0]

    def row_copy(j, u, hi, lo):
        return pltpu.make_async_copy(t_ref.at[j, pl.ds(u, 1), :],
                                     xs_ref.at[hi, pl.ds(lo, 1), :], sem)

    _start_rows(groups * SUBLANES, dest_ref, row_copy)
    pltpu.make_async_copy(t_ref, xs_ref.at[pl.ds(0, groups)], sem).wait()


def _dispatch(t_ext, dest3, xs):
    nb, _, td = dest3.shape
    W = t_ext.shape[1]
    t_ext = t_ext.reshape(-1, SUBLANES, W)
    return pl.pallas_call(
        _dispatch_body,
        out_shape=jax.ShapeDtypeStruct(xs.shape, F32),
        grid=(nb,),
        in_specs=[
            pl.BlockSpec((1, 1, td), lambda i: (i, 0, 0), memory_space=pltpu.SMEM),
            pl.BlockSpec((td // SUBLANES, SUBLANES, W), lambda i: (i, 0, 0)),
            pl.BlockSpec(memory_space=pl.ANY),
        ],
        out_specs=pl.BlockSpec(memory_space=pl.ANY),
        scratch_shapes=[pltpu.SemaphoreType.DMA(())],
        input_output_aliases={2: 0},
        compiler_params=_params(("arbitrary",)),
        name="moe_dispatch",
    )(dest3, t_ext, xs)


def _ffn_body(ea_ref, eb_ref, halves_ref, xs_ref, wga_ref, wua_ref, wda_ref,
              wgb_ref, wub_ref, wdb_ref, ys_ref, wg_scr, wu_scr, wd_scr):
    i = pl.program_id(0)
    tg, D = ys_ref.shape
    nh = halves_ref[i]
    prev = jnp.maximum(i - 1, 0)
    new_class = (i == 0) | (ea_ref[i] != ea_ref[prev]) | (eb_ref[i] != eb_ref[prev])

    @pl.when((nh > 0) & new_class)
    def _():
        for slot, (wg, wu, wd) in enumerate(((wga_ref, wua_ref, wda_ref), (wgb_ref, wub_ref, wdb_ref))):
            wg_scr[slot] = wg[0, 0].astype(BF16)
            wu_scr[slot] = wu[0, 0].astype(BF16)
            wd_scr[slot] = wd[0, 0].astype(BF16)

    def experts(rows):
        x = xs_ref[rows, :D].astype(BF16)
        w_a = xs_ref[rows, D:D + 1]
        w_b = xs_ref[rows, D + 1:D + 2]

        def expert(slot):
            gt = _dot(x, wg_scr[slot])
            hdn = (gt * jax.nn.sigmoid(gt)) * _dot(x, wu_scr[slot])
            return _dot(hdn.astype(BF16), wd_scr[slot])

        return w_a * expert(0) + w_b * expert(1)

    @pl.when(nh == 2)
    def _():
        ys_ref[...] = experts(slice(0, tg))

    @pl.when(nh == 1)
    def _():
        ys_ref[:FFN_HALF, :] = experts(slice(0, FFN_HALF))
        ys_ref[FFN_HALF:, :] = jnp.zeros((tg - FFN_HALF, D), F32)

    @pl.when(nh == 0)
    def _():
        ys_ref[...] = jnp.zeros_like(ys_ref)


def _ffn(xs, ea, eb, halves, layer, wg, wu, wd):
    P, W = xs.shape
    D = W - EXTRA
    ff = wg.shape[3]
    tg = FFN_TILE
    wspec_a = lambda shape: pl.BlockSpec(shape, lambda i, ea, eb, nh: (layer, ea[i], 0, 0))
    wspec_b = lambda shape: pl.BlockSpec(shape, lambda i, ea, eb, nh: (layer, eb[i], 0, 0))
    grid_spec = pltpu.PrefetchScalarGridSpec(
        num_scalar_prefetch=3,
        grid=(P // tg,),
        in_specs=[
            pl.BlockSpec((tg, W), lambda i, ea, eb, nh: (i, 0)),
            wspec_a((1, 1, D, ff)), wspec_a((1, 1, D, ff)), wspec_a((1, 1, ff, D)),
            wspec_b((1, 1, D, ff)), wspec_b((1, 1, D, ff)), wspec_b((1, 1, ff, D)),
        ],
        out_specs=pl.BlockSpec((tg, D), lambda i, ea, eb, nh: (i, 0)),
        scratch_shapes=[pltpu.VMEM((2, D, ff), BF16), pltpu.VMEM((2, D, ff), BF16),
                        pltpu.VMEM((2, ff, D), BF16)],
    )
    return pl.pallas_call(
        _ffn_body,
        out_shape=jax.ShapeDtypeStruct((P, D), F32),
        grid_spec=grid_spec,
        compiler_params=_params(("arbitrary",)),
        name="moe_ffn",
    )(ea, eb, halves, xs, wg, wu, wd, wg, wu, wd)


def _combine_body(dest_ref, dnext_ref, x_ref, gfin_ref, ys_ref, o_ref, ybuf, sems, *, final_norm):
    i, n = pl.program_id(0), pl.num_programs(0)
    tc, D = x_ref.shape
    groups = tc // SUBLANES

    def gather(idx_ref, slot):
        def row_copy(j, u, hi, lo):
            return pltpu.make_async_copy(ys_ref.at[hi, pl.ds(lo, 1), :],
                                         ybuf.at[slot, j, pl.ds(u, 1), :], sems.at[slot])
        _start_rows(tc, idx_ref, row_copy)

    @pl.when(i == 0)
    def _():
        gather(dest_ref, 0)

    for slot in range(2):
        @pl.when((i + 1 < n) & ((i + 1) % 2 == slot))
        def _():
            gather(dnext_ref, slot)

    for slot in range(2):
        @pl.when(i % 2 == slot)
        def _():
            pltpu.make_async_copy(ys_ref.at[pl.ds(0, groups)], ybuf.at[slot], sems.at[slot]).wait()
            y = x_ref[...] + ybuf[slot].reshape(tc, D)
            if final_norm:
                y = _rms(y, gfin_ref[...])
            o_ref[...] = y


def _combine(x, dest3, ys, gfin, final_norm):
    T, D = x.shape
    nb, _, tc = dest3.shape
    body = functools.partial(_combine_body, final_norm=final_norm)
    return pl.pallas_call(
        body,
        out_shape=jax.ShapeDtypeStruct((T, D), F32),
        grid=(nb,),
        in_specs=[
            pl.BlockSpec((1, 1, tc), lambda i: (i, 0, 0), memory_space=pltpu.SMEM),
            pl.BlockSpec((1, 1, tc), lambda i: (jnp.minimum(i + 1, nb - 1), 0, 0),
                         memory_space=pltpu.SMEM),
            pl.BlockSpec((tc, D), lambda i: (i, 0)),
            _const_spec((1, D)),
            pl.BlockSpec(memory_space=pl.ANY),
        ],
        out_specs=pl.BlockSpec((tc, D), lambda i: (i, 0)),
        scratch_shapes=[pltpu.VMEM((2, tc // SUBLANES, SUBLANES, D), F32),
                        pltpu.SemaphoreType.DMA((2,))],
        compiler_params=_params(("arbitrary",)),
        name="moe_combine",
    )(dest3, dest3, x, gfin, ys.reshape(-1, SUBLANES, D))


def _moe_tables(metas, counts, P):
    tg = FFN_TILE
    cnt = counts[0, :N_CLASSES].astype(jnp.int32)
    padded = ((cnt + tg - 1) // tg) * tg
    ends = jnp.cumsum(padded)
    offs = ends - padded
    dests = [jnp.take(offs, m[:, 0].astype(jnp.int32)) + m[:, 1].astype(jnp.int32) for m in metas]
    tile_start = jnp.arange(P // tg, dtype=jnp.int32) * tg
    tile_cls = jnp.minimum(jnp.sum((tile_start[:, None] >= ends[None, :]).astype(jnp.int32), axis=1),
                           N_CLASSES - 1)
    rows = jnp.clip(jnp.take(cnt, tile_cls) - (tile_start - jnp.take(offs, tile_cls)), 0, tg)
    halves = jnp.where(tile_start < ends[-1], (rows + FFN_HALF - 1) // FFN_HALF, 0).astype(jnp.int32)
    grp = tile_cls // N_PAIRS
    pair = tile_cls % N_PAIRS
    ea = grp * MOE_PER_GROUP + jnp.take(jnp.asarray(_PAIR_LO), pair)
    eb = grp * MOE_PER_GROUP + jnp.take(jnp.asarray(_PAIR_HI), pair)
    return dests, ea, eb, halves


def _moe(xs_in, g, w_group, b_group, w_router, b_router, layer, wg, wu, wd, gfin, final_norm):
    D = xs_in[0].shape[1]
    T = sum(x.shape[0] for x in xs_in)
    wcat = jnp.zeros((D, LANES), F32)
    wcat = wcat.at[:, :MOE_GROUPS].set(w_group).at[:, MOE_GROUPS:MOE_GROUPS + MOE_EXPERTS].set(w_router)
    bias = jnp.zeros((1, LANES), F32)
    bias = bias.at[0, :MOE_GROUPS].set(b_group).at[0, MOE_GROUPS:MOE_GROUPS + MOE_EXPERTS].set(b_router)
    wc = wcat.astype(BF16)

    counts = jnp.zeros((1, LANES), F32)
    texts, metas = [], []
    for x in xs_in:
        t_ext, meta, counts = _router(x, g, wc, bias, counts)
        texts.append(t_ext)
        metas.append(meta)

    tg = FFN_TILE
    P = ((T + tg - 1) // tg + N_CLASSES) * tg
    dests, ea, eb, halves = _moe_tables(metas, counts, P)
    dest3s = []
    xs = jnp.zeros((P // SUBLANES, SUBLANES, D + EXTRA), F32)
    for x, t_ext, dest in zip(xs_in, texts, dests):
        tmove = _pick_tile(x.shape[0], MOVE_TILE)
        dest3s.append(dest.reshape(x.shape[0] // tmove, 1, tmove))
        xs = _dispatch(t_ext, dest3s[-1], xs)
    ys = _ffn(xs.reshape(P, D + EXTRA), ea, eb, halves, layer, wg, wu, wd)
    return [_combine(x, dest3, ys, gfin, final_norm) for x, dest3 in zip(xs_in, dest3s)]


def _rope_tables(pos, dk):
    half = dk // 2
    inv = 1.0 / (ROPE_BASE ** jnp.linspace(0.0, 1.0, half, dtype=F32))
    ang = pos.astype(F32)[:, None] * inv[None, :]
    return jnp.cos(ang), jnp.sin(ang)


def kernel(x_prompt, x_sample, state_ret, state_conv, state_lru, norm_mix_g, norm_ffn_g, norm_final_g, ret_w_in, ret_w_out, ret_norm_g, lru_w_in, lru_conv_w, lru_conv_b, lru_w_a, lru_b_a, lru_w_x, lru_b_x, lru_lambda, lru_w_out, moe_w_group, moe_b_group, moe_w_router, moe_b_router, moe_w_gate, moe_w_up, moe_w_down):
    B, S, D = x_prompt.shape
    Bs, Ls, _ = x_sample.shape
    Tp, Ts = B * S, Bs * Ls
    assert norm_mix_g.shape[0] == 2
    dk = D // RET_HEADS
    row = lambda v: v.reshape(1, -1)
    gfin = row(norm_final_g)

    cos_p, sin_p = _rope_tables(jnp.arange(S), dk)
    cos_s, sin_s = _rope_tables(PAST_LEN + jnp.arange(Ls), dk)
    ret_args = (row(norm_mix_g[0]), ret_w_in[0].astype(BF16), ret_w_out[0].astype(BF16),
                row(ret_norm_g[0]))
    xp, ret_p = _ret_prompt(x_prompt.reshape(Tp, D), B, S, *ret_args, cos_p, sin_p)
    xs, ret_s = _ret_sample(x_sample.reshape(Ts, D), Bs, Ls, *ret_args, cos_s, sin_s, state_ret[0])
    xp, xs = _moe([xp, xs], row(norm_ffn_g[0]), moe_w_group[0], moe_b_group[0], moe_w_router[0],
                  moe_b_router[0], 0, moe_w_gate, moe_w_up, moe_w_down, gfin, False)

    lru_args = (row(norm_mix_g[1]), lru_w_in[0].astype(BF16), lru_conv_w[0], row(lru_conv_b[0]),
                lru_w_a[0].astype(BF16), row(lru_b_a[0]), lru_w_x[0].astype(BF16), row(lru_b_x[0]),
                row(lru_lambda[0]), lru_w_out[0].astype(BF16))
    cs8 = jnp.pad(state_conv[0], ((0, 0), (8 - (CONV_W - 1), 0), (0, 0)))
    h0_rows = jnp.pad(state_lru[0].reshape(Bs, 1, D), ((0, 0), (0, Ls - 1), (0, 0)))
    xp, conv_p, lru_p = _lru_prompt(xp, B, S, *lru_args)
    xs, conv_s, lru_s = _lru_sample(xs, Bs, Ls, *lru_args, cs8, h0_rows)
    y_p, y_s = _moe([xp, xs], row(norm_ffn_g[1]), moe_w_group[1], moe_b_group[1], moe_w_router[1],
                    moe_b_router[1], 1, moe_w_gate, moe_w_up, moe_w_down, gfin, True)

    return (y_p.reshape(B, S, D), y_s.reshape(Bs, Ls, D),
            ret_p[None], conv_p[None, :, 8 - (CONV_W - 1):], lru_p[None, :, 7],
            ret_s[None], conv_s[None, :, 8 - (CONV_W - 1):], lru_s[None, :, 7])
```

```python
import functools
import math

import numpy as np
import jax
import jax.numpy as jnp
from jax import lax
from jax.experimental import pallas as pl
from jax.experimental.pallas import tpu as pltpu

F32 = jnp.float32
BF16 = jnp.bfloat16

RET_HEADS = 4
RET_CHUNK = 64
ROPE_BASE = 10000.0
CONV_W = 4
LRU_BLOCKS = 4
LRU_C = 8.0
MOE_GROUPS = 4
MOE_PER_GROUP = 4
MOE_EXPERTS = MOE_GROUPS * MOE_PER_GROUP
NORM_EPS = 1e-6
PAST_LEN = 2048

LANES = 128
SUBLANES = 8
VMEM_LIMIT_BYTES = 56 * 1024 * 1024

RET_TILE = 512
RET_SCORE_BLOCK = 256
LRU_TILE = 512
ROUTE_TILE = 512
MOVE_TILE = 512
FFN_TILE = 512
FFN_HALF = 256
N_PAIRS = 6
N_CLASSES = MOE_GROUPS * N_PAIRS
EXTRA = LANES
META_W = 8
NEG_BIG = -1e30

_PAIR_LO = np.array([0, 0, 0, 1, 1, 2], np.int32)
_PAIR_HI = np.array([1, 2, 3, 2, 3, 3], np.int32)


def _log_gammas():
    return [math.log1p(-(2.0 ** (-5.0 - h))) for h in range(RET_HEADS)]


def _rms(x, g):
    return x * lax.rsqrt(jnp.mean(x * x, axis=-1, keepdims=True) + NORM_EPS) * g


def _dot(a, b):
    return jnp.dot(a, b, preferred_element_type=F32)


def _dot_nt(a, b):
    return lax.dot_general(a, b, (((1,), (1,)), ((), ())), preferred_element_type=F32)


def _dot_tn(a, b):
    return lax.dot_general(a, b, (((0,), (0,)), ((), ())), preferred_element_type=F32)


def _const_spec(shape):
    nd = len(shape)
    return pl.BlockSpec(shape, lambda *_: (0,) * nd, pipeline_mode=pl.Buffered(1))


def _params(sem):
    return pltpu.CompilerParams(dimension_semantics=sem, vmem_limit_bytes=VMEM_LIMIT_BYTES)


def _pick_tile(n, pref):
    t = pref
    while n % t:
        t //= 2
    assert t >= SUBLANES, (n, pref)
    return t


def _decay_matrix(L, chunk, lg):
    ii = lax.broadcasted_iota(jnp.int32, (L, L), 0)
    jj = lax.broadcasted_iota(jnp.int32, (L, L), 1)
    diff = (ii - jj).astype(F32)
    same_chunk = (ii // chunk) == (jj // chunk)
    return jnp.where(same_chunk & (ii >= jj), jnp.exp(jnp.maximum(diff, 0.0) * lg), 0.0)


def _rotary(t, cos, sin):
    half = t.shape[-1] // 2
    t1, t2 = t[:, :half], t[:, half:]
    return jnp.concatenate([t1 * cos - t2 * sin, t1 * sin + t2 * cos], axis=-1)


def _ret_proj(hn, win_ref, hd, dk, dv):
    qk, vv = RET_HEADS * dk, RET_HEADS * dv
    q = _dot(hn, win_ref[:, hd * dk:(hd + 1) * dk])
    k = _dot(hn, win_ref[:, qk + hd * dk:qk + (hd + 1) * dk])
    v = _dot(hn, win_ref[:, 2 * qk + hd * dv:2 * qk + (hd + 1) * dv])
    gt = _dot(hn, win_ref[:, 2 * qk + vv + hd * dv:2 * qk + vv + (hd + 1) * dv])
    return q, k, v, gt


def _ret_head(q, k, v, gt, cos, sin, decay, s0, ng, chunk, lg, kv_scr=None):
    L, dk = q.shape
    qr = _rotary(q, cos, sin)
    kr = _rotary(k, cos, sin) * (dk ** -0.5)
    idx = (lax.broadcasted_iota(jnp.int32, (L, 1), 0) % chunk).astype(F32)
    qd = (qr * jnp.exp((idx + 1.0) * lg)).astype(BF16)
    kd = (kr * jnp.exp((chunk - 1.0 - idx) * lg)).astype(BF16)
    qb, kb, vb = qr.astype(BF16), kr.astype(BF16), v.astype(BF16)
    core = decay.shape[0]
    chunks = [slice(c * chunk, (c + 1) * chunk) for c in range(L // chunk)]
    if kv_scr is not None:
        for c, rows in enumerate(chunks):
            kv_scr[c] = _dot_tn(kd[rows], vb[rows])
    s, parts = s0, []
    for r0 in range(0, L, core):
        blk = slice(r0, r0 + core)
        scores = _dot_nt(qb[blk], kb[blk]) * decay
        intra = _dot(scores.astype(BF16), vb[blk])
        for c in range(r0 // chunk, (r0 + core) // chunk):
            rows = chunks[c]
            parts.append(intra[rows.start - r0:rows.stop - r0] + _dot(qd[rows], s.astype(BF16)))
            kv = kv_scr[c] if kv_scr is not None else _dot_tn(kd[rows], vb[rows])
            s = math.exp(chunk * lg) * s + kv
    o = parts[0] if len(parts) == 1 else jnp.concatenate(parts, axis=0)
    o = o * lax.rsqrt(jnp.mean(o * o, axis=-1, keepdims=True) + NORM_EPS)
    og = (gt * jax.nn.sigmoid(gt)) * (o * ng)
    return og, s


def _ret_prompt_body(x_ref, g_ref, win_ref, wout_ref, ng_ref, cos_ref, sin_ref,
                     y_ref, s_ref, decay_scr, og_scr, kv_scr, *, L, chunk, dk, dv):
    b, c = pl.program_id(0), pl.program_id(1)
    lgs = _log_gammas()

    @pl.when((b == 0) & (c == 0))
    def _():
        for hd in range(RET_HEADS):
            decay_scr[hd] = _decay_matrix(decay_scr.shape[1], chunk, lgs[hd])

    @pl.when(c == 0)
    def _():
        s_ref[...] = jnp.zeros_like(s_ref)

    x = x_ref[...]
    hn = _rms(x, g_ref[...]).astype(BF16)
    cos, sin = cos_ref[...], sin_ref[...]
    for hd in range(RET_HEADS):
        og, s_new = _ret_head(*_ret_proj(hn, win_ref, hd, dk, dv), cos, sin, decay_scr[hd],
                              s_ref[0, hd], ng_ref[:, hd * dv:(hd + 1) * dv], chunk, lgs[hd],
                              kv_scr)
        s_ref[0, hd] = s_new
        og_scr[:, hd * dv:(hd + 1) * dv] = og.astype(BF16)
    y_ref[...] = x + _dot(og_scr[...], wout_ref[...])


def _ret_prompt(x_p, B, S, g, win, wout, ng, cos, sin):
    Tp, D = x_p.shape
    dk, dv = D // RET_HEADS, 2 * D // RET_HEADS
    chunk = min(S, RET_CHUNK)
    L = _pick_tile(S, RET_TILE)
    core = min(L, RET_SCORE_BLOCK)
    assert L % core == 0 and core % chunk == 0
    nc = S // L
    body = functools.partial(_ret_prompt_body, L=L, chunk=chunk, dk=dk, dv=dv)
    return pl.pallas_call(
        body,
        out_shape=(jax.ShapeDtypeStruct((Tp, D), F32),
                   jax.ShapeDtypeStruct((B, RET_HEADS, dk, dv), F32)),
        grid=(B, nc),
        in_specs=[
            pl.BlockSpec((L, D), lambda b, c: (b * nc + c, 0)),
            _const_spec((1, D)),
            _const_spec(win.shape),
            _const_spec(wout.shape),
            _const_spec((1, RET_HEADS * dv)),
            pl.BlockSpec((L, dk // 2), lambda b, c: (c, 0)),
            pl.BlockSpec((L, dk // 2), lambda b, c: (c, 0)),
        ],
        out_specs=(
            pl.BlockSpec((L, D), lambda b, c: (b * nc + c, 0)),
            pl.BlockSpec((1, RET_HEADS, dk, dv), lambda b, c: (b, 0, 0, 0)),
        ),
        scratch_shapes=[pltpu.VMEM((RET_HEADS, core, core), F32),
                        pltpu.VMEM((L, RET_HEADS * dv), BF16),
                        pltpu.VMEM((L // chunk, dk, dv), F32)],
        compiler_params=_params(("arbitrary", "arbitrary")),
        name="ret_prompt",
    )(x_p, g, win, wout, ng, cos, sin)


def _ret_sample_body(x_ref, g_ref, win_ref, wout_ref, ng_ref, cos_ref, sin_ref, s0_ref,
                     y_ref, s_ref, q_scr, k_scr, v_scr, gt_scr, og_scr, *, L, dk, dv):
    b, nb = pl.program_id(0), pl.num_programs(0)
    lgs = _log_gammas()

    @pl.when(b == 0)
    def _():
        hn = _rms(x_ref[...], g_ref[...]).astype(BF16)
        for hd in range(RET_HEADS):
            q, k, v, gt = _ret_proj(hn, win_ref, hd, dk, dv)
            q_scr[hd], k_scr[hd], v_scr[hd], gt_scr[hd] = q, k, v, gt

    rows = pl.ds(pl.multiple_of(b * L, L), L)
    cos, sin = cos_ref[...], sin_ref[...]
    for hd in range(RET_HEADS):
        og, s_new = _ret_head(q_scr[hd, rows, :], k_scr[hd, rows, :], v_scr[hd, rows, :],
                              gt_scr[hd, rows, :], cos, sin, _decay_matrix(L, L, lgs[hd]),
                              s0_ref[0, hd], ng_ref[:, hd * dv:(hd + 1) * dv], L, lgs[hd])
        s_ref[0, hd] = s_new
        og_scr[rows, hd * dv:(hd + 1) * dv] = og.astype(BF16)

    @pl.when(b == nb - 1)
    def _():
        y_ref[...] = x_ref[...] + _dot(og_scr[...], wout_ref[...])


def _ret_sample(x_s, Bs, Ls, g, win, wout, ng, cos, sin, s0):
    Ts, D = x_s.shape
    dk, dv = D // RET_HEADS, 2 * D // RET_HEADS
    assert Ls <= RET_CHUNK
    body = functools.partial(_ret_sample_body, L=Ls, dk=dk, dv=dv)
    return pl.pallas_call(
        body,
        out_shape=(jax.ShapeDtypeStruct((Ts, D), F32),
                   jax.ShapeDtypeStruct((Bs, RET_HEADS, dk, dv), F32)),
        grid=(Bs,),
        in_specs=[
            _const_spec((Ts, D)),
            _const_spec((1, D)),
            _const_spec(win.shape),
            _const_spec(wout.shape),
            _const_spec((1, RET_HEADS * dv)),
            _const_spec((Ls, dk // 2)),
            _const_spec((Ls, dk // 2)),
            pl.BlockSpec((1, RET_HEADS, dk, dv), lambda b: (b, 0, 0, 0)),
        ],
        out_specs=(
            pl.BlockSpec((Ts, D), lambda b: (0, 0)),
            pl.BlockSpec((1, RET_HEADS, dk, dv), lambda b: (b, 0, 0, 0)),
        ),
        scratch_shapes=[pltpu.VMEM((RET_HEADS, Ts, dk), F32), pltpu.VMEM((RET_HEADS, Ts, dk), F32),
                        pltpu.VMEM((RET_HEADS, Ts, dv), F32), pltpu.VMEM((RET_HEADS, Ts, dv), F32),
                        pltpu.VMEM((Ts, RET_HEADS * dv), BF16)],
        compiler_params=_params(("arbitrary",)),
        name="ret_sample",
    )(x_s, g, win, wout, ng, cos, sin, s0)


def _gelu_tanh(x):
    return 0.5 * x * (1.0 + jnp.tanh(math.sqrt(2.0 / math.pi) * (x + 0.044715 * (x * x * x))))


def _softplus(z):
    return jnp.maximum(z, 0.0) + jnp.log1p(jnp.exp(-jnp.abs(z)))


def _lru_gates(conv, wa_ref, wx_ref, ba, bx, lam):
    D = conv.shape[1]
    bw = D // LRU_BLOCKS
    ra, ix = [], []
    for n in range(LRU_BLOCKS):
        cb = conv[:, n * bw:(n + 1) * bw].astype(BF16)
        ra.append(_dot(cb, wa_ref[n]))
        ix.append(_dot(cb, wx_ref[n]))
    r = jax.nn.sigmoid(jnp.concatenate(ra, axis=-1) + ba)
    i = jax.nn.sigmoid(jnp.concatenate(ix, axis=-1) + bx)
    log_a = r * ((-LRU_C) * _softplus(-lam))
    a = jnp.exp(log_a)
    mult = jnp.sqrt(jnp.tanh(-log_a) * (1.0 + a * a))
    return a, mult, i * conv


def _seg_scan(a, b, seg):
    R = a.shape[0]
    row = lax.broadcasted_iota(jnp.int32, (R, 1), 0) % seg
    k = 1
    while k < seg:
        keep = row >= k
        a_sh = jnp.where(keep, pltpu.roll(a, k, axis=0), 1.0)
        b_sh = jnp.where(keep, pltpu.roll(b, k, axis=0), 0.0)
        b = a * b_sh + b
        a = a * a_sh
        k *= 2
    return b


def _carry_scan(a, b, h_prev):
    L, D = a.shape
    G = L // SUBLANES
    a3, b3 = a.reshape(G, SUBLANES, D), b.reshape(G, SUBLANES, D)
    sub = lax.broadcasted_iota(jnp.int32, (SUBLANES, D), 0)[None]
    k = 1
    while k < SUBLANES:
        keep = sub >= k
        a_sh = jnp.where(keep, pltpu.roll(a3, k, axis=1), 1.0)
        b_sh = jnp.where(keep, pltpu.roll(b3, k, axis=1), 0.0)
        b3 = a3 * b_sh + b3
        a3 = a3 * a_sh
        k *= 2
    carry, outs = h_prev, []
    for j in range(G):
        r = a3[j] * carry + b3[j]
        outs.append(r)
        carry = r[SUBLANES - 1:SUBLANES, :]
    return jnp.concatenate(outs, axis=0)


def _lru_prompt_body(x_ref, g_ref, win_ref, cw_ref, cb_ref, wa_ref, ba_ref, wx_ref, bx_ref,
                     lam_ref, wout_ref, y_ref, conv_ref, h_ref, xc_scr, hprev_scr, *, L):
    c = pl.program_id(1)
    D = x_ref.shape[1]

    @pl.when(c == 0)
    def _():
        xc_scr[...] = jnp.zeros_like(xc_scr)
        hprev_scr[...] = jnp.zeros_like(hprev_scr)

    x = x_ref[...]
    hn = _rms(x, g_ref[...]).astype(BF16)
    gate = _gelu_tanh(_dot(hn, win_ref[:, :D]))
    x_br = _dot(hn, win_ref[:, D:])
    prev8 = xc_scr[...]
    sub8 = lax.broadcasted_iota(jnp.int32, (SUBLANES, 1), 0)

    def shifted(k):
        rolled = pltpu.roll(x_br, k, axis=0)
        first = jnp.where(sub8 < k, pltpu.roll(prev8, k, axis=0), rolled[0:SUBLANES, :])
        return jnp.concatenate([first, rolled[SUBLANES:, :]], axis=0)

    conv = cb_ref[...] + shifted(CONV_W - 1) * cw_ref[0:1, :]
    for j in range(1, CONV_W - 1):
        conv = conv + shifted(CONV_W - 1 - j) * cw_ref[j:j + 1, :]
    conv = conv + x_br * cw_ref[CONV_W - 1:CONV_W, :]
    a, mult, gi = _lru_gates(conv, wa_ref, wx_ref, ba_ref[...], bx_ref[...], lam_ref[...])
    first = jnp.where((sub8 == 0) & (c == 0), 1.0, mult[0:SUBLANES, :])
    mult = jnp.concatenate([first, mult[SUBLANES:, :]], axis=0)
    hs = _carry_scan(a, mult * gi, hprev_scr[7:8, :])
    y_ref[...] = x + _dot((hs * gate).astype(BF16), wout_ref[...])
    hprev_scr[...] = hs[L - 8:L, :]
    h_ref[0] = hs[L - 8:L, :]
    conv_ref[0] = x_br[L - 8:L, :]
    xc_scr[...] = x_br[L - 8:L, :]


def _lru_prompt(x_p, B, S, g, win, cw, cb, wa, ba, wx, bx, lam, wout):
    Tp, D = x_p.shape
    L = _pick_tile(S, LRU_TILE)
    nc = S // L
    body = functools.partial(_lru_prompt_body, L=L)
    return pl.pallas_call(
        body,
        out_shape=(jax.ShapeDtypeStruct((Tp, D), F32),
                   jax.ShapeDtypeStruct((B, 8, D), F32),
                   jax.ShapeDtypeStruct((B, 8, D), F32)),
        grid=(B, nc),
        in_specs=[
            pl.BlockSpec((L, D), lambda b, c: (b * nc + c, 0)),
            _const_spec((1, D)), _const_spec(win.shape), _const_spec(cw.shape), _const_spec((1, D)),
            _const_spec(wa.shape), _const_spec((1, D)), _const_spec(wx.shape), _const_spec((1, D)),
            _const_spec((1, D)), _const_spec(wout.shape),
        ],
        out_specs=(
            pl.BlockSpec((L, D), lambda b, c: (b * nc + c, 0)),
            pl.BlockSpec((1, 8, D), lambda b, c: (b, 0, 0)),
            pl.BlockSpec((1, 8, D), lambda b, c: (b, 0, 0)),
        ),
        scratch_shapes=[pltpu.VMEM((8, D), F32), pltpu.VMEM((8, D), F32)],
        compiler_params=_params(("arbitrary", "arbitrary")),
        name="lru_prompt",
    )(x_p, g, win, cw, cb, wa, ba, wx, bx, lam, wout)


def _lru_sample_body(x_ref, g_ref, win_ref, cw_ref, cb_ref, wa_ref, ba_ref, wx_ref, bx_ref,
                     lam_ref, wout_ref, cs_ref, h0_ref, y_ref, conv_ref, h_ref,
                     xc_scr, *, Bs, Ls):
    Ts, D = x_ref.shape
    x = x_ref[...]
    hn = _rms(x, g_ref[...]).astype(BF16)
    gate = _gelu_tanh(_dot(hn, win_ref[:, :D]))
    x_br = _dot(hn, win_ref[:, D:])
    xc_scr[:, 0:8, :] = cs_ref[...]
    xc_scr[:, 8:8 + Ls, :] = x_br.reshape(Bs, Ls, D)
    conv = cb_ref[...] + xc_scr[:, 5:5 + Ls, :] * cw_ref[0:1, :]
    for j in range(1, CONV_W):
        conv = conv + xc_scr[:, 5 + j:5 + j + Ls, :] * cw_ref[j:j + 1, :]
    conv = conv.reshape(Ts, D)
    a, mult, gi = _lru_gates(conv, wa_ref, wx_ref, ba_ref[...], bx_ref[...], lam_ref[...])
    bvec = mult * gi + a * h0_ref[...].reshape(Ts, D)
    hs = _seg_scan(a, bvec, Ls)
    y_ref[...] = x + _dot((hs * gate).astype(BF16), wout_ref[...])
    h_ref[...] = hs.reshape(Bs, Ls, D)[:, Ls - 8:Ls, :]
    conv_ref[...] = xc_scr[:, Ls:Ls + 8, :]


def _lru_sample(x_s, Bs, Ls, g, win, cw, cb, wa, ba, wx, bx, lam, wout, cs, h0):
    Ts, D = x_s.shape
    assert Ls % 8 == 0
    body = functools.partial(_lru_sample_body, Bs=Bs, Ls=Ls)
    return pl.pallas_call(
        body,
        out_shape=(jax.ShapeDtypeStruct((Ts, D), F32),
                   jax.ShapeDtypeStruct((Bs, 8, D), F32),
                   jax.ShapeDtypeStruct((Bs, 8, D), F32)),
        grid=(1,),
        in_specs=[
            _const_spec((Ts, D)),
            _const_spec((1, D)), _const_spec(win.shape), _const_spec(cw.shape), _const_spec((1, D)),
            _const_spec(wa.shape), _const_spec((1, D)), _const_spec(wx.shape), _const_spec((1, D)),
            _const_spec((1, D)), _const_spec(wout.shape),
            _const_spec((Bs, 8, D)), _const_spec((Bs, Ls, D)),
        ],
        out_specs=(
            pl.BlockSpec((Ts, D), lambda i: (0, 0)),
            pl.BlockSpec((Bs, 8, D), lambda i: (0, 0, 0)),
            pl.BlockSpec((Bs, 8, D), lambda i: (0, 0, 0)),
        ),
        scratch_shapes=[pltpu.VMEM((Bs, Ls + 8, D), F32)],
        compiler_params=_params(("arbitrary",)),
        name="lru_sample",
    )(x_s, g, win, cw, cb, wa, ba, wx, bx, lam, wout, cs, h0)


def _router_body(x_ref, g_ref, wc_ref, bias_ref, cin_ref, text_ref, meta_ref, cnt_ref, run_scr,
                 tri_scr):
    i = pl.program_id(0)
    tm, D = x_ref.shape

    @pl.when(i == 0)
    def _():
        run_scr[...] = cin_ref[...]
        ri = lax.broadcasted_iota(jnp.int32, (tm, tm), 0)
        ci = lax.broadcasted_iota(jnp.int32, (tm, tm), 1)
        tri_scr[...] = (ci < ri).astype(BF16)

    t = _rms(x_ref[...], g_ref[...])
    logits = _dot(t.astype(BF16), wc_ref[...]) + bias_ref[...]
    lane_i = lax.broadcasted_iota(jnp.int32, (tm, LANES), 1)
    lane = lane_i.astype(F32)

    def first_argmax(v):
        m = jnp.max(v, axis=-1, keepdims=True)
        return m, jnp.min(jnp.where(v == m, lane, 1e6), axis=-1, keepdims=True)

    gmask = lane_i < MOE_GROUPS
    gl = jnp.where(gmask, logits, NEG_BIG)
    mg, g_idx = first_argmax(gl)
    p_g = 1.0 / jnp.sum(jnp.where(gmask, jnp.exp(gl - mg), 0.0), axis=-1, keepdims=True)
    e_lane = lane_i - MOE_GROUPS
    lane_grp = (e_lane >> 2).astype(F32)
    emask = (e_lane >= 0) & (e_lane < MOE_EXPERTS) & (lane_grp == g_idx)
    el = jnp.where(emask, logits, NEG_BIG)
    m1, i1 = first_argmax(el)
    m2, i2 = first_argmax(jnp.where(lane == i1, NEG_BIG, el))
    e21 = jnp.exp(m2 - m1)
    w1 = p_g / (1.0 + e21)
    w2 = p_g * e21 / (1.0 + e21)
    first_lower = i1 < i2
    j_lo = jnp.minimum(i1, i2) - (MOE_GROUPS + MOE_PER_GROUP * g_idx)
    j_hi = jnp.maximum(i1, i2) - (MOE_GROUPS + MOE_PER_GROUP * g_idx)
    pair = 0.5 * (j_lo * (7.0 - j_lo)) + (j_hi - j_lo - 1.0)
    cls = g_idx * N_PAIRS + pair
    w_lo = jnp.where(first_lower, w1, w2)
    w_hi = jnp.where(first_lower, w2, w1)

    onehot = (lane == cls).astype(F32)
    prefix = _dot(tri_scr[...], onehot.astype(BF16))
    rank = jnp.sum(onehot * (prefix + run_scr[...]), axis=-1, keepdims=True)
    run_scr[...] = run_scr[...] + jnp.sum(onehot, axis=0, keepdims=True)

    text_ref[:, :D] = t
    text_ref[:, D:] = jnp.where(lane_i == 0, w_lo, jnp.where(lane_i == 1, w_hi, 0.0))
    meta = jnp.where(lane_i == 0, cls, jnp.where(lane_i == 1, rank, 0.0))
    meta_ref[...] = meta[:, :META_W]
    cnt_ref[...] = run_scr[...]


def _router(x, g, wc, bias, cnt_in):
    T, D = x.shape
    tm = _pick_tile(T, ROUTE_TILE)
    return pl.pallas_call(
        _router_body,
        out_shape=(jax.ShapeDtypeStruct((T, D + EXTRA), F32),
                   jax.ShapeDtypeStruct((T, META_W), F32),
                   jax.ShapeDtypeStruct((1, LANES), F32)),
        grid=(T // tm,),
        in_specs=[
            pl.BlockSpec((tm, D), lambda i: (i, 0)),
            _const_spec((1, D)), _const_spec(wc.shape), _const_spec((1, LANES)),
            _const_spec((1, LANES)),
        ],
        out_specs=(
            pl.BlockSpec((tm, D + EXTRA), lambda i: (i, 0)),
            pl.BlockSpec((tm, META_W), lambda i: (i, 0)),
            pl.BlockSpec((1, LANES), lambda i: (0, 0)),
        ),
        scratch_shapes=[pltpu.VMEM((1, LANES), F32), pltpu.VMEM((tm, tm), BF16)],
        compiler_params=_params(("arbitrary",)),
        name="moe_router",
    )(x, g, wc, bias, cnt_in)


def _start_rows(n, idx_ref, row_copy):
    assert n % SUBLANES == 0

    def body(j, carry):
        for u in range(SUBLANES):
            pos = idx_ref[0, 0, j * SUBLANES + u]
            row_copy(j, u, pos >> 3, pos & (SUBLANES - 1)).start(priority=u % 2)
        return carry

    lax.fori_loop(0, n // SUBLANES, body, 0)


def _dispatch_body(dest_ref, t_ref, xs_in_ref, xs_ref, sem):
    del xs_in_ref
    groups = t_ref.shape[0]

    def row_copy(j, u, hi, lo):
        return pltpu.make_async_copy(t_ref.at[j, pl.ds(u, 1), :],
                                     xs_ref.at[hi, pl.ds(lo, 1), :], sem)

    _start_rows(groups * SUBLANES, dest_ref, row_copy)
    pltpu.make_async_copy(t_ref, xs_ref.at[pl.ds(0, groups)], sem).wait()


def _dispatch(t_ext, dest3, xs):
    nb, _, td = dest3.shape
    W = t_ext.shape[1]
    t_ext = t_ext.reshape(-1, SUBLANES, W)
    return pl.pallas_call(
        _dispatch_body,
        out_shape=jax.ShapeDtypeStruct(xs.shape, F32),
        grid=(nb,),
        in_specs=[
            pl.BlockSpec((1, 1, td), lambda i: (i, 0, 0), memory_space=pltpu.SMEM),
            pl.BlockSpec((td // SUBLANES, SUBLANES, W), lambda i: (i, 0, 0)),
            pl.BlockSpec(memory_space=pl.ANY),
        ],
        out_specs=pl.BlockSpec(memory_space=pl.ANY),
        scratch_shapes=[pltpu.SemaphoreType.DMA(())],
        input_output_aliases={2: 0},
        compiler_params=_params(("arbitrary",)),
        name="moe_dispatch",
    )(dest3, t_ext, xs)


def _ffn_body(ea_ref, eb_ref, halves_ref, xs_ref, wga_ref, wua_ref, wda_ref,
              wgb_ref, wub_ref, wdb_ref, ys_ref, wg_scr, wu_scr, wd_scr):
    i = pl.program_id(0)
    tg, D = ys_ref.shape
    nh = halves_ref[i]
    prev = jnp.maximum(i - 1, 0)
    new_class = (i == 0) | (ea_ref[i] != ea_ref[prev]) | (eb_ref[i] != eb_ref[prev])

    @pl.when((nh > 0) & new_class)
    def _():
        for slot, (wg, wu, wd) in enumerate(((wga_ref, wua_ref, wda_ref), (wgb_ref, wub_ref, wdb_ref))):
            wg_scr[slot] = wg[0, 0].astype(BF16)
            wu_scr[slot] = wu[0, 0].astype(BF16)
            wd_scr[slot] = wd[0, 0].astype(BF16)

    def experts(rows):
        x = xs_ref[rows, :D].astype(BF16)
        w_a = xs_ref[rows, D:D + 1]
        w_b = xs_ref[rows, D + 1:D + 2]

        def expert(slot):
            gt = _dot(x, wg_scr[slot])
            hdn = (gt * jax.nn.sigmoid(gt)) * _dot(x, wu_scr[slot])
            return _dot(hdn.astype(BF16), wd_scr[slot])

        return w_a * expert(0) + w_b * expert(1)

    @pl.when(nh == 2)
    def _():
        ys_ref[...] = experts(slice(0, tg))

    @pl.when(nh == 1)
    def _():
        ys_ref[:FFN_HALF, :] = experts(slice(0, FFN_HALF))
        ys_ref[FFN_HALF:, :] = jnp.zeros((tg - FFN_HALF, D), F32)

    @pl.when(nh == 0)
    def _():
        ys_ref[...] = jnp.zeros_like(ys_ref)


def _ffn(xs, ea, eb, halves, layer, wg, wu, wd):
    P, W = xs.shape
    D = W - EXTRA
    ff = wg.shape[3]
    tg = FFN_TILE
    wspec_a = lambda shape: pl.BlockSpec(shape, lambda i, ea, eb, nh: (layer, ea[i], 0, 0))
    wspec_b = lambda shape: pl.BlockSpec(shape, lambda i, ea, eb, nh: (layer, eb[i], 0, 0))
    grid_spec = pltpu.PrefetchScalarGridSpec(
        num_scalar_prefetch=3,
        grid=(P // tg,),
        in_specs=[
            pl.BlockSpec((tg, W), lambda i, ea, eb, nh: (i, 0)),
            wspec_a((1, 1, D, ff)), wspec_a((1, 1, D, ff)), wspec_a((1, 1, ff, D)),
            wspec_b((1, 1, D, ff)), wspec_b((1, 1, D, ff)), wspec_b((1, 1, ff, D)),
        ],
        out_specs=pl.BlockSpec((tg, D), lambda i, ea, eb, nh: (i, 0)),
        scratch_shapes=[pltpu.VMEM((2, D, ff), BF16), pltpu.VMEM((2, D, ff), BF16),
                        pltpu.VMEM((2, ff, D), BF16)],
    )
    return pl.pallas_call(
        _ffn_body,
        out_shape=jax.ShapeDtypeStruct((P, D), F32),
        grid_spec=grid_spec,
        compiler_params=_params(("arbitrary",)),
        name="moe_ffn",
    )(ea, eb, halves, xs, wg, wu, wd, wg, wu, wd)


def _combine_body(dest_ref, dnext_ref, x_ref, gfin_ref, ys_ref, o_ref, ybuf, sems, *, final_norm):
    i, n = pl.program_id(0), pl.num_programs(0)
    tc, D = x_ref.shape
    groups = tc // SUBLANES

    def gather(idx_ref, slot):
        def row_copy(j, u, hi, lo):
            return pltpu.make_async_copy(ys_ref.at[hi, pl.ds(lo, 1), :],
                                         ybuf.at[slot, j, pl.ds(u, 1), :], sems.at[slot])
        _start_rows(tc, idx_ref, row_copy)

    @pl.when(i == 0)
    def _():
        gather(dest_ref, 0)

    for slot in range(2):
        @pl.when((i + 1 < n) & ((i + 1) % 2 == slot))
        def _():
            gather(dnext_ref, slot)

    for slot in range(2):
        @pl.when(i % 2 == slot)
        def _():
            pltpu.make_async_copy(ys_ref.at[pl.ds(0, groups)], ybuf.at[slot], sems.at[slot]).wait()
            y = x_ref[...] + ybuf[slot].reshape(tc, D)
            if final_norm:
                y = _rms(y, gfin_ref[...])
            o_ref[...] = y


def _combine(x, dest3, ys, gfin, final_norm):
    T, D = x.shape
    nb, _, tc = dest3.shape
    body = functools.partial(_combine_body, final_norm=final_norm)
    return pl.pallas_call(
        body,
        out_shape=jax.ShapeDtypeStruct((T, D), F32),
        grid=(nb,),
        in_specs=[
            pl.BlockSpec((1, 1, tc), lambda i: (i, 0, 0), memory_space=pltpu.SMEM),
            pl.BlockSpec((1, 1, tc), lambda i: (jnp.minimum(i + 1, nb - 1), 0, 0),
                         memory_space=pltpu.SMEM),
            pl.BlockSpec((tc, D), lambda i: (i, 0)),
            _const_spec((1, D)),
            pl.BlockSpec(memory_space=pl.ANY),
        ],
        out_specs=pl.BlockSpec((tc, D), lambda i: (i, 0)),
        scratch_shapes=[pltpu.VMEM((2, tc // SUBLANES, SUBLANES, D), F32),
                        pltpu.SemaphoreType.DMA((2,))],
        compiler_params=_params(("arbitrary",)),
        name="moe_combine",
    )(dest3, dest3, x, gfin, ys.reshape(-1, SUBLANES, D))


def _moe_tables(metas, counts, P):
    tg = FFN_TILE
    cnt = counts[0, :N_CLASSES].astype(jnp.int32)
    padded = ((cnt + tg - 1) // tg) * tg
    ends = jnp.cumsum(padded)
    offs = ends - padded
    dests = [jnp.take(offs, m[:, 0].astype(jnp.int32)) + m[:, 1].astype(jnp.int32) for m in metas]
    tile_start = jnp.arange(P // tg, dtype=jnp.int32) * tg
    tile_cls = jnp.minimum(jnp.sum((tile_start[:, None] >= ends[None, :]).astype(jnp.int32), axis=1),
                           N_CLASSES - 1)
    rows = jnp.clip(jnp.take(cnt, tile_cls) - (tile_start - jnp.take(offs, tile_cls)), 0, tg)
    halves = jnp.where(tile_start < ends[-1], (rows + FFN_HALF - 1) // FFN_HALF, 0).astype(jnp.int32)
    grp = tile_cls // N_PAIRS
    pair = tile_cls % N_PAIRS
    ea = grp * MOE_PER_GROUP + jnp.take(jnp.asarray(_PAIR_LO), pair)
    eb = grp * MOE_PER_GROUP + jnp.take(jnp.asarray(_PAIR_HI), pair)
    return dests, ea, eb, halves


def _moe(xs_in, g, w_group, b_group, w_router, b_router, layer, wg, wu, wd, gfin, final_norm):
    D = xs_in[0].shape[1]
    T = sum(x.shape[0] for x in xs_in)
    wcat = jnp.zeros((D, LANES), F32)
    wcat = wcat.at[:, :MOE_GROUPS].set(w_group).at[:, MOE_GROUPS:MOE_GROUPS + MOE_EXPERTS].set(w_router)
    bias = jnp.zeros((1, LANES), F32)
    bias = bias.at[0, :MOE_GROUPS].set(b_group).at[0, MOE_GROUPS:MOE_GROUPS + MOE_EXPERTS].set(b_router)
    wc = wcat.astype(BF16)

    counts = jnp.zeros((1, LANES), F32)
    texts, metas = [], []
    for x in xs_in:
        t_ext, meta, counts = _router(x, g, wc, bias, counts)
        texts.append(t_ext)
        metas.append(meta)

    tg = FFN_TILE
    P = ((T + tg - 1) // tg + N_CLASSES) * tg
    dests, ea, eb, halves = _moe_tables(metas, counts, P)
    dest3s = []
    xs = jnp.zeros((P // SUBLANES, SUBLANES, D + EXTRA), F32)
    for x, t_ext, dest in zip(xs_in, texts, dests):
        tmove = _pick_tile(x.shape[0], MOVE_TILE)
        dest3s.append(dest.reshape(x.shape[0] // tmove, 1, tmove))
        xs = _dispatch(t_ext, dest3s[-1], xs)
    ys = _ffn(xs.reshape(P, D + EXTRA), ea, eb, halves, layer, wg, wu, wd)
    return [_combine(x, dest3, ys, gfin, final_norm) for x, dest3 in zip(xs_in, dest3s)]


def _rope_tables(pos, dk):
    half = dk // 2
    inv = 1.0 / (ROPE_BASE ** jnp.linspace(0.0, 1.0, half, dtype=F32))
    ang = pos.astype(F32)[:, None] * inv[None, :]
    return jnp.cos(ang), jnp.sin(ang)


def kernel(x_prompt, x_sample, state_ret, state_conv, state_lru, norm_mix_g, norm_ffn_g, norm_final_g, ret_w_in, ret_w_out, ret_norm_g, lru_w_in, lru_conv_w, lru_conv_b, lru_w_a, lru_b_a, lru_w_x, lru_b_x, lru_lambda, lru_w_out, moe_w_group, moe_b_group, moe_w_router, moe_b_router, moe_w_gate, moe_w_up, moe_w_down):
    B, S, D = x_prompt.shape
    Bs, Ls, _ = x_sample.shape
    Tp, Ts = B * S, Bs * Ls
    assert norm_mix_g.shape[0] == 2
    dk = D // RET_HEADS
    row = lambda v: v.reshape(1, -1)
    gfin = row(norm_final_g)

    cos_p, sin_p = _rope_tables(jnp.arange(S), dk)
    cos_s, sin_s = _rope_tables(PAST_LEN + jnp.arange(Ls), dk)
    ret_args = (row(norm_mix_g[0]), ret_w_in[0].astype(BF16), ret_w_out[0].astype(BF16),
                row(ret_norm_g[0]))
    xp, ret_p = _ret_prompt(x_prompt.reshape(Tp, D), B, S, *ret_args, cos_p, sin_p)
    xs, ret_s = _ret_sample(x_sample.reshape(Ts, D), Bs, Ls, *ret_args, cos_s, sin_s, state_ret[0])
    xp, xs = _moe([xp, xs], row(norm_ffn_g[0]), moe_w_group[0], moe_b_group[0], moe_w_router[0],
                  moe_b_router[0], 0, moe_w_gate, moe_w_up, moe_w_down, gfin, False)

    lru_args = (row(norm_mix_g[1]), lru_w_in[0].astype(BF16), lru_conv_w[0], row(lru_conv_b[0]),
                lru_w_a[0].astype(BF16), row(lru_b_a[0]), lru_w_x[0].astype(BF16), row(lru_b_x[0]),
                row(lru_lambda[0]), lru_w_out[0].astype(BF16))
    cs8 = jnp.pad(state_conv[0], ((0, 0), (8 - (CONV_W - 1), 0), (0, 0)))
    h0_rows = jnp.pad(state_lru[0].reshape(Bs, 1, D), ((0, 0), (0, Ls - 1), (0, 0)))
    xp, conv_p, lru_p = _lru_prompt(xp, B, S, *lru_args)
    xs, conv_s, lru_s = _lru_sample(xs, Bs, Ls, *lru_args, cs8, h0_rows)
    y_p, y_s = _moe([xp, xs], row(norm_ffn_g[1]), moe_w_group[1], moe_b_group[1], moe_w_router[1],
                    moe_b_router[1], 1, moe_w_gate, moe_w_up, moe_w_down, gfin, True)

    return (y_p.reshape(B, S, D), y_s.reshape(Bs, Ls, D),
            ret_p[None], conv_p[None, :, 8 - (CONV_W - 1):], lru_p[None, :, 7],
            ret_s[None], conv_s[None, :, 8 - (CONV_W - 1):], lru_s[None, :, 7])
```

```python
import functools
import math

import numpy as np
import jax
import jax.numpy as jnp
from jax import lax
from jax.experimental import pallas as pl
from jax.experimental.pallas import tpu as pltpu

F32 = jnp.float32
BF16 = jnp.bfloat16

RET_HEADS = 4
RET_CHUNK = 64
ROPE_BASE = 10000.0
CONV_W = 4
LRU_BLOCKS = 4
LRU_C = 8.0
MOE_GROUPS = 4
MOE_PER_GROUP = 4
MOE_EXPERTS = MOE_GROUPS * MOE_PER_GROUP
NORM_EPS = 1e-6
PAST_LEN = 2048

LANES = 128
SUBLANES = 8
VMEM_LIMIT_BYTES = 56 * 1024 * 1024

RET_TILE = 512
RET_SCORE_BLOCK = 256
LRU_TILE = 512
ROUTE_TILE = 1024
MOVE_TILE = 512
FFN_TILE = 512
FFN_HALF = 256
N_PAIRS = 6
N_CLASSES = MOE_GROUPS * N_PAIRS
EXTRA = LANES
META_W = 8
NEG_BIG = -1e30

_PAIR_LO = np.array([0, 0, 0, 1, 1, 2], np.int32)
_PAIR_HI = np.array([1, 2, 3, 2, 3, 3], np.int32)


def _log_gammas():
    return [math.log1p(-(2.0 ** (-5.0 - h))) for h in range(RET_HEADS)]


def _rms(x, g):
    return x * lax.rsqrt(jnp.mean(x * x, axis=-1, keepdims=True) + NORM_EPS) * g


def _dot(a, b):
    return jnp.dot(a, b, preferred_element_type=F32)


def _dot_nt(a, b):
    return lax.dot_general(a, b, (((1,), (1,)), ((), ())), preferred_element_type=F32)


def _dot_tn(a, b):
    return lax.dot_general(a, b, (((0,), (0,)), ((), ())), preferred_element_type=F32)


def _const_spec(shape):
    nd = len(shape)
    return pl.BlockSpec(shape, lambda *_: (0,) * nd, pipeline_mode=pl.Buffered(1))


def _params(sem):
    return pltpu.CompilerParams(dimension_semantics=sem, vmem_limit_bytes=VMEM_LIMIT_BYTES)


def _pick_tile(n, pref):
    t = pref
    while n % t:
        t //= 2
    assert t >= SUBLANES, (n, pref)
    return t


def _decay_matrix(L, chunk, lg):
    ii = lax.broadcasted_iota(jnp.int32, (L, L), 0)
    jj = lax.broadcasted_iota(jnp.int32, (L, L), 1)
    diff = (ii - jj).astype(F32)
    same_chunk = (ii // chunk) == (jj // chunk)
    return jnp.where(same_chunk & (ii >= jj), jnp.exp(jnp.maximum(diff, 0.0) * lg), 0.0)


def _rotary(t, cos, sin):
    half = t.shape[-1] // 2
    t1, t2 = t[:, :half], t[:, half:]
    return jnp.concatenate([t1 * cos - t2 * sin, t1 * sin + t2 * cos], axis=-1)


def _ret_proj(hn, win_ref, hd, dk, dv):
    qk, vv = RET_HEADS * dk, RET_HEADS * dv
    q = _dot(hn, win_ref[:, hd * dk:(hd + 1) * dk])
    k = _dot(hn, win_ref[:, qk + hd * dk:qk + (hd + 1) * dk])
    v = _dot(hn, win_ref[:, 2 * qk + hd * dv:2 * qk + (hd + 1) * dv])
    gt = _dot(hn, win_ref[:, 2 * qk + vv + hd * dv:2 * qk + vv + (hd + 1) * dv])
    return q, k, v, gt


def _ret_head(q, k, v, gt, cos, sin, decay, s0, ng, chunk, lg, kv_scr=None):
    L, dk = q.shape
    qr = _rotary(q, cos, sin)
    kr = _rotary(k, cos, sin) * (dk ** -0.5)
    idx = (lax.broadcasted_iota(jnp.int32, (L, 1), 0) % chunk).astype(F32)
    qd = (qr * jnp.exp((idx + 1.0) * lg)).astype(BF16)
    kd = (kr * jnp.exp((chunk - 1.0 - idx) * lg)).astype(BF16)
    qb, kb, vb = qr.astype(BF16), kr.astype(BF16), v.astype(BF16)
    core = decay.shape[0]
    chunks = [slice(c * chunk, (c + 1) * chunk) for c in range(L // chunk)]
    if kv_scr is not None:
        for c, rows in enumerate(chunks):
            kv_scr[c] = _dot_tn(kd[rows], vb[rows])
    s, parts = s0, []
    for r0 in range(0, L, core):
        blk = slice(r0, r0 + core)
        scores = _dot_nt(qb[blk], kb[blk]) * decay
        intra = _dot(scores.astype(BF16), vb[blk])
        for c in range(r0 // chunk, (r0 + core) // chunk):
            rows = chunks[c]
            parts.append(intra[rows.start - r0:rows.stop - r0] + _dot(qd[rows], s.astype(BF16)))
            kv = kv_scr[c] if kv_scr is not None else _dot_tn(kd[rows], vb[rows])
            s = math.exp(chunk * lg) * s + kv
    o = parts[0] if len(parts) == 1 else jnp.concatenate(parts, axis=0)
    o = o * lax.rsqrt(jnp.mean(o * o, axis=-1, keepdims=True) + NORM_EPS)
    og = (gt * jax.nn.sigmoid(gt)) * (o * ng)
    return og, s


def _ret_prompt_body(x_ref, g_ref, win_ref, wout_ref, ng_ref, cos_ref, sin_ref,
                     y_ref, s_ref, decay_scr, og_scr, kv_scr, *, L, chunk, dk, dv):
    b, c = pl.program_id(0), pl.program_id(1)
    lgs = _log_gammas()

    @pl.when((b == 0) & (c == 0))
    def _():
        for hd in range(RET_HEADS):
            decay_scr[hd] = _decay_matrix(decay_scr.shape[1], chunk, lgs[hd])

    @pl.when(c == 0)
    def _():
        s_ref[...] = jnp.zeros_like(s_ref)

    x = x_ref[...]
    hn = _rms(x, g_ref[...]).astype(BF16)
    cos, sin = cos_ref[...], sin_ref[...]
    for hd in range(RET_HEADS):
        og, s_new = _ret_head(*_ret_proj(hn, win_ref, hd, dk, dv), cos, sin, decay_scr[hd],
                              s_ref[0, hd], ng_ref[:, hd * dv:(hd + 1) * dv], chunk, lgs[hd],
                              kv_scr)
        s_ref[0, hd] = s_new
        og_scr[:, hd * dv:(hd + 1) * dv] = og.astype(BF16)
    y_ref[...] = x + _dot(og_scr[...], wout_ref[...])


def _ret_prompt(x_p, B, S, g, win, wout, ng, cos, sin):
    Tp, D = x_p.shape
    dk, dv = D // RET_HEADS, 2 * D // RET_HEADS
    chunk = min(S, RET_CHUNK)
    L = _pick_tile(S, RET_TILE)
    core = min(L, RET_SCORE_BLOCK)
    assert L % core == 0 and core % chunk == 0
    nc = S // L
    body = functools.partial(_ret_prompt_body, L=L, chunk=chunk, dk=dk, dv=dv)
    return pl.pallas_call(
        body,
        out_shape=(jax.ShapeDtypeStruct((Tp, D), F32),
                   jax.ShapeDtypeStruct((B, RET_HEADS, dk, dv), F32)),
        grid=(B, nc),
        in_specs=[
            pl.BlockSpec((L, D), lambda b, c: (b * nc + c, 0)),
            _const_spec((1, D)),
            _const_spec(win.shape),
            _const_spec(wout.shape),
            _const_spec((1, RET_HEADS * dv)),
            pl.BlockSpec((L, dk // 2), lambda b, c: (c, 0)),
            pl.BlockSpec((L, dk // 2), lambda b, c: (c, 0)),
        ],
        out_specs=(
            pl.BlockSpec((L, D), lambda b, c: (b * nc + c, 0)),
            pl.BlockSpec((1, RET_HEADS, dk, dv), lambda b, c: (b, 0, 0, 0)),
        ),
        scratch_shapes=[pltpu.VMEM((RET_HEADS, core, core), F32),
                        pltpu.VMEM((L, RET_HEADS * dv), BF16),
                        pltpu.VMEM((L // chunk, dk, dv), F32)],
        compiler_params=_params(("arbitrary", "arbitrary")),
        name="ret_prompt",
    )(x_p, g, win, wout, ng, cos, sin)


def _ret_sample_body(x_ref, g_ref, win_ref, wout_ref, ng_ref, cos_ref, sin_ref, s0_ref,
                     y_ref, s_ref, q_scr, k_scr, v_scr, gt_scr, og_scr, *, L, dk, dv):
    b, nb = pl.program_id(0), pl.num_programs(0)
    lgs = _log_gammas()

    @pl.when(b == 0)
    def _():
        hn = _rms(x_ref[...], g_ref[...]).astype(BF16)
        for hd in range(RET_HEADS):
            q, k, v, gt = _ret_proj(hn, win_ref, hd, dk, dv)
            q_scr[hd], k_scr[hd], v_scr[hd], gt_scr[hd] = q, k, v, gt

    rows = pl.ds(pl.multiple_of(b * L, L), L)
    cos, sin = cos_ref[...], sin_ref[...]
    for hd in range(RET_HEADS):
        og, s_new = _ret_head(q_scr[hd, rows, :], k_scr[hd, rows, :], v_scr[hd, rows, :],
                              gt_scr[hd, rows, :], cos, sin, _decay_matrix(L, L, lgs[hd]),
                              s0_ref[0, hd], ng_ref[:, hd * dv:(hd + 1) * dv], L, lgs[hd])
        s_ref[0, hd] = s_new
        og_scr[rows, hd * dv:(hd + 1) * dv] = og.astype(BF16)

    @pl.when(b == nb - 1)
    def _():
        y_ref[...] = x_ref[...] + _dot(og_scr[...], wout_ref[...])


def _ret_sample(x_s, Bs, Ls, g, win, wout, ng, cos, sin, s0):
    Ts, D = x_s.shape
    dk, dv = D // RET_HEADS, 2 * D // RET_HEADS
    assert Ls <= RET_CHUNK
    body = functools.partial(_ret_sample_body, L=Ls, dk=dk, dv=dv)
    return pl.pallas_call(
        body,
        out_shape=(jax.ShapeDtypeStruct((Ts, D), F32),
                   jax.ShapeDtypeStruct((Bs, RET_HEADS, dk, dv), F32)),
        grid=(Bs,),
        in_specs=[
            _const_spec((Ts, D)),
            _const_spec((1, D)),
            _const_spec(win.shape),
            _const_spec(wout.shape),
            _const_spec((1, RET_HEADS * dv)),
            _const_spec((Ls, dk // 2)),
            _const_spec((Ls, dk // 2)),
            pl.BlockSpec((1, RET_HEADS, dk, dv), lambda b: (b, 0, 0, 0)),
        ],
        out_specs=(
            pl.BlockSpec((Ts, D), lambda b: (0, 0)),
            pl.BlockSpec((1, RET_HEADS, dk, dv), lambda b: (b, 0, 0, 0)),
        ),
        scratch_shapes=[pltpu.VMEM((RET_HEADS, Ts, dk), F32), pltpu.VMEM((RET_HEADS, Ts, dk), F32),
                        pltpu.VMEM((RET_HEADS, Ts, dv), F32), pltpu.VMEM((RET_HEADS, Ts, dv), F32),
                        pltpu.VMEM((Ts, RET_HEADS * dv), BF16)],
        compiler_params=_params(("arbitrary",)),
        name="ret_sample",
    )(x_s, g, win, wout, ng, cos, sin, s0)


def _gelu_tanh(x):
    return 0.5 * x * (1.0 + jnp.tanh(math.sqrt(2.0 / math.pi) * (x + 0.044715 * (x * x * x))))


def _softplus(z):
    return jnp.maximum(z, 0.0) + jnp.log1p(jnp.exp(-jnp.abs(z)))


def _lru_gates(conv, wa_ref, wx_ref, ba, bx, lam):
    D = conv.shape[1]
    bw = D // LRU_BLOCKS
    ra, ix = [], []
    for n in range(LRU_BLOCKS):
        cb = conv[:, n * bw:(n + 1) * bw].astype(BF16)
        ra.append(_dot(cb, wa_ref[n]))
        ix.append(_dot(cb, wx_ref[n]))
    r = jax.nn.sigmoid(jnp.concatenate(ra, axis=-1) + ba)
    i = jax.nn.sigmoid(jnp.concatenate(ix, axis=-1) + bx)
    log_a = r * ((-LRU_C) * _softplus(-lam))
    a = jnp.exp(log_a)
    mult = jnp.sqrt(jnp.tanh(-log_a) * (1.0 + a * a))
    return a, mult, i * conv


def _seg_scan(a, b, seg):
    R = a.shape[0]
    row = lax.broadcasted_iota(jnp.int32, (R, 1), 0) % seg
    k = 1
    while k < seg:
        keep = row >= k
        a_sh = jnp.where(keep, pltpu.roll(a, k, axis=0), 1.0)
        b_sh = jnp.where(keep, pltpu.roll(b, k, axis=0), 0.0)
        b = a * b_sh + b
        a = a * a_sh
        k *= 2
    return b


def _carry_scan(a, b, h_prev):
    L, D = a.shape
    G = L // SUBLANES
    a3, b3 = a.reshape(G, SUBLANES, D), b.reshape(G, SUBLANES, D)
    sub = lax.broadcasted_iota(jnp.int32, (SUBLANES, D), 0)[None]
    k = 1
    while k < SUBLANES:
        keep = sub >= k
        a_sh = jnp.where(keep, pltpu.roll(a3, k, axis=1), 1.0)
        b_sh = jnp.where(keep, pltpu.roll(b3, k, axis=1), 0.0)
        b3 = a3 * b_sh + b3
        a3 = a3 * a_sh
        k *= 2
    carry, outs = h_prev, []
    for j in range(G):
        r = a3[j] * carry + b3[j]
        outs.append(r)
        carry = r[SUBLANES - 1:SUBLANES, :]
    return jnp.concatenate(outs, axis=0)


def _lru_prompt_body(x_ref, g_ref, win_ref, cw_ref, cb_ref, wa_ref, ba_ref, wx_ref, bx_ref,
                     lam_ref, wout_ref, y_ref, conv_ref, h_ref, xc_scr, hprev_scr, *, L):
    c = pl.program_id(1)
    D = x_ref.shape[1]

    @pl.when(c == 0)
    def _():
        xc_scr[...] = jnp.zeros_like(xc_scr)
        hprev_scr[...] = jnp.zeros_like(hprev_scr)

    x = x_ref[...]
    hn = _rms(x, g_ref[...]).astype(BF16)
    gate = _gelu_tanh(_dot(hn, win_ref[:, :D]))
    x_br = _dot(hn, win_ref[:, D:])
    prev8 = xc_scr[...]
    sub8 = lax.broadcasted_iota(jnp.int32, (SUBLANES, 1), 0)

    def shifted(k):
        rolled = pltpu.roll(x_br, k, axis=0)
        first = jnp.where(sub8 < k, pltpu.roll(prev8, k, axis=0), rolled[0:SUBLANES, :])
        return jnp.concatenate([first, rolled[SUBLANES:, :]], axis=0)

    conv = cb_ref[...] + shifted(CONV_W - 1) * cw_ref[0:1, :]
    for j in range(1, CONV_W - 1):
        conv = conv + shifted(CONV_W - 1 - j) * cw_ref[j:j + 1, :]
    conv = conv + x_br * cw_ref[CONV_W - 1:CONV_W, :]
    a, mult, gi = _lru_gates(conv, wa_ref, wx_ref, ba_ref[...], bx_ref[...], lam_ref[...])
    first = jnp.where((sub8 == 0) & (c == 0), 1.0, mult[0:SUBLANES, :])
    mult = jnp.concatenate([first, mult[SUBLANES:, :]], axis=0)
    hs = _carry_scan(a, mult * gi, hprev_scr[7:8, :])
    y_ref[...] = x + _dot((hs * gate).astype(BF16), wout_ref[...])
    hprev_scr[...] = hs[L - 8:L, :]
    h_ref[0] = hs[L - 8:L, :]
    conv_ref[0] = x_br[L - 8:L, :]
    xc_scr[...] = x_br[L - 8:L, :]


def _lru_prompt(x_p, B, S, g, win, cw, cb, wa, ba, wx, bx, lam, wout):
    Tp, D = x_p.shape
    L = _pick_tile(S, LRU_TILE)
    nc = S // L
    body = functools.partial(_lru_prompt_body, L=L)
    return pl.pallas_call(
        body,
        out_shape=(jax.ShapeDtypeStruct((Tp, D), F32),
                   jax.ShapeDtypeStruct((B, 8, D), F32),
                   jax.ShapeDtypeStruct((B, 8, D), F32)),
        grid=(B, nc),
        in_specs=[
            pl.BlockSpec((L, D), lambda b, c: (b * nc + c, 0)),
            _const_spec((1, D)), _const_spec(win.shape), _const_spec(cw.shape), _const_spec((1, D)),
            _const_spec(wa.shape), _const_spec((1, D)), _const_spec(wx.shape), _const_spec((1, D)),
            _const_spec((1, D)), _const_spec(wout.shape),
        ],
        out_specs=(
            pl.BlockSpec((L, D), lambda b, c: (b * nc + c, 0)),
            pl.BlockSpec((1, 8, D), lambda b, c: (b, 0, 0)),
            pl.BlockSpec((1, 8, D), lambda b, c: (b, 0, 0)),
        ),
        scratch_shapes=[pltpu.VMEM((8, D), F32), pltpu.VMEM((8, D), F32)],
        compiler_params=_params(("arbitrary", "arbitrary")),
        name="lru_prompt",
    )(x_p, g, win, cw, cb, wa, ba, wx, bx, lam, wout)


def _lru_sample_body(x_ref, g_ref, win_ref, cw_ref, cb_ref, wa_ref, ba_ref, wx_ref, bx_ref,
                     lam_ref, wout_ref, cs_ref, h0_ref, y_ref, conv_ref, h_ref,
                     xc_scr, *, Bs, Ls):
    Ts, D = x_ref.shape
    x = x_ref[...]
    hn = _rms(x, g_ref[...]).astype(BF16)
    gate = _gelu_tanh(_dot(hn, win_ref[:, :D]))
    x_br = _dot(hn, win_ref[:, D:])
    xc_scr[:, 0:8, :] = cs_ref[...]
    xc_scr[:, 8:8 + Ls, :] = x_br.reshape(Bs, Ls, D)
    conv = cb_ref[...] + xc_scr[:, 5:5 + Ls, :] * cw_ref[0:1, :]
    for j in range(1, CONV_W):
        conv = conv + xc_scr[:, 5 + j:5 + j + Ls, :] * cw_ref[j:j + 1, :]
    conv = conv.reshape(Ts, D)
    a, mult, gi = _lru_gates(conv, wa_ref, wx_ref, ba_ref[...], bx_ref[...], lam_ref[...])
    bvec = mult * gi + a * h0_ref[...].reshape(Ts, D)
    hs = _seg_scan(a, bvec, Ls)
    y_ref[...] = x + _dot((hs * gate).astype(BF16), wout_ref[...])
    h_ref[...] = hs.reshape(Bs, Ls, D)[:, Ls - 8:Ls, :]
    conv_ref[...] = xc_scr[:, Ls:Ls + 8, :]


def _lru_sample(x_s, Bs, Ls, g, win, cw, cb, wa, ba, wx, bx, lam, wout, cs, h0):
    Ts, D = x_s.shape
    assert Ls % 8 == 0
    body = functools.partial(_lru_sample_body, Bs=Bs, Ls=Ls)
    return pl.pallas_call(
        body,
        out_shape=(jax.ShapeDtypeStruct((Ts, D), F32),
                   jax.ShapeDtypeStruct((Bs, 8, D), F32),
                   jax.ShapeDtypeStruct((Bs, 8, D), F32)),
        grid=(1,),
        in_specs=[
            _const_spec((Ts, D)),
            _const_spec((1, D)), _const_spec(win.shape), _const_spec(cw.shape), _const_spec((1, D)),
            _const_spec(wa.shape), _const_spec((1, D)), _const_spec(wx.shape), _const_spec((1, D)),
            _const_spec((1, D)), _const_spec(wout.shape),
            _const_spec((Bs, 8, D)), _const_spec((Bs, Ls, D)),
        ],
        out_specs=(
            pl.BlockSpec((Ts, D), lambda i: (0, 0)),
            pl.BlockSpec((Bs, 8, D), lambda i: (0, 0, 0)),
            pl.BlockSpec((Bs, 8, D), lambda i: (0, 0, 0)),
        ),
        scratch_shapes=[pltpu.VMEM((Bs, Ls + 8, D), F32)],
        compiler_params=_params(("arbitrary",)),
        name="lru_sample",
    )(x_s, g, win, cw, cb, wa, ba, wx, bx, lam, wout, cs, h0)


def _router_body(x_ref, g_ref, wc_ref, bias_ref, cin_ref, text_ref, meta_ref, cnt_ref, run_scr,
                 tri_scr):
    i = pl.program_id(0)
    tm, D = x_ref.shape

    @pl.when(i == 0)
    def _():
        run_scr[...] = cin_ref[...]
        ri = lax.broadcasted_iota(jnp.int32, (tm, tm), 0)
        ci = lax.broadcasted_iota(jnp.int32, (tm, tm), 1)
        tri_scr[...] = (ci < ri).astype(BF16)

    t = _rms(x_ref[...], g_ref[...])
    logits = _dot(t.astype(BF16), wc_ref[...]) + bias_ref[...]
    lane_i = lax.broadcasted_iota(jnp.int32, (tm, LANES), 1)
    lane = lane_i.astype(F32)

    def first_argmax(v):
        m = jnp.max(v, axis=-1, keepdims=True)
        return m, jnp.min(jnp.where(v == m, lane, 1e6), axis=-1, keepdims=True)

    gmask = lane_i < MOE_GROUPS
    gl = jnp.where(gmask, logits, NEG_BIG)
    mg, g_idx = first_argmax(gl)
    p_g = 1.0 / jnp.sum(jnp.where(gmask, jnp.exp(gl - mg), 0.0), axis=-1, keepdims=True)
    e_lane = lane_i - MOE_GROUPS
    lane_grp = (e_lane >> 2).astype(F32)
    emask = (e_lane >= 0) & (e_lane < MOE_EXPERTS) & (lane_grp == g_idx)
    el = jnp.where(emask, logits, NEG_BIG)
    m1, i1 = first_argmax(el)
    m2, i2 = first_argmax(jnp.where(lane == i1, NEG_BIG, el))
    e21 = jnp.exp(m2 - m1)
    w1 = p_g / (1.0 + e21)
    w2 = p_g * e21 / (1.0 + e21)
    first_lower = i1 < i2
    j_lo = jnp.minimum(i1, i2) - (MOE_GROUPS + MOE_PER_GROUP * g_idx)
    j_hi = jnp.maximum(i1, i2) - (MOE_GROUPS + MOE_PER_GROUP * g_idx)
    pair = 0.5 * (j_lo * (7.0 - j_lo)) + (j_hi - j_lo - 1.0)
    cls = g_idx * N_PAIRS + pair
    w_lo = jnp.where(first_lower, w1, w2)
    w_hi = jnp.where(first_lower, w2, w1)

    onehot = (lane == cls).astype(F32)
    prefix = _dot(tri_scr[...], onehot.astype(BF16))
    rank = jnp.sum(onehot * (prefix + run_scr[...]), axis=-1, keepdims=True)
    run_scr[...] = run_scr[...] + jnp.sum(onehot, axis=0, keepdims=True)

    text_ref[:, :D] = t
    text_ref[:, D:] = jnp.where(lane_i == 0, w_lo, jnp.where(lane_i == 1, w_hi, 0.0))
    meta = jnp.where(lane_i == 0, cls, jnp.where(lane_i == 1, rank, 0.0))
    meta_ref[...] = meta[:, :META_W]
    cnt_ref[...] = run_scr[...]


def _router(x, g, wc, bias, cnt_in):
    T, D = x.shape
    tm = _pick_tile(T, ROUTE_TILE)
    return pl.pallas_call(
        _router_body,
        out_shape=(jax.ShapeDtypeStruct((T, D + EXTRA), F32),
                   jax.ShapeDtypeStruct((T, META_W), F32),
                   jax.ShapeDtypeStruct((1, LANES), F32)),
        grid=(T // tm,),
        in_specs=[
            pl.BlockSpec((tm, D), lambda i: (i, 0)),
            _const_spec((1, D)), _const_spec(wc.shape), _const_spec((1, LANES)),
            _const_spec((1, LANES)),
        ],
        out_specs=(
            pl.BlockSpec((tm, D + EXTRA), lambda i: (i, 0)),
            pl.BlockSpec((tm, META_W), lambda i: (i, 0)),
            pl.BlockSpec((1, LANES), lambda i: (0, 0)),
        ),
        scratch_shapes=[pltpu.VMEM((1, LANES), F32), pltpu.VMEM((tm, tm), BF16)],
        compiler_params=_params(("arbitrary",)),
        name="moe_router",
    )(x, g, wc, bias, cnt_in)


def _start_rows(n, idx_ref, row_copy):
    assert n % SUBLANES == 0

    def body(j, carry):
        for u in range(SUBLANES):
            pos = idx_ref[0, 0, j * SUBLANES + u]
            row_copy(j, u, pos >> 3, pos & (SUBLANES - 1)).start(priority=u % 2)
        return carry

    lax.fori_loop(0, n // SUBLANES, body, 0)


def _dispatch_body(zinfo_ref, dest_ref, tp_ref, ts_ref, xs_ref, zbuf, sem, zsem, *, nbp, steps):
    i = pl.program_id(0)
    zgroups = zbuf.shape[0]

    def zero_copies():
        out = []
        for c in range(N_CLASSES):
            r0, r1 = zinfo_ref[c], zinfo_ref[N_CLASSES + c]
            nhead = jnp.minimum(r1, (r0 + SUBLANES - 1) & -SUBLANES) - r0
            for k in range(SUBLANES - 1):
                row = r0 + k
                out.append((k < nhead, pltpu.make_async_copy(
                    zbuf.at[0, pl.ds(0, 1), :],
                    xs_ref.at[row >> 3, pl.ds(row & (SUBLANES - 1), 1), :], zsem)))
            g0 = (r0 + SUBLANES - 1) >> 3
            n = jnp.maximum((r1 >> 3) - g0, 0)
            b = zgroups // 2
            while b >= 1:
                out.append(((n & b) != 0, pltpu.make_async_copy(
                    zbuf.at[pl.ds(0, b)], xs_ref.at[pl.ds(g0 + (n & -(2 * b)), b)], zsem)))
                b //= 2
        for k in range(N_CLASSES):
            step = zinfo_ref[2 * N_CLASSES] + k
            out.append((step < steps, pltpu.make_async_copy(
                zbuf, xs_ref.at[pl.ds(jnp.minimum(step, steps - 1) * zgroups, zgroups)], zsem)))
        return out

    @pl.when(i == 0)
    def _():
        zbuf[...] = jnp.zeros_like(zbuf)
        for pred, cp in zero_copies():
            @pl.when(pred)
            def _():
                cp.start()
        for pred, cp in zero_copies():
            @pl.when(pred)
            def _():
                cp.wait()

    def scatter(t_ref):
        groups = t_ref.shape[0]

        def row_copy(j, u, hi, lo):
            return pltpu.make_async_copy(t_ref.at[j, pl.ds(u, 1), :],
                                         xs_ref.at[hi, pl.ds(lo, 1), :], sem)

        _start_rows(groups * SUBLANES, dest_ref, row_copy)
        pltpu.make_async_copy(t_ref, xs_ref.at[pl.ds(0, groups)], sem).wait()

    @pl.when(i < nbp)
    def _():
        scatter(tp_ref)

    @pl.when(i >= nbp)
    def _():
        scatter(ts_ref)


def _dispatch(t_p, t_s, dest3, zinfo, P):
    nb, _, td = dest3.shape
    W = t_p.shape[1]
    assert t_p.shape[0] % td == 0 and t_s.shape[0] % td == 0 and FFN_TILE % SUBLANES == 0
    nbp, nbs = t_p.shape[0] // td, t_s.shape[0] // td
    assert nb == nbp + nbs
    zgroups = FFN_TILE // SUBLANES
    body = functools.partial(_dispatch_body, nbp=nbp, steps=P // FFN_TILE)
    grid_spec = pltpu.PrefetchScalarGridSpec(
        num_scalar_prefetch=1,
        grid=(nb,),
        in_specs=[
            pl.BlockSpec((1, 1, td), lambda i, z: (i, 0, 0), memory_space=pltpu.SMEM),
            pl.BlockSpec((td // SUBLANES, SUBLANES, W), lambda i, z: (jnp.minimum(i, nbp - 1), 0, 0)),
            pl.BlockSpec((td // SUBLANES, SUBLANES, W), lambda i, z: (jnp.maximum(i - nbp, 0), 0, 0)),
        ],
        out_specs=pl.BlockSpec(memory_space=pl.ANY),
        scratch_shapes=[pltpu.VMEM((zgroups, SUBLANES, W), F32),
                        pltpu.SemaphoreType.DMA(()), pltpu.SemaphoreType.DMA(())],
    )
    return pl.pallas_call(
        body,
        out_shape=jax.ShapeDtypeStruct((P // SUBLANES, SUBLANES, W), F32),
        grid_spec=grid_spec,
        compiler_params=_params(("arbitrary",)),
        name="moe_dispatch",
    )(zinfo, dest3, t_p.reshape(-1, SUBLANES, W), t_s.reshape(-1, SUBLANES, W))


def _ffn_body(ea_ref, eb_ref, halves_ref, xs_ref, wga_ref, wua_ref, wda_ref,
              wgb_ref, wub_ref, wdb_ref, ys_ref, wg_scr, wu_scr, wd_scr):
    i = pl.program_id(0)
    tg, D = ys_ref.shape
    nh = halves_ref[i]
    prev = jnp.maximum(i - 1, 0)
    new_class = (i == 0) | (ea_ref[i] != ea_ref[prev]) | (eb_ref[i] != eb_ref[prev])

    @pl.when((nh > 0) & new_class)
    def _():
        for slot, (wg, wu, wd) in enumerate(((wga_ref, wua_ref, wda_ref), (wgb_ref, wub_ref, wdb_ref))):
            wg_scr[slot] = wg[0, 0].astype(BF16)
            wu_scr[slot] = wu[0, 0].astype(BF16)
            wd_scr[slot] = wd[0, 0].astype(BF16)

    def experts(rows):
        x = xs_ref[rows, :D].astype(BF16)
        w_a = xs_ref[rows, D:D + 1]
        w_b = xs_ref[rows, D + 1:D + 2]

        def expert(slot):
            gt = _dot(x, wg_scr[slot])
            hdn = (gt * jax.nn.sigmoid(gt)) * _dot(x, wu_scr[slot])
            return _dot(hdn.astype(BF16), wd_scr[slot])

        return w_a * expert(0) + w_b * expert(1)

    @pl.when(nh == 2)
    def _():
        ys_ref[...] = experts(slice(0, tg))

    @pl.when(nh == 1)
    def _():
        ys_ref[:FFN_HALF, :] = experts(slice(0, FFN_HALF))
        ys_ref[FFN_HALF:, :] = jnp.zeros((tg - FFN_HALF, D), F32)

    @pl.when(nh == 0)
    def _():
        ys_ref[...] = jnp.zeros_like(ys_ref)


def _ffn(xs, ea, eb, halves, layer, wg, wu, wd):
    P, W = xs.shape
    D = W - EXTRA
    ff = wg.shape[3]
    tg = FFN_TILE
    wspec_a = lambda shape: pl.BlockSpec(shape, lambda i, ea, eb, nh: (layer, ea[i], 0, 0))
    wspec_b = lambda shape: pl.BlockSpec(shape, lambda i, ea, eb, nh: (layer, eb[i], 0, 0))
    grid_spec = pltpu.PrefetchScalarGridSpec(
        num_scalar_prefetch=3,
        grid=(P // tg,),
        in_specs=[
            pl.BlockSpec((tg, W), lambda i, ea, eb, nh: (i, 0)),
            wspec_a((1, 1, D, ff)), wspec_a((1, 1, D, ff)), wspec_a((1, 1, ff, D)),
            wspec_b((1, 1, D, ff)), wspec_b((1, 1, D, ff)), wspec_b((1, 1, ff, D)),
        ],
        out_specs=pl.BlockSpec((tg, D), lambda i, ea, eb, nh: (i, 0)),
        scratch_shapes=[pltpu.VMEM((2, D, ff), BF16), pltpu.VMEM((2, D, ff), BF16),
                        pltpu.VMEM((2, ff, D), BF16)],
    )
    return pl.pallas_call(
        _ffn_body,
        out_shape=jax.ShapeDtypeStruct((P, D), F32),
        grid_spec=grid_spec,
        compiler_params=_params(("arbitrary",)),
        name="moe_ffn",
    )(ea, eb, halves, xs, wg, wu, wd, wg, wu, wd)


def _combine_body(dest_ref, dnext_ref, x_ref, gfin_ref, ys_ref, o_ref, ybuf, sems, *, final_norm):
    i, n = pl.program_id(0), pl.num_programs(0)
    tc, D = x_ref.shape
    groups = tc // SUBLANES

    def gather(idx_ref, slot):
        def row_copy(j, u, hi, lo):
            return pltpu.make_async_copy(ys_ref.at[hi, pl.ds(lo, 1), :],
                                         ybuf.at[slot, j, pl.ds(u, 1), :], sems.at[slot])
        _start_rows(tc, idx_ref, row_copy)

    @pl.when(i == 0)
    def _():
        gather(dest_ref, 0)

    for slot in range(2):
        @pl.when((i + 1 < n) & ((i + 1) % 2 == slot))
        def _():
            gather(dnext_ref, slot)

    for slot in range(2):
        @pl.when(i % 2 == slot)
        def _():
            pltpu.make_async_copy(ys_ref.at[pl.ds(0, groups)], ybuf.at[slot], sems.at[slot]).wait()
            y = x_ref[...] + ybuf[slot].reshape(tc, D)
            if final_norm:
                y = _rms(y, gfin_ref[...])
            o_ref[...] = y


def _combine(x, dest3, ys, gfin, final_norm):
    T, D = x.shape
    nb, _, tc = dest3.shape
    body = functools.partial(_combine_body, final_norm=final_norm)
    return pl.pallas_call(
        body,
        out_shape=jax.ShapeDtypeStruct((T, D), F32),
        grid=(nb,),
        in_specs=[
            pl.BlockSpec((1, 1, tc), lambda i: (i, 0, 0), memory_space=pltpu.SMEM),
            pl.BlockSpec((1, 1, tc), lambda i: (jnp.minimum(i + 1, nb - 1), 0, 0),
                         memory_space=pltpu.SMEM),
            pl.BlockSpec((tc, D), lambda i: (i, 0)),
            _const_spec((1, D)),
            pl.BlockSpec(memory_space=pl.ANY),
        ],
        out_specs=pl.BlockSpec((tc, D), lambda i: (i, 0)),
        scratch_shapes=[pltpu.VMEM((2, tc // SUBLANES, SUBLANES, D), F32),
                        pltpu.SemaphoreType.DMA((2,))],
        compiler_params=_params(("arbitrary",)),
        name="moe_combine",
    )(dest3, dest3, x, gfin, ys.reshape(-1, SUBLANES, D))


def _moe_tables(metas, counts, P):
    tg = FFN_TILE
    cnt = counts[0, :N_CLASSES].astype(jnp.int32)
    padded = ((cnt + tg - 1) // tg) * tg
    ends = jnp.cumsum(padded)
    offs = ends - padded
    dests = [jnp.take(offs, m[:, 0].astype(jnp.int32)) + m[:, 1].astype(jnp.int32) for m in metas]
    tile_start = jnp.arange(P // tg, dtype=jnp.int32) * tg
    tile_cls = jnp.minimum(jnp.sum((tile_start[:, None] >= ends[None, :]).astype(jnp.int32), axis=1),
                           N_CLASSES - 1)
    rows = jnp.clip(jnp.take(cnt, tile_cls) - (tile_start - jnp.take(offs, tile_cls)), 0, tg)
    halves = jnp.where(tile_start < ends[-1], (rows + FFN_HALF - 1) // FFN_HALF, 0).astype(jnp.int32)
    grp = tile_cls // N_PAIRS
    pair = tile_cls % N_PAIRS
    ea = grp * MOE_PER_GROUP + jnp.take(jnp.asarray(_PAIR_LO), pair)
    eb = grp * MOE_PER_GROUP + jnp.take(jnp.asarray(_PAIR_HI), pair)
    zinfo = jnp.concatenate([offs + cnt, ends, ends[-1:] // tg]).astype(jnp.int32)
    return dests, ea, eb, halves, zinfo


def _moe(xs_in, g, w_group, b_group, w_router, b_router, layer, wg, wu, wd, gfin, final_norm):
    assert len(xs_in) == 2
    D = xs_in[0].shape[1]
    T = sum(x.shape[0] for x in xs_in)
    wcat = jnp.zeros((D, LANES), F32)
    wcat = wcat.at[:, :MOE_GROUPS].set(w_group).at[:, MOE_GROUPS:MOE_GROUPS + MOE_EXPERTS].set(w_router)
    bias = jnp.zeros((1, LANES), F32)
    bias = bias.at[0, :MOE_GROUPS].set(b_group).at[0, MOE_GROUPS:MOE_GROUPS + MOE_EXPERTS].set(b_router)
    wc = wcat.astype(BF16)

    counts = jnp.zeros((1, LANES), F32)
    texts, metas = [], []
    for x in xs_in:
        t_ext, meta, counts = _router(x, g, wc, bias, counts)
        texts.append(t_ext)
        metas.append(meta)

    tg = FFN_TILE
    P = ((T + tg - 1) // tg + N_CLASSES) * tg
    dests, ea, eb, halves, zinfo = _moe_tables(metas, counts, P)
    tmove = _pick_tile(math.gcd(*(x.shape[0] for x in xs_in)), MOVE_TILE)
    dest3s = [dest.reshape(-1, 1, tmove) for dest in dests]
    xs = _dispatch(texts[0], texts[1], jnp.concatenate(dest3s, axis=0), zinfo, P)
    ys = _ffn(xs.reshape(P, D + EXTRA), ea, eb, halves, layer, wg, wu, wd)
    return [_combine(x, dest3, ys, gfin, final_norm) for x, dest3 in zip(xs_in, dest3s)]


def _rope_tables(pos, dk):
    half = dk // 2
    inv = 1.0 / (ROPE_BASE ** jnp.linspace(0.0, 1.0, half, dtype=F32))
    ang = pos.astype(F32)[:, None] * inv[None, :]
    return jnp.cos(ang), jnp.sin(ang)


def kernel(x_prompt, x_sample, state_ret, state_conv, state_lru, norm_mix_g, norm_ffn_g, norm_final_g, ret_w_in, ret_w_out, ret_norm_g, lru_w_in, lru_conv_w, lru_conv_b, lru_w_a, lru_b_a, lru_w_x, lru_b_x, lru_lambda, lru_w_out, moe_w_group, moe_b_group, moe_w_router, moe_b_router, moe_w_gate, moe_w_up, moe_w_down):
    B, S, D = x_prompt.shape
    Bs, Ls, _ = x_sample.shape
    Tp, Ts = B * S, Bs * Ls
    assert norm_mix_g.shape[0] == 2
    dk = D // RET_HEADS
    row = lambda v: v.reshape(1, -1)
    gfin = row(norm_final_g)

    cos_p, sin_p = _rope_tables(jnp.arange(S), dk)
    cos_s, sin_s = _rope_tables(PAST_LEN + jnp.arange(Ls), dk)
    ret_args = (row(norm_mix_g[0]), ret_w_in[0].astype(BF16), ret_w_out[0].astype(BF16),
                row(ret_norm_g[0]))
    xp, ret_p = _ret_prompt(x_prompt.reshape(Tp, D), B, S, *ret_args, cos_p, sin_p)
    xs, ret_s = _ret_sample(x_sample.reshape(Ts, D), Bs, Ls, *ret_args, cos_s, sin_s, state_ret[0])
    xp, xs = _moe([xp, xs], row(norm_ffn_g[0]), moe_w_group[0], moe_b_group[0], moe_w_router[0],
                  moe_b_router[0], 0, moe_w_gate, moe_w_up, moe_w_down, gfin, False)

    lru_args = (row(norm_mix_g[1]), lru_w_in[0].astype(BF16), lru_conv_w[0], row(lru_conv_b[0]),
                lru_w_a[0].astype(BF16), row(lru_b_a[0]), lru_w_x[0].astype(BF16), row(lru_b_x[0]),
                row(lru_lambda[0]), lru_w_out[0].astype(BF16))
    cs8 = jnp.pad(state_conv[0], ((0, 0), (8 - (CONV_W - 1), 0), (0, 0)))
    h0_rows = jnp.pad(state_lru[0].reshape(Bs, 1, D), ((0, 0), (0, Ls - 1), (0, 0)))
    xp, conv_p, lru_p = _lru_prompt(xp, B, S, *lru_args)
    xs, conv_s, lru_s = _lru_sample(xs, Bs, Ls, *lru_args, cs8, h0_rows)
    y_p, y_s = _moe([xp, xs], row(norm_ffn_g[1]), moe_w_group[1], moe_b_group[1], moe_w_router[1],
                    moe_b_router[1], 1, moe_w_gate, moe_w_up, moe_w_down, gfin, True)

    return (y_p.reshape(B, S, D), y_s.reshape(Bs, Ls, D),
            ret_p[None], conv_p[None, :, 8 - (CONV_W - 1):], lru_p[None, :, 7],
            ret_s[None], conv_s[None, :, 8 - (CONV_W - 1):], lru_s[None, :, 7])
```

```python
import functools
import math

import numpy as np
import jax
import jax.numpy as jnp
from jax import lax
from jax.experimental import pallas as pl
from jax.experimental.pallas import tpu as pltpu

F32 = jnp.float32
BF16 = jnp.bfloat16

RET_HEADS = 4
RET_CHUNK = 64
ROPE_BASE = 10000.0
CONV_W = 4
LRU_BLOCKS = 4
LRU_C = 8.0
MOE_GROUPS = 4
MOE_PER_GROUP = 4
MOE_EXPERTS = MOE_GROUPS * MOE_PER_GROUP
NORM_EPS = 1e-6
PAST_LEN = 2048

LANES = 128
SUBLANES = 8
VMEM_LIMIT_BYTES = 56 * 1024 * 1024

RET_TILE = 512
RET_SCORE_BLOCK = 256
LRU_TILE = 512
ROUTE_TILE = 1024
MOVE_TILE = 512
FFN_TILE = 512
FFN_HALF = 256
N_PAIRS = 6
N_CLASSES = MOE_GROUPS * N_PAIRS
EXTRA = LANES
NEG_BIG = -1e30

_PAIR_LO = np.array([0, 0, 0, 1, 1, 2], np.int32)
_PAIR_HI = np.array([1, 2, 3, 2, 3, 3], np.int32)


def _log_gammas():
    return [math.log1p(-(2.0 ** (-5.0 - h))) for h in range(RET_HEADS)]


def _rms(x, g):
    return x * lax.rsqrt(jnp.mean(x * x, axis=-1, keepdims=True) + NORM_EPS) * g


def _dot(a, b):
    return jnp.dot(a, b, preferred_element_type=F32)


def _dot_nt(a, b):
    return lax.dot_general(a, b, (((1,), (1,)), ((), ())), preferred_element_type=F32)


def _dot_tn(a, b):
    return lax.dot_general(a, b, (((0,), (0,)), ((), ())), preferred_element_type=F32)


def _const_spec(shape):
    nd = len(shape)
    return pl.BlockSpec(shape, lambda *_: (0,) * nd, pipeline_mode=pl.Buffered(1))


def _params(sem):
    return pltpu.CompilerParams(dimension_semantics=sem, vmem_limit_bytes=VMEM_LIMIT_BYTES)


def _pick_tile(n, pref):
    t = pref
    while n % t:
        t //= 2
    assert t >= SUBLANES, (n, pref)
    return t


def _decay_matrix(L, chunk, lg):
    ii = lax.broadcasted_iota(jnp.int32, (L, L), 0)
    jj = lax.broadcasted_iota(jnp.int32, (L, L), 1)
    diff = (ii - jj).astype(F32)
    same_chunk = (ii // chunk) == (jj // chunk)
    return jnp.where(same_chunk & (ii >= jj), jnp.exp(jnp.maximum(diff, 0.0) * lg), 0.0)


def _rotary(t, cos, sin):
    half = t.shape[-1] // 2
    t1, t2 = t[:, :half], t[:, half:]
    return jnp.concatenate([t1 * cos - t2 * sin, t1 * sin + t2 * cos], axis=-1)


def _ret_proj(hn, win_ref, hd, dk, dv):
    qk, vv = RET_HEADS * dk, RET_HEADS * dv
    q = _dot(hn, win_ref[:, hd * dk:(hd + 1) * dk])
    k = _dot(hn, win_ref[:, qk + hd * dk:qk + (hd + 1) * dk])
    v = _dot(hn, win_ref[:, 2 * qk + hd * dv:2 * qk + (hd + 1) * dv])
    gt = _dot(hn, win_ref[:, 2 * qk + vv + hd * dv:2 * qk + vv + (hd + 1) * dv])
    return q, k, v, gt


def _ret_head(q, k, v, gt, cos, sin, decay, s0, ng, chunk, lg, kv_scr=None):
    L, dk = q.shape
    qr = _rotary(q, cos, sin)
    kr = _rotary(k, cos, sin) * (dk ** -0.5)
    idx = (lax.broadcasted_iota(jnp.int32, (L, 1), 0) % chunk).astype(F32)
    qd = (qr * jnp.exp((idx + 1.0) * lg)).astype(BF16)
    kd = (kr * jnp.exp((chunk - 1.0 - idx) * lg)).astype(BF16)
    qb, kb, vb = qr.astype(BF16), kr.astype(BF16), v.astype(BF16)
    core = decay.shape[0]
    chunks = [slice(c * chunk, (c + 1) * chunk) for c in range(L // chunk)]
    if kv_scr is not None:
        for c, rows in enumerate(chunks):
            kv_scr[c] = _dot_tn(kd[rows], vb[rows])
    s, parts = s0, []
    for r0 in range(0, L, core):
        blk = slice(r0, r0 + core)
        scores = _dot_nt(qb[blk], kb[blk]) * decay
        intra = _dot(scores.astype(BF16), vb[blk])
        for c in range(r0 // chunk, (r0 + core) // chunk):
            rows = chunks[c]
            parts.append(intra[rows.start - r0:rows.stop - r0] + _dot(qd[rows], s.astype(BF16)))
            kv = kv_scr[c] if kv_scr is not None else _dot_tn(kd[rows], vb[rows])
            s = math.exp(chunk * lg) * s + kv
    o = parts[0] if len(parts) == 1 else jnp.concatenate(parts, axis=0)
    o = o * lax.rsqrt(jnp.mean(o * o, axis=-1, keepdims=True) + NORM_EPS)
    og = (gt * jax.nn.sigmoid(gt)) * (o * ng)
    return og, s


def _ret_prompt_body(x_ref, g_ref, win_ref, wout_ref, ng_ref, cos_ref, sin_ref,
                     y_ref, s_ref, decay_scr, og_scr, kv_scr, *, L, chunk, dk, dv):
    b, c = pl.program_id(0), pl.program_id(1)
    lgs = _log_gammas()

    @pl.when((b == 0) & (c == 0))
    def _():
        for hd in range(RET_HEADS):
            decay_scr[hd] = _decay_matrix(decay_scr.shape[1], chunk, lgs[hd])

    @pl.when(c == 0)
    def _():
        s_ref[...] = jnp.zeros_like(s_ref)

    x = x_ref[...]
    hn = _rms(x, g_ref[...]).astype(BF16)
    cos, sin = cos_ref[...], sin_ref[...]
    for hd in range(RET_HEADS):
        og, s_new = _ret_head(*_ret_proj(hn, win_ref, hd, dk, dv), cos, sin, decay_scr[hd],
                              s_ref[0, hd], ng_ref[:, hd * dv:(hd + 1) * dv], chunk, lgs[hd],
                              kv_scr)
        s_ref[0, hd] = s_new
        og_scr[:, hd * dv:(hd + 1) * dv] = og.astype(BF16)
    y_ref[...] = x + _dot(og_scr[...], wout_ref[...])


def _ret_prompt(x_p, B, S, g, win, wout, ng, cos, sin):
    Tp, D = x_p.shape
    dk, dv = D // RET_HEADS, 2 * D // RET_HEADS
    chunk = min(S, RET_CHUNK)
    L = _pick_tile(S, RET_TILE)
    core = min(L, RET_SCORE_BLOCK)
    assert L % core == 0 and core % chunk == 0
    nc = S // L
    body = functools.partial(_ret_prompt_body, L=L, chunk=chunk, dk=dk, dv=dv)
    return pl.pallas_call(
        body,
        out_shape=(jax.ShapeDtypeStruct((Tp, D), F32),
                   jax.ShapeDtypeStruct((B, RET_HEADS, dk, dv), F32)),
        grid=(B, nc),
        in_specs=[
            pl.BlockSpec((L, D), lambda b, c: (b * nc + c, 0)),
            _const_spec((1, D)),
            _const_spec(win.shape),
            _const_spec(wout.shape),
            _const_spec((1, RET_HEADS * dv)),
            pl.BlockSpec((L, dk // 2), lambda b, c: (c, 0)),
            pl.BlockSpec((L, dk // 2), lambda b, c: (c, 0)),
        ],
        out_specs=(
            pl.BlockSpec((L, D), lambda b, c: (b * nc + c, 0)),
            pl.BlockSpec((1, RET_HEADS, dk, dv), lambda b, c: (b, 0, 0, 0)),
        ),
        scratch_shapes=[pltpu.VMEM((RET_HEADS, core, core), F32),
                        pltpu.VMEM((L, RET_HEADS * dv), BF16),
                        pltpu.VMEM((L // chunk, dk, dv), F32)],
        compiler_params=_params(("arbitrary", "arbitrary")),
        name="ret_prompt",
    )(x_p, g, win, wout, ng, cos, sin)


def _ret_sample_body(x_ref, g_ref, win_ref, wout_ref, ng_ref, cos_ref, sin_ref, s0_ref,
                     y_ref, s_ref, q_scr, k_scr, v_scr, gt_scr, og_scr, *, L, dk, dv):
    b, nb = pl.program_id(0), pl.num_programs(0)
    lgs = _log_gammas()

    @pl.when(b == 0)
    def _():
        hn = _rms(x_ref[...], g_ref[...]).astype(BF16)
        for hd in range(RET_HEADS):
            q, k, v, gt = _ret_proj(hn, win_ref, hd, dk, dv)
            q_scr[hd], k_scr[hd], v_scr[hd], gt_scr[hd] = q, k, v, gt

    rows = pl.ds(pl.multiple_of(b * L, L), L)
    cos, sin = cos_ref[...], sin_ref[...]
    for hd in range(RET_HEADS):
        og, s_new = _ret_head(q_scr[hd, rows, :], k_scr[hd, rows, :], v_scr[hd, rows, :],
                              gt_scr[hd, rows, :], cos, sin, _decay_matrix(L, L, lgs[hd]),
                              s0_ref[0, hd], ng_ref[:, hd * dv:(hd + 1) * dv], L, lgs[hd])
        s_ref[0, hd] = s_new
        og_scr[rows, hd * dv:(hd + 1) * dv] = og.astype(BF16)

    @pl.when(b == nb - 1)
    def _():
        y_ref[...] = x_ref[...] + _dot(og_scr[...], wout_ref[...])


def _ret_sample(x_s, Bs, Ls, g, win, wout, ng, cos, sin, s0):
    Ts, D = x_s.shape
    dk, dv = D // RET_HEADS, 2 * D // RET_HEADS
    assert Ls <= RET_CHUNK
    body = functools.partial(_ret_sample_body, L=Ls, dk=dk, dv=dv)
    return pl.pallas_call(
        body,
        out_shape=(jax.ShapeDtypeStruct((Ts, D), F32),
                   jax.ShapeDtypeStruct((Bs, RET_HEADS, dk, dv), F32)),
        grid=(Bs,),
        in_specs=[
            _const_spec((Ts, D)),
            _const_spec((1, D)),
            _const_spec(win.shape),
            _const_spec(wout.shape),
            _const_spec((1, RET_HEADS * dv)),
            _const_spec((Ls, dk // 2)),
            _const_spec((Ls, dk // 2)),
            pl.BlockSpec((1, RET_HEADS, dk, dv), lambda b: (b, 0, 0, 0)),
        ],
        out_specs=(
            pl.BlockSpec((Ts, D), lambda b: (0, 0)),
            pl.BlockSpec((1, RET_HEADS, dk, dv), lambda b: (b, 0, 0, 0)),
        ),
        scratch_shapes=[pltpu.VMEM((RET_HEADS, Ts, dk), F32), pltpu.VMEM((RET_HEADS, Ts, dk), F32),
                        pltpu.VMEM((RET_HEADS, Ts, dv), F32), pltpu.VMEM((RET_HEADS, Ts, dv), F32),
                        pltpu.VMEM((Ts, RET_HEADS * dv), BF16)],
        compiler_params=_params(("arbitrary",)),
        name="ret_sample",
    )(x_s, g, win, wout, ng, cos, sin, s0)


def _gelu_tanh(x):
    return 0.5 * x * (1.0 + jnp.tanh(math.sqrt(2.0 / math.pi) * (x + 0.044715 * (x * x * x))))


def _softplus(z):
    return jnp.maximum(z, 0.0) + jnp.log1p(jnp.exp(-jnp.abs(z)))


def _lru_gates(conv, wa_ref, wx_ref, ba, bx, lam):
    D = conv.shape[1]
    bw = D // LRU_BLOCKS
    ra, ix = [], []
    for n in range(LRU_BLOCKS):
        cb = conv[:, n * bw:(n + 1) * bw].astype(BF16)
        ra.append(_dot(cb, wa_ref[n]))
        ix.append(_dot(cb, wx_ref[n]))
    r = jax.nn.sigmoid(jnp.concatenate(ra, axis=-1) + ba)
    i = jax.nn.sigmoid(jnp.concatenate(ix, axis=-1) + bx)
    log_a = r * ((-LRU_C) * _softplus(-lam))
    a = jnp.exp(log_a)
    mult = jnp.sqrt(jnp.tanh(-log_a) * (1.0 + a * a))
    return a, mult, i * conv


def _seg_scan(a, b, seg):
    R = a.shape[0]
    row = lax.broadcasted_iota(jnp.int32, (R, 1), 0) % seg
    k = 1
    while k < seg:
        keep = row >= k
        a_sh = jnp.where(keep, pltpu.roll(a, k, axis=0), 1.0)
        b_sh = jnp.where(keep, pltpu.roll(b, k, axis=0), 0.0)
        b = a * b_sh + b
        a = a * a_sh
        k *= 2
    return b


def _carry_scan(a, b, h_prev):
    L, D = a.shape
    G = L // SUBLANES
    a3, b3 = a.reshape(G, SUBLANES, D), b.reshape(G, SUBLANES, D)
    sub = lax.broadcasted_iota(jnp.int32, (SUBLANES, D), 0)[None]
    k = 1
    while k < SUBLANES:
        keep = sub >= k
        a_sh = jnp.where(keep, pltpu.roll(a3, k, axis=1), 1.0)
        b_sh = jnp.where(keep, pltpu.roll(b3, k, axis=1), 0.0)
        b3 = a3 * b_sh + b3
        a3 = a3 * a_sh
        k *= 2
    carry, outs = h_prev, []
    for j in range(G):
        r = a3[j] * carry + b3[j]
        outs.append(r)
        carry = r[SUBLANES - 1:SUBLANES, :]
    return jnp.concatenate(outs, axis=0)


def _lru_prompt_body(x_ref, g_ref, win_ref, cw_ref, cb_ref, wa_ref, ba_ref, wx_ref, bx_ref,
                     lam_ref, wout_ref, y_ref, conv_ref, h_ref, xc_scr, hprev_scr, *, L):
    c = pl.program_id(1)
    D = x_ref.shape[1]

    @pl.when(c == 0)
    def _():
        xc_scr[...] = jnp.zeros_like(xc_scr)
        hprev_scr[...] = jnp.zeros_like(hprev_scr)

    x = x_ref[...]
    hn = _rms(x, g_ref[...]).astype(BF16)
    gate = _gelu_tanh(_dot(hn, win_ref[:, :D]))
    x_br = _dot(hn, win_ref[:, D:])
    prev8 = xc_scr[...]
    sub8 = lax.broadcasted_iota(jnp.int32, (SUBLANES, 1), 0)

    def shifted(k):
        rolled = pltpu.roll(x_br, k, axis=0)
        first = jnp.where(sub8 < k, pltpu.roll(prev8, k, axis=0), rolled[0:SUBLANES, :])
        return jnp.concatenate([first, rolled[SUBLANES:, :]], axis=0)

    conv = cb_ref[...] + shifted(CONV_W - 1) * cw_ref[0:1, :]
    for j in range(1, CONV_W - 1):
        conv = conv + shifted(CONV_W - 1 - j) * cw_ref[j:j + 1, :]
    conv = conv + x_br * cw_ref[CONV_W - 1:CONV_W, :]
    a, mult, gi = _lru_gates(conv, wa_ref, wx_ref, ba_ref[...], bx_ref[...], lam_ref[...])
    first = jnp.where((sub8 == 0) & (c == 0), 1.0, mult[0:SUBLANES, :])
    mult = jnp.concatenate([first, mult[SUBLANES:, :]], axis=0)
    hs = _carry_scan(a, mult * gi, hprev_scr[7:8, :])
    y_ref[...] = x + _dot((hs * gate).astype(BF16), wout_ref[...])
    hprev_scr[...] = hs[L - 8:L, :]
    h_ref[0] = hs[L - 8:L, :]
    conv_ref[0] = x_br[L - 8:L, :]
    xc_scr[...] = x_br[L - 8:L, :]


def _lru_prompt(x_p, B, S, g, win, cw, cb, wa, ba, wx, bx, lam, wout):
    Tp, D = x_p.shape
    L = _pick_tile(S, LRU_TILE)
    nc = S // L
    body = functools.partial(_lru_prompt_body, L=L)
    return pl.pallas_call(
        body,
        out_shape=(jax.ShapeDtypeStruct((Tp, D), F32),
                   jax.ShapeDtypeStruct((B, 8, D), F32),
                   jax.ShapeDtypeStruct((B, 8, D), F32)),
        grid=(B, nc),
        in_specs=[
            pl.BlockSpec((L, D), lambda b, c: (b * nc + c, 0)),
            _const_spec((1, D)), _const_spec(win.shape), _const_spec(cw.shape), _const_spec((1, D)),
            _const_spec(wa.shape), _const_spec((1, D)), _const_spec(wx.shape), _const_spec((1, D)),
            _const_spec((1, D)), _const_spec(wout.shape),
        ],
        out_specs=(
            pl.BlockSpec((L, D), lambda b, c: (b * nc + c, 0)),
            pl.BlockSpec((1, 8, D), lambda b, c: (b, 0, 0)),
            pl.BlockSpec((1, 8, D), lambda b, c: (b, 0, 0)),
        ),
        scratch_shapes=[pltpu.VMEM((8, D), F32), pltpu.VMEM((8, D), F32)],
        compiler_params=_params(("arbitrary", "arbitrary")),
        name="lru_prompt",
    )(x_p, g, win, cw, cb, wa, ba, wx, bx, lam, wout)


def _lru_sample_body(x_ref, g_ref, win_ref, cw_ref, cb_ref, wa_ref, ba_ref, wx_ref, bx_ref,
                     lam_ref, wout_ref, cs_ref, h0_ref, y_ref, conv_ref, h_ref,
                     xc_scr, *, Bs, Ls):
    Ts, D = x_ref.shape
    x = x_ref[...]
    hn = _rms(x, g_ref[...]).astype(BF16)
    gate = _gelu_tanh(_dot(hn, win_ref[:, :D]))
    x_br = _dot(hn, win_ref[:, D:])
    xc_scr[:, 0:8, :] = cs_ref[...]
    xc_scr[:, 8:8 + Ls, :] = x_br.reshape(Bs, Ls, D)
    conv = cb_ref[...] + xc_scr[:, 5:5 + Ls, :] * cw_ref[0:1, :]
    for j in range(1, CONV_W):
        conv = conv + xc_scr[:, 5 + j:5 + j + Ls, :] * cw_ref[j:j + 1, :]
    conv = conv.reshape(Ts, D)
    a, mult, gi = _lru_gates(conv, wa_ref, wx_ref, ba_ref[...], bx_ref[...], lam_ref[...])
    bvec = mult * gi + a * h0_ref[...].reshape(Ts, D)
    hs = _seg_scan(a, bvec, Ls)
    y_ref[...] = x + _dot((hs * gate).astype(BF16), wout_ref[...])
    h_ref[...] = hs.reshape(Bs, Ls, D)[:, Ls - 8:Ls, :]
    conv_ref[...] = xc_scr[:, Ls:Ls + 8, :]


def _lru_sample(x_s, Bs, Ls, g, win, cw, cb, wa, ba, wx, bx, lam, wout, cs, h0):
    Ts, D = x_s.shape
    assert Ls % 8 == 0
    body = functools.partial(_lru_sample_body, Bs=Bs, Ls=Ls)
    return pl.pallas_call(
        body,
        out_shape=(jax.ShapeDtypeStruct((Ts, D), F32),
                   jax.ShapeDtypeStruct((Bs, 8, D), F32),
                   jax.ShapeDtypeStruct((Bs, 8, D), F32)),
        grid=(1,),
        in_specs=[
            _const_spec((Ts, D)),
            _const_spec((1, D)), _const_spec(win.shape), _const_spec(cw.shape), _const_spec((1, D)),
            _const_spec(wa.shape), _const_spec((1, D)), _const_spec(wx.shape), _const_spec((1, D)),
            _const_spec((1, D)), _const_spec(wout.shape),
            _const_spec((Bs, 8, D)), _const_spec((Bs, Ls, D)),
        ],
        out_specs=(
            pl.BlockSpec((Ts, D), lambda i: (0, 0)),
            pl.BlockSpec((Bs, 8, D), lambda i: (0, 0, 0)),
            pl.BlockSpec((Bs, 8, D), lambda i: (0, 0, 0)),
        ),
        scratch_shapes=[pltpu.VMEM((Bs, Ls + 8, D), F32)],
        compiler_params=_params(("arbitrary",)),
        name="lru_sample",
    )(x_s, g, win, cw, cb, wa, ba, wx, bx, lam, wout, cs, h0)


def _router_body(x_ref, g_ref, wc_ref, bias_ref, cin_ref, text_ref, meta_ref, cnt_ref, run_scr,
                 tri_scr):
    i = pl.program_id(0)
    tm, D = x_ref.shape

    @pl.when(i == 0)
    def _():
        run_scr[...] = cin_ref[...]
        ri = lax.broadcasted_iota(jnp.int32, (tm, tm), 0)
        ci = lax.broadcasted_iota(jnp.int32, (tm, tm), 1)
        tri_scr[...] = (ci < ri).astype(BF16)

    t = _rms(x_ref[...], g_ref[...])
    logits = _dot(t.astype(BF16), wc_ref[...]) + bias_ref[...]
    lane_i = lax.broadcasted_iota(jnp.int32, (tm, LANES), 1)
    lane = lane_i.astype(F32)

    def first_argmax(v):
        m = jnp.max(v, axis=-1, keepdims=True)
        return m, jnp.min(jnp.where(v == m, lane, 1e6), axis=-1, keepdims=True)

    gmask = lane_i < MOE_GROUPS
    gl = jnp.where(gmask, logits, NEG_BIG)
    mg, g_idx = first_argmax(gl)
    p_g = 1.0 / jnp.sum(jnp.where(gmask, jnp.exp(gl - mg), 0.0), axis=-1, keepdims=True)
    e_lane = lane_i - MOE_GROUPS
    lane_grp = (e_lane >> 2).astype(F32)
    emask = (e_lane >= 0) & (e_lane < MOE_EXPERTS) & (lane_grp == g_idx)
    el = jnp.where(emask, logits, NEG_BIG)
    m1, i1 = first_argmax(el)
    m2, i2 = first_argmax(jnp.where(lane == i1, NEG_BIG, el))
    e21 = jnp.exp(m2 - m1)
    w1 = p_g / (1.0 + e21)
    w2 = p_g * e21 / (1.0 + e21)
    first_lower = i1 < i2
    j_lo = jnp.minimum(i1, i2) - (MOE_GROUPS + MOE_PER_GROUP * g_idx)
    j_hi = jnp.maximum(i1, i2) - (MOE_GROUPS + MOE_PER_GROUP * g_idx)
    pair = 0.5 * (j_lo * (7.0 - j_lo)) + (j_hi - j_lo - 1.0)
    cls = g_idx * N_PAIRS + pair
    w_lo = jnp.where(first_lower, w1, w2)
    w_hi = jnp.where(first_lower, w2, w1)

    onehot = (lane == cls).astype(F32)
    prefix = _dot(tri_scr[...], onehot.astype(BF16))
    rank = jnp.sum(onehot * (prefix + run_scr[...]), axis=-1, keepdims=True)
    run_scr[...] = run_scr[...] + jnp.sum(onehot, axis=0, keepdims=True)

    text_ref[:, :D] = t
    text_ref[:, D:] = jnp.where(lane_i == 0, w_lo, jnp.where(lane_i == 1, w_hi, 0.0))
    ones8 = jnp.ones((SUBLANES, LANES), BF16)
    ids8 = lax.broadcasted_iota(jnp.int32, (SUBLANES, LANES), 1).astype(F32).astype(BF16)
    d2 = jnp.floor(rank * (1.0 / 65536.0))
    rem = rank - 65536.0 * d2
    d1 = jnp.floor(rem * (1.0 / 256.0))
    d0 = rem - 256.0 * d1

    def as_row(col):
        return _dot_nt(ones8, jnp.where(lane_i == 0, col, 0.0).astype(BF16))

    cls_row = _dot_nt(ids8, onehot.astype(BF16))
    rank_row = as_row(d0) + 256.0 * as_row(d1) + 65536.0 * as_row(d2)
    sub = lax.broadcasted_iota(jnp.int32, (SUBLANES, tm), 0)
    meta_ref[...] = jnp.where(sub == 0, cls_row, jnp.where(sub == 1, rank_row, 0.0))
    cnt_ref[...] = run_scr[...]


def _router(x, g, wc, bias, cnt_in):
    T, D = x.shape
    tm = _pick_tile(T, ROUTE_TILE)
    return pl.pallas_call(
        _router_body,
        out_shape=(jax.ShapeDtypeStruct((T, D + EXTRA), F32),
                   jax.ShapeDtypeStruct((T // tm * SUBLANES, tm), F32),
                   jax.ShapeDtypeStruct((1, LANES), F32)),
        grid=(T // tm,),
        in_specs=[
            pl.BlockSpec((tm, D), lambda i: (i, 0)),
            _const_spec((1, D)), _const_spec(wc.shape), _const_spec((1, LANES)),
            _const_spec((1, LANES)),
        ],
        out_specs=(
            pl.BlockSpec((tm, D + EXTRA), lambda i: (i, 0)),
            pl.BlockSpec((SUBLANES, tm), lambda i: (i, 0)),
            pl.BlockSpec((1, LANES), lambda i: (0, 0)),
        ),
        scratch_shapes=[pltpu.VMEM((1, LANES), F32), pltpu.VMEM((tm, tm), BF16)],
        compiler_params=_params(("arbitrary",)),
        name="moe_router",
    )(x, g, wc, bias, cnt_in)


def _start_rows(n, idx_ref, row_copy):
    assert n % SUBLANES == 0

    def body(j, carry):
        for u in range(SUBLANES):
            pos = idx_ref[0, 0, j * SUBLANES + u]
            row_copy(j, u, pos >> 3, pos & (SUBLANES - 1)).start(priority=u % 2)
        return carry

    lax.fori_loop(0, n // SUBLANES, body, 0)


def _dispatch_body(zinfo_ref, dest_ref, tp_ref, ts_ref, xs_ref, zbuf, sem, zsem, *, nbp, steps):
    i = pl.program_id(0)
    zgroups = zbuf.shape[0]

    def zero_copies():
        out = []
        for c in range(N_CLASSES):
            r0, r1 = zinfo_ref[c], zinfo_ref[N_CLASSES + c]
            g0, g1 = r0 >> 3, r1 >> 3
            for k in range(SUBLANES - 1):
                out.append((k < r1 - g1 * SUBLANES, pltpu.make_async_copy(
                    zbuf.at[0, pl.ds(0, 1), :], xs_ref.at[g1, pl.ds(k, 1), :], zsem)))
            n = g1 - g0
            b = zgroups // 2
            while b >= 1:
                out.append(((n & b) != 0, pltpu.make_async_copy(
                    zbuf.at[pl.ds(0, b)], xs_ref.at[pl.ds(g0 + (n & -(2 * b)), b)], zsem)))
                b //= 2
        for k in range(N_CLASSES):
            step = zinfo_ref[2 * N_CLASSES] + k
            out.append((step < steps, pltpu.make_async_copy(
                zbuf, xs_ref.at[pl.ds(jnp.minimum(step, steps - 1) * zgroups, zgroups)], zsem)))
        return out

    @pl.when(i == 0)
    def _():
        zbuf[...] = jnp.zeros_like(zbuf)
        for pred, cp in zero_copies():
            @pl.when(pred)
            def _():
                cp.start()
        for pred, cp in zero_copies():
            @pl.when(pred)
            def _():
                cp.wait()

    def scatter(t_ref):
        groups = t_ref.shape[0]

        def row_copy(j, u, hi, lo):
            return pltpu.make_async_copy(t_ref.at[j, pl.ds(u, 1), :],
                                         xs_ref.at[hi, pl.ds(lo, 1), :], sem)

        _start_rows(groups * SUBLANES, dest_ref, row_copy)
        pltpu.make_async_copy(t_ref, xs_ref.at[pl.ds(0, groups)], sem).wait()

    @pl.when(i < nbp)
    def _():
        scatter(tp_ref)

    @pl.when(i >= nbp)
    def _():
        scatter(ts_ref)


def _dispatch(t_p, t_s, dest3, zinfo, P):
    nb, _, td = dest3.shape
    W = t_p.shape[1]
    assert t_p.shape[0] % td == 0 and t_s.shape[0] % td == 0 and FFN_TILE % SUBLANES == 0
    nbp, nbs = t_p.shape[0] // td, t_s.shape[0] // td
    assert nb == nbp + nbs
    zgroups = FFN_TILE // SUBLANES
    body = functools.partial(_dispatch_body, nbp=nbp, steps=P // FFN_TILE)
    grid_spec = pltpu.PrefetchScalarGridSpec(
        num_scalar_prefetch=1,
        grid=(nb,),
        in_specs=[
            pl.BlockSpec((1, 1, td), lambda i, z: (i, 0, 0), memory_space=pltpu.SMEM),
            pl.BlockSpec((td // SUBLANES, SUBLANES, W), lambda i, z: (jnp.minimum(i, nbp - 1), 0, 0)),
            pl.BlockSpec((td // SUBLANES, SUBLANES, W), lambda i, z: (jnp.maximum(i - nbp, 0), 0, 0)),
        ],
        out_specs=pl.BlockSpec(memory_space=pl.ANY),
        scratch_shapes=[pltpu.VMEM((zgroups, SUBLANES, W), F32),
                        pltpu.SemaphoreType.DMA(()), pltpu.SemaphoreType.DMA(())],
    )
    return pl.pallas_call(
        body,
        out_shape=jax.ShapeDtypeStruct((P // SUBLANES, SUBLANES, W), F32),
        grid_spec=grid_spec,
        compiler_params=_params(("arbitrary",)),
        name="moe_dispatch",
    )(zinfo, dest3, t_p.reshape(-1, SUBLANES, W), t_s.reshape(-1, SUBLANES, W))


def _ffn_body(ea_ref, eb_ref, halves_ref, xs_ref, wga_ref, wua_ref, wda_ref,
              wgb_ref, wub_ref, wdb_ref, ys_ref, wg_scr, wu_scr, wd_scr):
    i = pl.program_id(0)
    tg, D = ys_ref.shape
    nh = halves_ref[i]
    prev = jnp.maximum(i - 1, 0)
    new_class = (i == 0) | (ea_ref[i] != ea_ref[prev]) | (eb_ref[i] != eb_ref[prev])

    @pl.when((nh > 0) & new_class)
    def _():
        for slot, (wg, wu, wd) in enumerate(((wga_ref, wua_ref, wda_ref), (wgb_ref, wub_ref, wdb_ref))):
            wg_scr[slot] = wg[0, 0].astype(BF16)
            wu_scr[slot] = wu[0, 0].astype(BF16)
            wd_scr[slot] = wd[0, 0].astype(BF16)

    def experts(rows):
        x = xs_ref[rows, :D].astype(BF16)
        w_a = xs_ref[rows, D:D + 1]
        w_b = xs_ref[rows, D + 1:D + 2]

        def expert(slot):
            gt = _dot(x, wg_scr[slot])
            hdn = (gt * jax.nn.sigmoid(gt)) * _dot(x, wu_scr[slot])
            return _dot(hdn.astype(BF16), wd_scr[slot])

        return w_a * expert(0) + w_b * expert(1)

    @pl.when(nh == 2)
    def _():
        ys_ref[...] = experts(slice(0, tg))

    @pl.when(nh == 1)
    def _():
        ys_ref[:tg - FFN_HALF, :] = jnp.zeros((tg - FFN_HALF, D), F32)
        ys_ref[tg - FFN_HALF:, :] = experts(slice(tg - FFN_HALF, tg))

    @pl.when(nh == 0)
    def _():
        ys_ref[...] = jnp.zeros_like(ys_ref)


def _ffn(xs, ea, eb, halves, layer, wg, wu, wd):
    P, W = xs.shape
    D = W - EXTRA
    ff = wg.shape[3]
    tg = FFN_TILE
    wspec_a = lambda shape: pl.BlockSpec(shape, lambda i, ea, eb, nh: (layer, ea[i], 0, 0))
    wspec_b = lambda shape: pl.BlockSpec(shape, lambda i, ea, eb, nh: (layer, eb[i], 0, 0))
    grid_spec = pltpu.PrefetchScalarGridSpec(
        num_scalar_prefetch=3,
        grid=(P // tg,),
        in_specs=[
            pl.BlockSpec((tg, W), lambda i, ea, eb, nh: (i, 0)),
            wspec_a((1, 1, D, ff)), wspec_a((1, 1, D, ff)), wspec_a((1, 1, ff, D)),
            wspec_b((1, 1, D, ff)), wspec_b((1, 1, D, ff)), wspec_b((1, 1, ff, D)),
        ],
        out_specs=pl.BlockSpec((tg, D), lambda i, ea, eb, nh: (i, 0)),
        scratch_shapes=[pltpu.VMEM((2, D, ff), BF16), pltpu.VMEM((2, D, ff), BF16),
                        pltpu.VMEM((2, ff, D), BF16)],
    )
    return pl.pallas_call(
        _ffn_body,
        out_shape=jax.ShapeDtypeStruct((P, D), F32),
        grid_spec=grid_spec,
        compiler_params=_params(("arbitrary",)),
        name="moe_ffn",
    )(ea, eb, halves, xs, wg, wu, wd, wg, wu, wd)


def _combine_body(dest_ref, dnext_ref, x_ref, gfin_ref, ys_ref, o_ref, ybuf, sems, *, final_norm):
    i, n = pl.program_id(0), pl.num_programs(0)
    tc, D = x_ref.shape
    groups = tc // SUBLANES

    def gather(idx_ref, slot):
        def row_copy(j, u, hi, lo):
            return pltpu.make_async_copy(ys_ref.at[hi, pl.ds(lo, 1), :],
                                         ybuf.at[slot, j, pl.ds(u, 1), :], sems.at[slot])
        _start_rows(tc, idx_ref, row_copy)

    @pl.when(i == 0)
    def _():
        gather(dest_ref, 0)

    for slot in range(2):
        @pl.when((i + 1 < n) & ((i + 1) % 2 == slot))
        def _():
            gather(dnext_ref, slot)

    for slot in range(2):
        @pl.when(i % 2 == slot)
        def _():
            pltpu.make_async_copy(ys_ref.at[pl.ds(0, groups)], ybuf.at[slot], sems.at[slot]).wait()
            y = x_ref[...] + ybuf[slot].reshape(tc, D)
            if final_norm:
                y = _rms(y, gfin_ref[...])
            o_ref[...] = y


def _combine(x, dest3, ys, gfin, final_norm):
    T, D = x.shape
    nb, _, tc = dest3.shape
    body = functools.partial(_combine_body, final_norm=final_norm)
    return pl.pallas_call(
        body,
        out_shape=jax.ShapeDtypeStruct((T, D), F32),
        grid=(nb,),
        in_specs=[
            pl.BlockSpec((1, 1, tc), lambda i: (i, 0, 0), memory_space=pltpu.SMEM),
            pl.BlockSpec((1, 1, tc), lambda i: (jnp.minimum(i + 1, nb - 1), 0, 0),
                         memory_space=pltpu.SMEM),
            pl.BlockSpec((tc, D), lambda i: (i, 0)),
            _const_spec((1, D)),
            pl.BlockSpec(memory_space=pl.ANY),
        ],
        out_specs=pl.BlockSpec((tc, D), lambda i: (i, 0)),
        scratch_shapes=[pltpu.VMEM((2, tc // SUBLANES, SUBLANES, D), F32),
                        pltpu.SemaphoreType.DMA((2,))],
        compiler_params=_params(("arbitrary",)),
        name="moe_combine",
    )(dest3, dest3, x, gfin, ys.reshape(-1, SUBLANES, D))


def _moe_tables(metas, counts, P):
    tg = FFN_TILE
    cnt = counts[0, :N_CLASSES].astype(jnp.int32)
    padded = ((cnt + tg - 1) // tg) * tg
    ends = jnp.cumsum(padded)
    offs = ends - padded
    front = padded - cnt
    dests = []
    for m in metas:
        m = m.reshape(-1, SUBLANES, m.shape[1])
        cls, rank = m[:, 0, :].reshape(-1).astype(jnp.int32), m[:, 1, :].reshape(-1).astype(jnp.int32)
        dests.append(jnp.take(offs + front, cls) + rank)
    tile_start = jnp.arange(P // tg, dtype=jnp.int32) * tg
    tile_cls = jnp.minimum(jnp.sum((tile_start[:, None] >= ends[None, :]).astype(jnp.int32), axis=1),
                           N_CLASSES - 1)
    rows = tg - jnp.clip(jnp.take(offs + front, tile_cls) - tile_start, 0, tg)
    halves = jnp.where(tile_start < ends[-1], (rows + FFN_HALF - 1) // FFN_HALF, 0).astype(jnp.int32)
    grp = tile_cls // N_PAIRS
    pair = tile_cls % N_PAIRS
    ea = grp * MOE_PER_GROUP + jnp.take(jnp.asarray(_PAIR_LO), pair)
    eb = grp * MOE_PER_GROUP + jnp.take(jnp.asarray(_PAIR_HI), pair)
    zinfo = jnp.concatenate([offs, offs + front, ends[-1:] // tg]).astype(jnp.int32)
    return dests, ea, eb, halves, zinfo


def _moe(xs_in, g, w_group, b_group, w_router, b_router, layer, wg, wu, wd, gfin, final_norm):
    assert len(xs_in) == 2
    D = xs_in[0].shape[1]
    T = sum(x.shape[0] for x in xs_in)
    wcat = jnp.zeros((D, LANES), F32)
    wcat = wcat.at[:, :MOE_GROUPS].set(w_group).at[:, MOE_GROUPS:MOE_GROUPS + MOE_EXPERTS].set(w_router)
    bias = jnp.zeros((1, LANES), F32)
    bias = bias.at[0, :MOE_GROUPS].set(b_group).at[0, MOE_GROUPS:MOE_GROUPS + MOE_EXPERTS].set(b_router)
    wc = wcat.astype(BF16)

    counts = jnp.zeros((1, LANES), F32)
    texts, metas = [], []
    for x in xs_in:
        t_ext, meta, counts = _router(x, g, wc, bias, counts)
        texts.append(t_ext)
        metas.append(meta)

    tg = FFN_TILE
    P = ((T + tg - 1) // tg + N_CLASSES) * tg
    dests, ea, eb, halves, zinfo = _moe_tables(metas, counts, P)
    tmove = _pick_tile(math.gcd(*(x.shape[0] for x in xs_in)), MOVE_TILE)
    dest3s = [dest.reshape(-1, 1, tmove) for dest in dests]
    xs = _dispatch(texts[0], texts[1], jnp.concatenate(dest3s, axis=0), zinfo, P)
    ys = _ffn(xs.reshape(P, D + EXTRA), ea, eb, halves, layer, wg, wu, wd)
    return [_combine(x, dest3, ys, gfin, final_norm) for x, dest3 in zip(xs_in, dest3s)]


def _rope_tables(pos, dk):
    half = dk // 2
    inv = 1.0 / (ROPE_BASE ** jnp.linspace(0.0, 1.0, half, dtype=F32))
    ang = pos.astype(F32)[:, None] * inv[None, :]
    return jnp.cos(ang), jnp.sin(ang)


def kernel(x_prompt, x_sample, state_ret, state_conv, state_lru, norm_mix_g, norm_ffn_g, norm_final_g, ret_w_in, ret_w_out, ret_norm_g, lru_w_in, lru_conv_w, lru_conv_b, lru_w_a, lru_b_a, lru_w_x, lru_b_x, lru_lambda, lru_w_out, moe_w_group, moe_b_group, moe_w_router, moe_b_router, moe_w_gate, moe_w_up, moe_w_down):
    B, S, D = x_prompt.shape
    Bs, Ls, _ = x_sample.shape
    Tp, Ts = B * S, Bs * Ls
    assert norm_mix_g.shape[0] == 2
    dk = D // RET_HEADS
    row = lambda v: v.reshape(1, -1)
    gfin = row(norm_final_g)

    cos_p, sin_p = _rope_tables(jnp.arange(S), dk)
    cos_s, sin_s = _rope_tables(PAST_LEN + jnp.arange(Ls), dk)
    ret_args = (row(norm_mix_g[0]), ret_w_in[0].astype(BF16), ret_w_out[0].astype(BF16),
                row(ret_norm_g[0]))
    xp, ret_p = _ret_prompt(x_prompt.reshape(Tp, D), B, S, *ret_args, cos_p, sin_p)
    xs, ret_s = _ret_sample(x_sample.reshape(Ts, D), Bs, Ls, *ret_args, cos_s, sin_s, state_ret[0])
    xp, xs = _moe([xp, xs], row(norm_ffn_g[0]), moe_w_group[0], moe_b_group[0], moe_w_router[0],
                  moe_b_router[0], 0, moe_w_gate, moe_w_up, moe_w_down, gfin, False)

    lru_args = (row(norm_mix_g[1]), lru_w_in[0].astype(BF16), lru_conv_w[0], row(lru_conv_b[0]),
                lru_w_a[0].astype(BF16), row(lru_b_a[0]), lru_w_x[0].astype(BF16), row(lru_b_x[0]),
                row(lru_lambda[0]), lru_w_out[0].astype(BF16))
    cs8 = jnp.pad(state_conv[0], ((0, 0), (8 - (CONV_W - 1), 0), (0, 0)))
    h0_rows = jnp.pad(state_lru[0].reshape(Bs, 1, D), ((0, 0), (0, Ls - 1), (0, 0)))
    xp, conv_p, lru_p = _lru_prompt(xp, B, S, *lru_args)
    xs, conv_s, lru_s = _lru_sample(xs, Bs, Ls, *lru_args, cs8, h0_rows)
    y_p, y_s = _moe([xp, xs], row(norm_ffn_g[1]), moe_w_group[1], moe_b_group[1], moe_w_router[1],
                    moe_b_router[1], 1, moe_w_gate, moe_w_up, moe_w_down, gfin, True)

    return (y_p.reshape(B, S, D), y_s.reshape(Bs, Ls, D),
            ret_p[None], conv_p[None, :, 8 - (CONV_W - 1):], lru_p[None, :, 7],
            ret_s[None], conv_s[None, :, 8 - (CONV_W - 1):], lru_s[None, :, 7])
```

```python
import functools
import math

import numpy as np
import jax
import jax.numpy as jnp
from jax import lax
from jax.experimental import pallas as pl
from jax.experimental.pallas import tpu as pltpu

F32 = jnp.float32
BF16 = jnp.bfloat16

RET_HEADS = 4
RET_CHUNK = 64
ROPE_BASE = 10000.0
CONV_W = 4
LRU_BLOCKS = 4
LRU_C = 8.0
MOE_GROUPS = 4
MOE_PER_GROUP = 4
MOE_EXPERTS = MOE_GROUPS * MOE_PER_GROUP
NORM_EPS = 1e-6
PAST_LEN = 2048

LANES = 128
SUBLANES = 8
VMEM_LIMIT_BYTES = 56 * 1024 * 1024

RET_TILE = 512
RET_SCORE_BLOCK = 256
LRU_TILE = 512
ROUTE_TILE = 1024
MOVE_TILE = 512
FFN_TILE = 512
FFN_HALF = 256
N_PAIRS = 6
N_CLASSES = MOE_GROUPS * N_PAIRS
EXTRA = LANES
NEG_BIG = -1e30

_PAIR_LO = np.array([0, 0, 0, 1, 1, 2], np.int32)
_PAIR_HI = np.array([1, 2, 3, 2, 3, 3], np.int32)


def _log_gammas():
    return [math.log1p(-(2.0 ** (-5.0 - h))) for h in range(RET_HEADS)]


def _rms(x, g):
    return x * lax.rsqrt(jnp.mean(x * x, axis=-1, keepdims=True) + NORM_EPS) * g


def _dot(a, b):
    return jnp.dot(a, b, preferred_element_type=F32)


def _dot_nt(a, b):
    return lax.dot_general(a, b, (((1,), (1,)), ((), ())), preferred_element_type=F32)


def _dot_tn(a, b):
    return lax.dot_general(a, b, (((0,), (0,)), ((), ())), preferred_element_type=F32)


def _const_spec(shape):
    nd = len(shape)
    return pl.BlockSpec(shape, lambda *_: (0,) * nd, pipeline_mode=pl.Buffered(1))


def _params(sem):
    return pltpu.CompilerParams(dimension_semantics=sem, vmem_limit_bytes=VMEM_LIMIT_BYTES)


def _pick_tile(n, pref):
    t = pref
    while n % t:
        t //= 2
    assert t >= SUBLANES, (n, pref)
    return t


def _decay_matrix(L, chunk, lg):
    ii = lax.broadcasted_iota(jnp.int32, (L, L), 0)
    jj = lax.broadcasted_iota(jnp.int32, (L, L), 1)
    diff = (ii - jj).astype(F32)
    same_chunk = (ii // chunk) == (jj // chunk)
    return jnp.where(same_chunk & (ii >= jj), jnp.exp(jnp.maximum(diff, 0.0) * lg), 0.0)


def _rotary(t, cos, sin):
    half = t.shape[-1] // 2
    t1, t2 = t[:, :half], t[:, half:]
    return jnp.concatenate([t1 * cos - t2 * sin, t1 * sin + t2 * cos], axis=-1)


def _ret_proj(hn, win_ref, hd, dk, dv):
    qk, vv = RET_HEADS * dk, RET_HEADS * dv
    q = _dot(hn, win_ref[:, hd * dk:(hd + 1) * dk])
    k = _dot(hn, win_ref[:, qk + hd * dk:qk + (hd + 1) * dk])
    v = _dot(hn, win_ref[:, 2 * qk + hd * dv:2 * qk + (hd + 1) * dv])
    gt = _dot(hn, win_ref[:, 2 * qk + vv + hd * dv:2 * qk + vv + (hd + 1) * dv])
    return q, k, v, gt


def _ret_head(q, k, v, gt, cos, sin, decay, s0, ng, chunk, lg, kv_scr=None):
    L, dk = q.shape
    qr = _rotary(q, cos, sin)
    kr = _rotary(k, cos, sin) * (dk ** -0.5)
    idx = (lax.broadcasted_iota(jnp.int32, (L, 1), 0) % chunk).astype(F32)
    qd = (qr * jnp.exp((idx + 1.0) * lg)).astype(BF16)
    kd = (kr * jnp.exp((chunk - 1.0 - idx) * lg)).astype(BF16)
    qb, kb, vb = qr.astype(BF16), kr.astype(BF16), v.astype(BF16)
    core = decay.shape[0]
    chunks = [slice(c * chunk, (c + 1) * chunk) for c in range(L // chunk)]
    if kv_scr is not None:
        for c, rows in enumerate(chunks):
            kv_scr[c] = _dot_tn(kd[rows], vb[rows])
    s, parts = s0, []
    for r0 in range(0, L, core):
        blk = slice(r0, r0 + core)
        scores = _dot_nt(qb[blk], kb[blk]) * decay
        intra = _dot(scores.astype(BF16), vb[blk])
        for c in range(r0 // chunk, (r0 + core) // chunk):
            rows = chunks[c]
            parts.append(intra[rows.start - r0:rows.stop - r0] + _dot(qd[rows], s.astype(BF16)))
            kv = kv_scr[c] if kv_scr is not None else _dot_tn(kd[rows], vb[rows])
            s = math.exp(chunk * lg) * s + kv
    o = parts[0] if len(parts) == 1 else jnp.concatenate(parts, axis=0)
    o = o * lax.rsqrt(jnp.mean(o * o, axis=-1, keepdims=True) + NORM_EPS)
    og = (gt * jax.nn.sigmoid(gt)) * (o * ng)
    return og, s


def _ret_prompt_body(x_ref, g_ref, win_ref, wout_ref, ng_ref, cos_ref, sin_ref,
                     y_ref, s_ref, decay_scr, og_scr, kv_scr, *, L, chunk, dk, dv):
    b, c = pl.program_id(0), pl.program_id(1)
    lgs = _log_gammas()

    @pl.when((b == 0) & (c == 0))
    def _():
        for hd in range(RET_HEADS):
            decay_scr[hd] = _decay_matrix(decay_scr.shape[1], chunk, lgs[hd])

    @pl.when(c == 0)
    def _():
        s_ref[...] = jnp.zeros_like(s_ref)

    x = x_ref[...]
    hn = _rms(x, g_ref[...]).astype(BF16)
    cos, sin = cos_ref[...], sin_ref[...]
    for hd in range(RET_HEADS):
        og, s_new = _ret_head(*_ret_proj(hn, win_ref, hd, dk, dv), cos, sin, decay_scr[hd],
                              s_ref[0, hd], ng_ref[:, hd * dv:(hd + 1) * dv], chunk, lgs[hd],
                              kv_scr)
        s_ref[0, hd] = s_new
        og_scr[:, hd * dv:(hd + 1) * dv] = og.astype(BF16)
    y_ref[...] = x + _dot(og_scr[...], wout_ref[...])


def _ret_prompt(x_p, B, S, g, win, wout, ng, cos, sin):
    Tp, D = x_p.shape
    dk, dv = D // RET_HEADS, 2 * D // RET_HEADS
    chunk = min(S, RET_CHUNK)
    L = _pick_tile(S, RET_TILE)
    core = min(L, RET_SCORE_BLOCK)
    assert L % core == 0 and core % chunk == 0
    nc = S // L
    body = functools.partial(_ret_prompt_body, L=L, chunk=chunk, dk=dk, dv=dv)
    return pl.pallas_call(
        body,
        out_shape=(jax.ShapeDtypeStruct((Tp, D), F32),
                   jax.ShapeDtypeStruct((B, RET_HEADS, dk, dv), F32)),
        grid=(B, nc),
        in_specs=[
            pl.BlockSpec((L, D), lambda b, c: (b * nc + c, 0)),
            _const_spec((1, D)),
            _const_spec(win.shape),
            _const_spec(wout.shape),
            _const_spec((1, RET_HEADS * dv)),
            pl.BlockSpec((L, dk // 2), lambda b, c: (c, 0)),
            pl.BlockSpec((L, dk // 2), lambda b, c: (c, 0)),
        ],
        out_specs=(
            pl.BlockSpec((L, D), lambda b, c: (b * nc + c, 0)),
            pl.BlockSpec((1, RET_HEADS, dk, dv), lambda b, c: (b, 0, 0, 0)),
        ),
        scratch_shapes=[pltpu.VMEM((RET_HEADS, core, core), F32),
                        pltpu.VMEM((L, RET_HEADS * dv), BF16),
                        pltpu.VMEM((L // chunk, dk, dv), F32)],
        compiler_params=_params(("arbitrary", "arbitrary")),
        name="ret_prompt",
    )(x_p, g, win, wout, ng, cos, sin)


def _ret_sample_body(x_ref, g_ref, win_ref, wout_ref, ng_ref, cos_ref, sin_ref, s0_ref,
                     y_ref, s_ref, q_scr, k_scr, v_scr, gt_scr, og_scr, *, L, dk, dv):
    b, nb = pl.program_id(0), pl.num_programs(0)
    lgs = _log_gammas()

    @pl.when(b == 0)
    def _():
        hn = _rms(x_ref[...], g_ref[...]).astype(BF16)
        for hd in range(RET_HEADS):
            q, k, v, gt = _ret_proj(hn, win_ref, hd, dk, dv)
            q_scr[hd], k_scr[hd], v_scr[hd], gt_scr[hd] = q, k, v, gt

    rows = pl.ds(pl.multiple_of(b * L, L), L)
    cos, sin = cos_ref[...], sin_ref[...]
    for hd in range(RET_HEADS):
        og, s_new = _ret_head(q_scr[hd, rows, :], k_scr[hd, rows, :], v_scr[hd, rows, :],
                              gt_scr[hd, rows, :], cos, sin, _decay_matrix(L, L, lgs[hd]),
                              s0_ref[0, hd], ng_ref[:, hd * dv:(hd + 1) * dv], L, lgs[hd])
        s_ref[0, hd] = s_new
        og_scr[rows, hd * dv:(hd + 1) * dv] = og.astype(BF16)

    @pl.when(b == nb - 1)
    def _():
        y_ref[...] = x_ref[...] + _dot(og_scr[...], wout_ref[...])


def _ret_sample(x_s, Bs, Ls, g, win, wout, ng, cos, sin, s0):
    Ts, D = x_s.shape
    dk, dv = D // RET_HEADS, 2 * D // RET_HEADS
    assert Ls <= RET_CHUNK
    body = functools.partial(_ret_sample_body, L=Ls, dk=dk, dv=dv)
    return pl.pallas_call(
        body,
        out_shape=(jax.ShapeDtypeStruct((Ts, D), F32),
                   jax.ShapeDtypeStruct((Bs, RET_HEADS, dk, dv), F32)),
        grid=(Bs,),
        in_specs=[
            _const_spec((Ts, D)),
            _const_spec((1, D)),
            _const_spec(win.shape),
            _const_spec(wout.shape),
            _const_spec((1, RET_HEADS * dv)),
            _const_spec((Ls, dk // 2)),
            _const_spec((Ls, dk // 2)),
            pl.BlockSpec((1, RET_HEADS, dk, dv), lambda b: (b, 0, 0, 0)),
        ],
        out_specs=(
            pl.BlockSpec((Ts, D), lambda b: (0, 0)),
            pl.BlockSpec((1, RET_HEADS, dk, dv), lambda b: (b, 0, 0, 0)),
        ),
        scratch_shapes=[pltpu.VMEM((RET_HEADS, Ts, dk), F32), pltpu.VMEM((RET_HEADS, Ts, dk), F32),
                        pltpu.VMEM((RET_HEADS, Ts, dv), F32), pltpu.VMEM((RET_HEADS, Ts, dv), F32),
                        pltpu.VMEM((Ts, RET_HEADS * dv), BF16)],
        compiler_params=_params(("arbitrary",)),
        name="ret_sample",
    )(x_s, g, win, wout, ng, cos, sin, s0)


def _gelu_tanh(x):
    return 0.5 * x * (1.0 + jnp.tanh(math.sqrt(2.0 / math.pi) * (x + 0.044715 * (x * x * x))))


def _softplus(z):
    return jnp.maximum(z, 0.0) + jnp.log1p(jnp.exp(-jnp.abs(z)))


def _lru_gates(conv, wa_ref, wx_ref, ba, bx, lam):
    D = conv.shape[1]
    bw = D // LRU_BLOCKS
    ra, ix = [], []
    for n in range(LRU_BLOCKS):
        cb = conv[:, n * bw:(n + 1) * bw].astype(BF16)
        ra.append(_dot(cb, wa_ref[n]))
        ix.append(_dot(cb, wx_ref[n]))
    r = jax.nn.sigmoid(jnp.concatenate(ra, axis=-1) + ba)
    i = jax.nn.sigmoid(jnp.concatenate(ix, axis=-1) + bx)
    log_a = r * ((-LRU_C) * _softplus(-lam))
    a = jnp.exp(log_a)
    m2 = jnp.tanh(-log_a) * (1.0 + a * a)
    mult = jnp.where(m2 > 0.0, m2 * lax.rsqrt(m2), 0.0)
    return a, mult, i * conv


def _seg_scan(a, b, seg):
    R = a.shape[0]
    row = lax.broadcasted_iota(jnp.int32, (R, 1), 0) % seg
    k = 1
    while k < seg:
        keep = row >= k
        a_sh = jnp.where(keep, pltpu.roll(a, k, axis=0), 1.0)
        b_sh = jnp.where(keep, pltpu.roll(b, k, axis=0), 0.0)
        b = a * b_sh + b
        a = a * a_sh
        k *= 2
    return b


def _carry_scan(a, b, h_prev):
    L, D = a.shape
    G = L // SUBLANES
    a3, b3 = a.reshape(G, SUBLANES, D), b.reshape(G, SUBLANES, D)
    sub = lax.broadcasted_iota(jnp.int32, (SUBLANES, D), 0)[None]
    k = 1
    while k < SUBLANES:
        keep = sub >= k
        a_sh = jnp.where(keep, pltpu.roll(a3, k, axis=1), 1.0)
        b_sh = jnp.where(keep, pltpu.roll(b3, k, axis=1), 0.0)
        b3 = a3 * b_sh + b3
        a3 = a3 * a_sh
        k *= 2
    carry, outs = h_prev, []
    for j in range(G):
        r = a3[j] * carry + b3[j]
        outs.append(r)
        carry = r[SUBLANES - 1:SUBLANES, :]
    return jnp.concatenate(outs, axis=0)


def _lru_prompt_body(x_ref, g_ref, win_ref, cw_ref, cb_ref, wa_ref, ba_ref, wx_ref, bx_ref,
                     lam_ref, wout_ref, y_ref, conv_ref, h_ref, xc_scr, hprev_scr, *, L):
    c = pl.program_id(1)
    D = x_ref.shape[1]

    @pl.when(c == 0)
    def _():
        xc_scr[...] = jnp.zeros_like(xc_scr)
        hprev_scr[...] = jnp.zeros_like(hprev_scr)

    x = x_ref[...]
    hn = _rms(x, g_ref[...]).astype(BF16)
    gate = _gelu_tanh(_dot(hn, win_ref[:, :D]))
    x_br = _dot(hn, win_ref[:, D:])
    prev8 = xc_scr[...]
    sub8 = lax.broadcasted_iota(jnp.int32, (SUBLANES, 1), 0)

    def shifted(k):
        rolled = pltpu.roll(x_br, k, axis=0)
        first = jnp.where(sub8 < k, pltpu.roll(prev8, k, axis=0), rolled[0:SUBLANES, :])
        return jnp.concatenate([first, rolled[SUBLANES:, :]], axis=0)

    conv = cb_ref[...] + shifted(CONV_W - 1) * cw_ref[0:1, :]
    for j in range(1, CONV_W - 1):
        conv = conv + shifted(CONV_W - 1 - j) * cw_ref[j:j + 1, :]
    conv = conv + x_br * cw_ref[CONV_W - 1:CONV_W, :]
    a, mult, gi = _lru_gates(conv, wa_ref, wx_ref, ba_ref[...], bx_ref[...], lam_ref[...])
    first = jnp.where((sub8 == 0) & (c == 0), 1.0, mult[0:SUBLANES, :])
    mult = jnp.concatenate([first, mult[SUBLANES:, :]], axis=0)
    hs = _carry_scan(a, mult * gi, hprev_scr[7:8, :])
    y_ref[...] = x + _dot((hs * gate).astype(BF16), wout_ref[...])
    hprev_scr[...] = hs[L - 8:L, :]
    h_ref[0] = hs[L - 8:L, :]
    conv_ref[0] = x_br[L - 8:L, :]
    xc_scr[...] = x_br[L - 8:L, :]


def _lru_prompt(x_p, B, S, g, win, cw, cb, wa, ba, wx, bx, lam, wout):
    Tp, D = x_p.shape
    L = _pick_tile(S, LRU_TILE)
    nc = S // L
    body = functools.partial(_lru_prompt_body, L=L)
    return pl.pallas_call(
        body,
        out_shape=(jax.ShapeDtypeStruct((Tp, D), F32),
                   jax.ShapeDtypeStruct((B, 8, D), F32),
                   jax.ShapeDtypeStruct((B, 8, D), F32)),
        grid=(B, nc),
        in_specs=[
            pl.BlockSpec((L, D), lambda b, c: (b * nc + c, 0)),
            _const_spec((1, D)), _const_spec(win.shape), _const_spec(cw.shape), _const_spec((1, D)),
            _const_spec(wa.shape), _const_spec((1, D)), _const_spec(wx.shape), _const_spec((1, D)),
            _const_spec((1, D)), _const_spec(wout.shape),
        ],
        out_specs=(
            pl.BlockSpec((L, D), lambda b, c: (b * nc + c, 0)),
            pl.BlockSpec((1, 8, D), lambda b, c: (b, 0, 0)),
            pl.BlockSpec((1, 8, D), lambda b, c: (b, 0, 0)),
        ),
        scratch_shapes=[pltpu.VMEM((8, D), F32), pltpu.VMEM((8, D), F32)],
        compiler_params=_params(("arbitrary", "arbitrary")),
        name="lru_prompt",
    )(x_p, g, win, cw, cb, wa, ba, wx, bx, lam, wout)


def _lru_sample_body(x_ref, g_ref, win_ref, cw_ref, cb_ref, wa_ref, ba_ref, wx_ref, bx_ref,
                     lam_ref, wout_ref, cs_ref, h0_ref, y_ref, conv_ref, h_ref,
                     xc_scr, *, Bs, Ls):
    Ts, D = x_ref.shape
    x = x_ref[...]
    hn = _rms(x, g_ref[...]).astype(BF16)
    gate = _gelu_tanh(_dot(hn, win_ref[:, :D]))
    x_br = _dot(hn, win_ref[:, D:])
    xc_scr[:, 0:8, :] = cs_ref[...]
    xc_scr[:, 8:8 + Ls, :] = x_br.reshape(Bs, Ls, D)
    conv = cb_ref[...] + xc_scr[:, 5:5 + Ls, :] * cw_ref[0:1, :]
    for j in range(1, CONV_W):
        conv = conv + xc_scr[:, 5 + j:5 + j + Ls, :] * cw_ref[j:j + 1, :]
    conv = conv.reshape(Ts, D)
    a, mult, gi = _lru_gates(conv, wa_ref, wx_ref, ba_ref[...], bx_ref[...], lam_ref[...])
    bvec = mult * gi + a * h0_ref[...].reshape(Ts, D)
    hs = _seg_scan(a, bvec, Ls)
    y_ref[...] = x + _dot((hs * gate).astype(BF16), wout_ref[...])
    h_ref[...] = hs.reshape(Bs, Ls, D)[:, Ls - 8:Ls, :]
    conv_ref[...] = xc_scr[:, Ls:Ls + 8, :]


def _lru_sample(x_s, Bs, Ls, g, win, cw, cb, wa, ba, wx, bx, lam, wout, cs, h0):
    Ts, D = x_s.shape
    assert Ls % 8 == 0
    body = functools.partial(_lru_sample_body, Bs=Bs, Ls=Ls)
    return pl.pallas_call(
        body,
        out_shape=(jax.ShapeDtypeStruct((Ts, D), F32),
                   jax.ShapeDtypeStruct((Bs, 8, D), F32),
                   jax.ShapeDtypeStruct((Bs, 8, D), F32)),
        grid=(1,),
        in_specs=[
            _const_spec((Ts, D)),
            _const_spec((1, D)), _const_spec(win.shape), _const_spec(cw.shape), _const_spec((1, D)),
            _const_spec(wa.shape), _const_spec((1, D)), _const_spec(wx.shape), _const_spec((1, D)),
            _const_spec((1, D)), _const_spec(wout.shape),
            _const_spec((Bs, 8, D)), _const_spec((Bs, Ls, D)),
        ],
        out_specs=(
            pl.BlockSpec((Ts, D), lambda i: (0, 0)),
            pl.BlockSpec((Bs, 8, D), lambda i: (0, 0, 0)),
            pl.BlockSpec((Bs, 8, D), lambda i: (0, 0, 0)),
        ),
        scratch_shapes=[pltpu.VMEM((Bs, Ls + 8, D), F32)],
        compiler_params=_params(("arbitrary",)),
        name="lru_sample",
    )(x_s, g, win, cw, cb, wa, ba, wx, bx, lam, wout, cs, h0)


def _router_body(x_ref, g_ref, wc_ref, bias_ref, cin_ref, text_ref, meta_ref, cnt_ref, run_scr,
                 tri_scr):
    i = pl.program_id(0)
    tm, D = x_ref.shape

    @pl.when(i == 0)
    def _():
        run_scr[...] = cin_ref[...]
        ri = lax.broadcasted_iota(jnp.int32, (tm, tm), 0)
        ci = lax.broadcasted_iota(jnp.int32, (tm, tm), 1)
        tri_scr[...] = (ci < ri).astype(BF16)

    t = _rms(x_ref[...], g_ref[...])
    logits = _dot(t.astype(BF16), wc_ref[...]) + bias_ref[...]
    lane_i = lax.broadcasted_iota(jnp.int32, (tm, LANES), 1)
    lane = lane_i.astype(F32)

    def first_argmax(v):
        m = jnp.max(v, axis=-1, keepdims=True)
        return m, jnp.min(jnp.where(v == m, lane, 1e6), axis=-1, keepdims=True)

    gmask = lane_i < MOE_GROUPS
    gl = jnp.where(gmask, logits, NEG_BIG)
    mg, g_idx = first_argmax(gl)
    p_g = 1.0 / jnp.sum(jnp.where(gmask, jnp.exp(gl - mg), 0.0), axis=-1, keepdims=True)
    e_lane = lane_i - MOE_GROUPS
    lane_grp = (e_lane >> 2).astype(F32)
    emask = (e_lane >= 0) & (e_lane < MOE_EXPERTS) & (lane_grp == g_idx)
    el = jnp.where(emask, logits, NEG_BIG)
    m1, i1 = first_argmax(el)
    m2, i2 = first_argmax(jnp.where(lane == i1, NEG_BIG, el))
    e21 = jnp.exp(m2 - m1)
    w1 = p_g / (1.0 + e21)
    w2 = p_g * e21 / (1.0 + e21)
    first_lower = i1 < i2
    j_lo = jnp.minimum(i1, i2) - (MOE_GROUPS + MOE_PER_GROUP * g_idx)
    j_hi = jnp.maximum(i1, i2) - (MOE_GROUPS + MOE_PER_GROUP * g_idx)
    pair = 0.5 * (j_lo * (7.0 - j_lo)) + (j_hi - j_lo - 1.0)
    cls = g_idx * N_PAIRS + pair
    w_lo = jnp.where(first_lower, w1, w2)
    w_hi = jnp.where(first_lower, w2, w1)

    onehot = (lane == cls).astype(F32)
    prefix = _dot(tri_scr[...], onehot.astype(BF16))
    rank = jnp.sum(onehot * (prefix + run_scr[...]), axis=-1, keepdims=True)
    run_scr[...] = run_scr[...] + jnp.sum(onehot, axis=0, keepdims=True)

    text_ref[:, :D] = t
    text_ref[:, D:] = jnp.where(lane_i == 0, w_lo, jnp.where(lane_i == 1, w_hi, 0.0))
    ones8 = jnp.ones((SUBLANES, LANES), BF16)
    ids8 = lax.broadcasted_iota(jnp.int32, (SUBLANES, LANES), 1).astype(F32).astype(BF16)
    d2 = jnp.floor(rank * (1.0 / 65536.0))
    rem = rank - 65536.0 * d2
    d1 = jnp.floor(rem * (1.0 / 256.0))
    d0 = rem - 256.0 * d1

    def as_row(col):
        return _dot_nt(ones8, jnp.where(lane_i == 0, col, 0.0).astype(BF16))

    cls_row = _dot_nt(ids8, onehot.astype(BF16))
    rank_row = as_row(d0) + 256.0 * as_row(d1) + 65536.0 * as_row(d2)
    sub = lax.broadcasted_iota(jnp.int32, (SUBLANES, tm), 0)
    meta_ref[...] = jnp.where(sub == 0, cls_row, jnp.where(sub == 1, rank_row, 0.0))
    cnt_ref[...] = run_scr[...]


def _router(x, g, wc, bias, cnt_in):
    T, D = x.shape
    tm = _pick_tile(T, ROUTE_TILE)
    return pl.pallas_call(
        _router_body,
        out_shape=(jax.ShapeDtypeStruct((T, D + EXTRA), F32),
                   jax.ShapeDtypeStruct((T // tm * SUBLANES, tm), F32),
                   jax.ShapeDtypeStruct((1, LANES), F32)),
        grid=(T // tm,),
        in_specs=[
            pl.BlockSpec((tm, D), lambda i: (i, 0)),
            _const_spec((1, D)), _const_spec(wc.shape), _const_spec((1, LANES)),
            _const_spec((1, LANES)),
        ],
        out_specs=(
            pl.BlockSpec((tm, D + EXTRA), lambda i: (i, 0)),
            pl.BlockSpec((SUBLANES, tm), lambda i: (i, 0)),
            pl.BlockSpec((1, LANES), lambda i: (0, 0)),
        ),
        scratch_shapes=[pltpu.VMEM((1, LANES), F32), pltpu.VMEM((tm, tm), BF16)],
        compiler_params=_params(("arbitrary",)),
        name="moe_router",
    )(x, g, wc, bias, cnt_in)


def _start_rows(n, idx_ref, row_copy):
    assert n % SUBLANES == 0

    def body(j, carry):
        for u in range(SUBLANES):
            pos = idx_ref[0, 0, j * SUBLANES + u]
            row_copy(j, u, pos >> 3, pos & (SUBLANES - 1)).start(priority=u % 2)
        return carry

    lax.fori_loop(0, n // SUBLANES, body, 0)


def _dispatch_body(zinfo_ref, dest_ref, tp_ref, ts_ref, xs_ref, zbuf, sem, zsem, *, nbp, steps):
    i = pl.program_id(0)
    zgroups = zbuf.shape[0]

    def zero_copies():
        out = []
        for c in range(N_CLASSES):
            r0, r1 = zinfo_ref[c], zinfo_ref[N_CLASSES + c]
            g0, g1 = r0 >> 3, r1 >> 3
            for k in range(SUBLANES - 1):
                out.append((k < r1 - g1 * SUBLANES, pltpu.make_async_copy(
                    zbuf.at[0, pl.ds(0, 1), :], xs_ref.at[g1, pl.ds(k, 1), :], zsem)))
            n = g1 - g0
            b = zgroups // 2
            while b >= 1:
                out.append(((n & b) != 0, pltpu.make_async_copy(
                    zbuf.at[pl.ds(0, b)], xs_ref.at[pl.ds(g0 + (n & -(2 * b)), b)], zsem)))
                b //= 2
        for k in range(N_CLASSES):
            step = zinfo_ref[2 * N_CLASSES] + k
            out.append((step < steps, pltpu.make_async_copy(
                zbuf, xs_ref.at[pl.ds(jnp.minimum(step, steps - 1) * zgroups, zgroups)], zsem)))
        return out

    @pl.when(i == 0)
    def _():
        zbuf[...] = jnp.zeros_like(zbuf)
        for pred, cp in zero_copies():
            @pl.when(pred)
            def _():
                cp.start()

    def scatter(t_ref):
        groups = t_ref.shape[0]

        def row_copy(j, u, hi, lo):
            return pltpu.make_async_copy(t_ref.at[j, pl.ds(u, 1), :],
                                         xs_ref.at[hi, pl.ds(lo, 1), :], sem)

        _start_rows(groups * SUBLANES, dest_ref, row_copy)
        pltpu.make_async_copy(t_ref, xs_ref.at[pl.ds(0, groups)], sem).wait()

    @pl.when(i < nbp)
    def _():
        scatter(tp_ref)

    @pl.when(i >= nbp)
    def _():
        scatter(ts_ref)

    @pl.when(i == 0)
    def _():
        for pred, cp in zero_copies():
            @pl.when(pred)
            def _():
                cp.wait()


def _dispatch(t_p, t_s, dest3, zinfo, P):
    nb, _, td = dest3.shape
    W = t_p.shape[1]
    assert t_p.shape[0] % td == 0 and t_s.shape[0] % td == 0 and FFN_TILE % SUBLANES == 0
    nbp, nbs = t_p.shape[0] // td, t_s.shape[0] // td
    assert nb == nbp + nbs
    zgroups = FFN_TILE // SUBLANES
    body = functools.partial(_dispatch_body, nbp=nbp, steps=P // FFN_TILE)
    grid_spec = pltpu.PrefetchScalarGridSpec(
        num_scalar_prefetch=1,
        grid=(nb,),
        in_specs=[
            pl.BlockSpec((1, 1, td), lambda i, z: (i, 0, 0), memory_space=pltpu.SMEM),
            pl.BlockSpec((td // SUBLANES, SUBLANES, W), lambda i, z: (jnp.minimum(i, nbp - 1), 0, 0)),
            pl.BlockSpec((td // SUBLANES, SUBLANES, W), lambda i, z: (jnp.maximum(i - nbp, 0), 0, 0)),
        ],
        out_specs=pl.BlockSpec(memory_space=pl.ANY),
        scratch_shapes=[pltpu.VMEM((zgroups, SUBLANES, W), F32),
                        pltpu.SemaphoreType.DMA(()), pltpu.SemaphoreType.DMA(())],
    )
    return pl.pallas_call(
        body,
        out_shape=jax.ShapeDtypeStruct((P // SUBLANES, SUBLANES, W), F32),
        grid_spec=grid_spec,
        compiler_params=_params(("arbitrary",)),
        name="moe_dispatch",
    )(zinfo, dest3, t_p.reshape(-1, SUBLANES, W), t_s.reshape(-1, SUBLANES, W))


def _ffn_body(ea_ref, eb_ref, halves_ref, xs_ref, wga_ref, wua_ref, wda_ref,
              wgb_ref, wub_ref, wdb_ref, ys_ref, wg_scr, wu_scr, wd_scr):
    i = pl.program_id(0)
    tg, D = ys_ref.shape
    nh = halves_ref[i]
    prev = jnp.maximum(i - 1, 0)
    new_class = (i == 0) | (ea_ref[i] != ea_ref[prev]) | (eb_ref[i] != eb_ref[prev])

    @pl.when((nh > 0) & new_class)
    def _():
        for slot, (wg, wu, wd) in enumerate(((wga_ref, wua_ref, wda_ref), (wgb_ref, wub_ref, wdb_ref))):
            wg_scr[slot] = wg[0, 0].astype(BF16)
            wu_scr[slot] = wu[0, 0].astype(BF16)
            wd_scr[slot] = wd[0, 0].astype(BF16)

    def experts(rows):
        x = xs_ref[rows, :D].astype(BF16)
        w_a = xs_ref[rows, D:D + 1]
        w_b = xs_ref[rows, D + 1:D + 2]

        def expert(slot):
            gt = _dot(x, wg_scr[slot])
            hdn = (gt * jax.nn.sigmoid(gt)) * _dot(x, wu_scr[slot])
            return _dot(hdn.astype(BF16), wd_scr[slot])

        return w_a * expert(0) + w_b * expert(1)

    @pl.when(nh == 2)
    def _():
        ys_ref[...] = experts(slice(0, tg))

    @pl.when(nh == 1)
    def _():
        ys_ref[:tg - FFN_HALF, :] = jnp.zeros((tg - FFN_HALF, D), F32)
        ys_ref[tg - FFN_HALF:, :] = experts(slice(tg - FFN_HALF, tg))

    @pl.when(nh == 0)
    def _():
        ys_ref[...] = jnp.zeros_like(ys_ref)


def _ffn(xs, ea, eb, halves, layer, wg, wu, wd):
    P, W = xs.shape
    D = W - EXTRA
    ff = wg.shape[3]
    tg = FFN_TILE
    wspec_a = lambda shape: pl.BlockSpec(shape, lambda i, ea, eb, nh: (layer, ea[i], 0, 0))
    wspec_b = lambda shape: pl.BlockSpec(shape, lambda i, ea, eb, nh: (layer, eb[i], 0, 0))
    grid_spec = pltpu.PrefetchScalarGridSpec(
        num_scalar_prefetch=3,
        grid=(P // tg,),
        in_specs=[
            pl.BlockSpec((tg, W), lambda i, ea, eb, nh: (i, 0)),
            wspec_a((1, 1, D, ff)), wspec_a((1, 1, D, ff)), wspec_a((1, 1, ff, D)),
            wspec_b((1, 1, D, ff)), wspec_b((1, 1, D, ff)), wspec_b((1, 1, ff, D)),
        ],
        out_specs=pl.BlockSpec((tg, D), lambda i, ea, eb, nh: (i, 0)),
        scratch_shapes=[pltpu.VMEM((2, D, ff), BF16), pltpu.VMEM((2, D, ff), BF16),
                        pltpu.VMEM((2, ff, D), BF16)],
    )
    return pl.pallas_call(
        _ffn_body,
        out_shape=jax.ShapeDtypeStruct((P, D), F32),
        grid_spec=grid_spec,
        compiler_params=_params(("arbitrary",)),
        name="moe_ffn",
    )(ea, eb, halves, xs, wg, wu, wd, wg, wu, wd)


def _combine_body(dest_ref, dnext_ref, x_ref, gfin_ref, ys_ref, o_ref, ybuf, sems, *, final_norm):
    i, n = pl.program_id(0), pl.num_programs(0)
    tc, D = x_ref.shape
    groups = tc // SUBLANES

    def gather(idx_ref, slot):
        def row_copy(j, u, hi, lo):
            return pltpu.make_async_copy(ys_ref.at[hi, pl.ds(lo, 1), :],
                                         ybuf.at[slot, j, pl.ds(u, 1), :], sems.at[slot])
        _start_rows(tc, idx_ref, row_copy)

    @pl.when(i == 0)
    def _():
        gather(dest_ref, 0)

    for slot in range(2):
        @pl.when((i + 1 < n) & ((i + 1) % 2 == slot))
        def _():
            gather(dnext_ref, slot)

    for slot in range(2):
        @pl.when(i % 2 == slot)
        def _():
            pltpu.make_async_copy(ys_ref.at[pl.ds(0, groups)], ybuf.at[slot], sems.at[slot]).wait()
            y = x_ref[...] + ybuf[slot].reshape(tc, D)
            if final_norm:
                y = _rms(y, gfin_ref[...])
            o_ref[...] = y


def _combine(x, dest3, ys, gfin, final_norm):
    T, D = x.shape
    nb, _, tc = dest3.shape
    body = functools.partial(_combine_body, final_norm=final_norm)
    return pl.pallas_call(
        body,
        out_shape=jax.ShapeDtypeStruct((T, D), F32),
        grid=(nb,),
        in_specs=[
            pl.BlockSpec((1, 1, tc), lambda i: (i, 0, 0), memory_space=pltpu.SMEM),
            pl.BlockSpec((1, 1, tc), lambda i: (jnp.minimum(i + 1, nb - 1), 0, 0),
                         memory_space=pltpu.SMEM),
            pl.BlockSpec((tc, D), lambda i: (i, 0)),
            _const_spec((1, D)),
            pl.BlockSpec(memory_space=pl.ANY),
        ],
        out_specs=pl.BlockSpec((tc, D), lambda i: (i, 0)),
        scratch_shapes=[pltpu.VMEM((2, tc // SUBLANES, SUBLANES, D), F32),
                        pltpu.SemaphoreType.DMA((2,))],
        compiler_params=_params(("arbitrary",)),
        name="moe_combine",
    )(dest3, dest3, x, gfin, ys.reshape(-1, SUBLANES, D))


def _moe_tables(metas, counts, P):
    tg = FFN_TILE
    cnt = counts[0, :N_CLASSES].astype(jnp.int32)
    padded = ((cnt + tg - 1) // tg) * tg
    ends = jnp.cumsum(padded)
    offs = ends - padded
    front = padded - cnt
    dests = []
    for m in metas:
        m = m.reshape(-1, SUBLANES, m.shape[1])
        cls, rank = m[:, 0, :].reshape(-1).astype(jnp.int32), m[:, 1, :].reshape(-1).astype(jnp.int32)
        base = jnp.sum(jnp.where(cls[:, None] == jnp.arange(N_CLASSES, dtype=jnp.int32)[None, :],
                                 (offs + front)[None, :], 0), axis=1)
        dests.append(base + rank)
    tile_start = jnp.arange(P // tg, dtype=jnp.int32) * tg
    tile_cls = jnp.minimum(jnp.sum((tile_start[:, None] >= ends[None, :]).astype(jnp.int32), axis=1),
                           N_CLASSES - 1)
    rows = tg - jnp.clip(jnp.take(offs + front, tile_cls) - tile_start, 0, tg)
    halves = jnp.where(tile_start < ends[-1], (rows + FFN_HALF - 1) // FFN_HALF, 0).astype(jnp.int32)
    grp = tile_cls // N_PAIRS
    pair = tile_cls % N_PAIRS
    ea = grp * MOE_PER_GROUP + jnp.take(jnp.asarray(_PAIR_LO), pair)
    eb = grp * MOE_PER_GROUP + jnp.take(jnp.asarray(_PAIR_HI), pair)
    zinfo = jnp.concatenate([offs, offs + front, ends[-1:] // tg]).astype(jnp.int32)
    return dests, ea, eb, halves, zinfo


def _moe(xs_in, g, w_group, b_group, w_router, b_router, layer, wg, wu, wd, gfin, final_norm):
    assert len(xs_in) == 2
    D = xs_in[0].shape[1]
    T = sum(x.shape[0] for x in xs_in)
    wcat = jnp.zeros((D, LANES), F32)
    wcat = wcat.at[:, :MOE_GROUPS].set(w_group).at[:, MOE_GROUPS:MOE_GROUPS + MOE_EXPERTS].set(w_router)
    bias = jnp.zeros((1, LANES), F32)
    bias = bias.at[0, :MOE_GROUPS].set(b_group).at[0, MOE_GROUPS:MOE_GROUPS + MOE_EXPERTS].set(b_router)
    wc = wcat.astype(BF16)

    counts = jnp.zeros((1, LANES), F32)
    texts, metas = [], []
    for x in xs_in:
        t_ext, meta, counts = _router(x, g, wc, bias, counts)
        texts.append(t_ext)
        metas.append(meta)

    tg = FFN_TILE
    P = ((T + tg - 1) // tg + N_CLASSES) * tg
    dests, ea, eb, halves, zinfo = _moe_tables(metas, counts, P)
    tmove = _pick_tile(math.gcd(*(x.shape[0] for x in xs_in)), MOVE_TILE)
    dest3s = [dest.reshape(-1, 1, tmove) for dest in dests]
    xs = _dispatch(texts[0], texts[1], jnp.concatenate(dest3s, axis=0), zinfo, P)
    ys = _ffn(xs.reshape(P, D + EXTRA), ea, eb, halves, layer, wg, wu, wd)
    return [_combine(x, dest3, ys, gfin, final_norm) for x, dest3 in zip(xs_in, dest3s)]


def _rope_tables(pos, dk):
    half = dk // 2
    inv = 1.0 / (ROPE_BASE ** jnp.linspace(0.0, 1.0, half, dtype=F32))
    ang = pos.astype(F32)[:, None] * inv[None, :]
    return jnp.cos(ang), jnp.sin(ang)


def kernel(x_prompt, x_sample, state_ret, state_conv, state_lru, norm_mix_g, norm_ffn_g, norm_final_g, ret_w_in, ret_w_out, ret_norm_g, lru_w_in, lru_conv_w, lru_conv_b, lru_w_a, lru_b_a, lru_w_x, lru_b_x, lru_lambda, lru_w_out, moe_w_group, moe_b_group, moe_w_router, moe_b_router, moe_w_gate, moe_w_up, moe_w_down):
    B, S, D = x_prompt.shape
    Bs, Ls, _ = x_sample.shape
    Tp, Ts = B * S, Bs * Ls
    assert norm_mix_g.shape[0] == 2
    dk = D // RET_HEADS
    row = lambda v: v.reshape(1, -1)
    gfin = row(norm_final_g)

    cos_p, sin_p = _rope_tables(jnp.arange(S), dk)
    cos_s, sin_s = _rope_tables(PAST_LEN + jnp.arange(Ls), dk)
    ret_args = (row(norm_mix_g[0]), ret_w_in[0].astype(BF16), ret_w_out[0].astype(BF16),
                row(ret_norm_g[0]))
    xp, ret_p = _ret_prompt(x_prompt.reshape(Tp, D), B, S, *ret_args, cos_p, sin_p)
    xs, ret_s = _ret_sample(x_sample.reshape(Ts, D), Bs, Ls, *ret_args, cos_s, sin_s, state_ret[0])
    xp, xs = _moe([xp, xs], row(norm_ffn_g[0]), moe_w_group[0], moe_b_group[0], moe_w_router[0],
                  moe_b_router[0], 0, moe_w_gate, moe_w_up, moe_w_down, gfin, False)

    lru_args = (row(norm_mix_g[1]), lru_w_in[0].astype(BF16), lru_conv_w[0], row(lru_conv_b[0]),
                lru_w_a[0].astype(BF16), row(lru_b_a[0]), lru_w_x[0].astype(BF16), row(lru_b_x[0]),
                row(lru_lambda[0]), lru_w_out[0].astype(BF16))
    cs8 = jnp.pad(state_conv[0], ((0, 0), (8 - (CONV_W - 1), 0), (0, 0)))
    h0_rows = jnp.pad(state_lru[0].reshape(Bs, 1, D), ((0, 0), (0, Ls - 1), (0, 0)))
    xp, conv_p, lru_p = _lru_prompt(xp, B, S, *lru_args)
    xs, conv_s, lru_s = _lru_sample(xs, Bs, Ls, *lru_args, cs8, h0_rows)
    y_p, y_s = _moe([xp, xs], row(norm_ffn_g[1]), moe_w_group[1], moe_b_group[1], moe_w_router[1],
                    moe_b_router[1], 1, moe_w_gate, moe_w_up, moe_w_down, gfin, True)

    return (y_p.reshape(B, S, D), y_s.reshape(Bs, Ls, D),
            ret_p[None], conv_p[None, :, 8 - (CONV_W - 1):], lru_p[None, :, 7],
            ret_s[None], conv_s[None, :, 8 - (CONV_W - 1):], lru_s[None, :, 7])
```

```python
import functools
import math

import numpy as np
import jax
import jax.numpy as jnp
from jax import lax
from jax.experimental import pallas as pl
from jax.experimental.pallas import tpu as pltpu

F32 = jnp.float32
BF16 = jnp.bfloat16

RET_HEADS = 4
RET_CHUNK = 64
ROPE_BASE = 10000.0
CONV_W = 4
LRU_BLOCKS = 4
LRU_C = 8.0
MOE_GROUPS = 4
MOE_PER_GROUP = 4
MOE_EXPERTS = MOE_GROUPS * MOE_PER_GROUP
NORM_EPS = 1e-6
PAST_LEN = 2048

LANES = 128
SUBLANES = 8
VMEM_LIMIT_BYTES = 56 * 1024 * 1024

RET_TILE = 512
RET_SCORE_BLOCK = 256
RET_SAMPLE_STREAMS = 2
LRU_TILE = 512
ROUTE_TILE = 1024
MOVE_TILE = 512
COMBINE_TILE = 1024
FFN_TILE = 512
FFN_HALF = 256
N_PAIRS = 6
N_CLASSES = MOE_GROUPS * N_PAIRS
EXTRA = LANES
NEG_BIG = -1e30

_PAIR_LO = np.array([0, 0, 0, 1, 1, 2], np.int32)
_PAIR_HI = np.array([1, 2, 3, 2, 3, 3], np.int32)


def _log_gammas():
    return [math.log1p(-(2.0 ** (-5.0 - h))) for h in range(RET_HEADS)]


def _rms(x, g):
    return x * lax.rsqrt(jnp.mean(x * x, axis=-1, keepdims=True) + NORM_EPS) * g


def _dot(a, b):
    return jnp.dot(a, b, preferred_element_type=F32)


def _dot_nt(a, b):
    return lax.dot_general(a, b, (((1,), (1,)), ((), ())), preferred_element_type=F32)


def _dot_tn(a, b):
    return lax.dot_general(a, b, (((0,), (0,)), ((), ())), preferred_element_type=F32)


def _const_spec(shape):
    nd = len(shape)
    return pl.BlockSpec(shape, lambda *_: (0,) * nd, pipeline_mode=pl.Buffered(1))


def _params(sem):
    return pltpu.CompilerParams(dimension_semantics=sem, vmem_limit_bytes=VMEM_LIMIT_BYTES)


def _pick_tile(n, pref):
    t = pref
    while n % t:
        t //= 2
    assert t >= SUBLANES, (n, pref)
    return t


def _decay_matrix(L, chunk, lg):
    ii = lax.broadcasted_iota(jnp.int32, (L, L), 0)
    jj = lax.broadcasted_iota(jnp.int32, (L, L), 1)
    diff = (ii - jj).astype(F32)
    same_chunk = (ii // chunk) == (jj // chunk)
    return jnp.where(same_chunk & (ii >= jj), jnp.exp(jnp.maximum(diff, 0.0) * lg), 0.0)


def _rotary(t, cos, sin):
    half = t.shape[-1] // 2
    t1, t2 = t[:, :half], t[:, half:]
    return jnp.concatenate([t1 * cos - t2 * sin, t1 * sin + t2 * cos], axis=-1)


def _ret_proj(hn, win_ref, hd, dk, dv):
    qk, vv = RET_HEADS * dk, RET_HEADS * dv
    q = _dot(hn, win_ref[:, hd * dk:(hd + 1) * dk])
    k = _dot(hn, win_ref[:, qk + hd * dk:qk + (hd + 1) * dk])
    v = _dot(hn, win_ref[:, 2 * qk + hd * dv:2 * qk + (hd + 1) * dv])
    gt = _dot(hn, win_ref[:, 2 * qk + vv + hd * dv:2 * qk + vv + (hd + 1) * dv])
    return q, k, v, gt


def _ret_head(q, k, v, gt, cos, sin, decay, s0, ng, chunk, lg, kv_scr=None):
    L, dk = q.shape
    qr = _rotary(q, cos, sin)
    kr = _rotary(k, cos, sin) * (dk ** -0.5)
    idx = (lax.broadcasted_iota(jnp.int32, (L, 1), 0) % chunk).astype(F32)
    qd = (qr * jnp.exp((idx + 1.0) * lg)).astype(BF16)
    kd = (kr * jnp.exp((chunk - 1.0 - idx) * lg)).astype(BF16)
    qb, kb, vb = qr.astype(BF16), kr.astype(BF16), v.astype(BF16)
    core = decay.shape[0]
    chunks = [slice(c * chunk, (c + 1) * chunk) for c in range(L // chunk)]
    if kv_scr is not None:
        for c, rows in enumerate(chunks):
            kv_scr[c] = _dot_tn(kd[rows], vb[rows])
    s, parts = s0, []
    for r0 in range(0, L, core):
        blk = slice(r0, r0 + core)
        scores = _dot_nt(qb[blk], kb[blk]) * decay
        intra = _dot(scores.astype(BF16), vb[blk])
        for c in range(r0 // chunk, (r0 + core) // chunk):
            rows = chunks[c]
            parts.append(intra[rows.start - r0:rows.stop - r0] + _dot(qd[rows], s.astype(BF16)))
            kv = kv_scr[c] if kv_scr is not None else _dot_tn(kd[rows], vb[rows])
            s = math.exp(chunk * lg) * s + kv
    o = parts[0] if len(parts) == 1 else jnp.concatenate(parts, axis=0)
    o = o * lax.rsqrt(jnp.mean(o * o, axis=-1, keepdims=True) + NORM_EPS)
    og = (gt * jax.nn.sigmoid(gt)) * (o * ng)
    return og, s


def _ret_prompt_body(x_ref, g_ref, win_ref, wout_ref, ng_ref, cos_ref, sin_ref,
                     y_ref, s_ref, decay_scr, og_scr, kv_scr, *, L, chunk, dk, dv):
    b, c = pl.program_id(0), pl.program_id(1)
    lgs = _log_gammas()

    @pl.when((b == 0) & (c == 0))
    def _():
        for hd in range(RET_HEADS):
            decay_scr[hd] = _decay_matrix(decay_scr.shape[1], chunk, lgs[hd])

    @pl.when(c == 0)
    def _():
        s_ref[...] = jnp.zeros_like(s_ref)

    x = x_ref[...]
    hn = _rms(x, g_ref[...]).astype(BF16)
    cos, sin = cos_ref[...], sin_ref[...]
    for hd in range(RET_HEADS):
        og, s_new = _ret_head(*_ret_proj(hn, win_ref, hd, dk, dv), cos, sin, decay_scr[hd],
                              s_ref[0, hd], ng_ref[:, hd * dv:(hd + 1) * dv], chunk, lgs[hd],
                              kv_scr)
        s_ref[0, hd] = s_new
        og_scr[:, hd * dv:(hd + 1) * dv] = og.astype(BF16)
    y_ref[...] = x + _dot(og_scr[...], wout_ref[...])


def _ret_prompt(x_p, B, S, g, win, wout, ng, cos, sin):
    Tp, D = x_p.shape
    dk, dv = D // RET_HEADS, 2 * D // RET_HEADS
    chunk = min(S, RET_CHUNK)
    L = _pick_tile(S, RET_TILE)
    core = min(L, RET_SCORE_BLOCK)
    assert L % core == 0 and core % chunk == 0
    nc = S // L
    body = functools.partial(_ret_prompt_body, L=L, chunk=chunk, dk=dk, dv=dv)
    return pl.pallas_call(
        body,
        out_shape=(jax.ShapeDtypeStruct((Tp, D), F32),
                   jax.ShapeDtypeStruct((B, RET_HEADS, dk, dv), F32)),
        grid=(B, nc),
        in_specs=[
            pl.BlockSpec((L, D), lambda b, c: (b * nc + c, 0)),
            _const_spec((1, D)),
            _const_spec(win.shape),
            _const_spec(wout.shape),
            _const_spec((1, RET_HEADS * dv)),
            pl.BlockSpec((L, dk // 2), lambda b, c: (c, 0)),
            pl.BlockSpec((L, dk // 2), lambda b, c: (c, 0)),
        ],
        out_specs=(
            pl.BlockSpec((L, D), lambda b, c: (b * nc + c, 0)),
            pl.BlockSpec((1, RET_HEADS, dk, dv), lambda b, c: (b, 0, 0, 0)),
        ),
        scratch_shapes=[pltpu.VMEM((RET_HEADS, core, core), F32),
                        pltpu.VMEM((L, RET_HEADS * dv), BF16),
                        pltpu.VMEM((L // chunk, dk, dv), F32)],
        compiler_params=_params(("arbitrary", "arbitrary")),
        name="ret_prompt",
    )(x_p, g, win, wout, ng, cos, sin)


def _ret_sample_body(x_ref, g_ref, win_ref, wout_ref, ng_ref, cos_ref, sin_ref, s0_ref,
                     y_ref, s_ref, q_scr, k_scr, v_scr, gt_scr, og_scr, *, L, dk, dv):
    b, nb = pl.program_id(0), pl.num_programs(0)
    per = s0_ref.shape[0]
    lgs = _log_gammas()

    @pl.when(b == 0)
    def _():
        hn = _rms(x_ref[...], g_ref[...]).astype(BF16)
        for hd in range(RET_HEADS):
            q, k, v, gt = _ret_proj(hn, win_ref, hd, dk, dv)
            q_scr[hd], k_scr[hd], v_scr[hd], gt_scr[hd] = q, k, v, gt

    cos, sin = cos_ref[...], sin_ref[...]
    for j in range(per):
        rows = pl.ds(pl.multiple_of((b * per + j) * L, L), L)
        for hd in range(RET_HEADS):
            og, s_new = _ret_head(q_scr[hd, rows, :], k_scr[hd, rows, :], v_scr[hd, rows, :],
                                  gt_scr[hd, rows, :], cos, sin, _decay_matrix(L, L, lgs[hd]),
                                  s0_ref[j, hd], ng_ref[:, hd * dv:(hd + 1) * dv], L, lgs[hd])
            s_ref[j, hd] = s_new
            og_scr[rows, hd * dv:(hd + 1) * dv] = og.astype(BF16)

    @pl.when(b == nb - 1)
    def _():
        y_ref[...] = x_ref[...] + _dot(og_scr[...], wout_ref[...])


def _ret_sample(x_s, Bs, Ls, g, win, wout, ng, cos, sin, s0):
    Ts, D = x_s.shape
    dk, dv = D // RET_HEADS, 2 * D // RET_HEADS
    assert Ls <= RET_CHUNK
    per = RET_SAMPLE_STREAMS if Bs % RET_SAMPLE_STREAMS == 0 else 1
    body = functools.partial(_ret_sample_body, L=Ls, dk=dk, dv=dv)
    return pl.pallas_call(
        body,
        out_shape=(jax.ShapeDtypeStruct((Ts, D), F32),
                   jax.ShapeDtypeStruct((Bs, RET_HEADS, dk, dv), F32)),
        grid=(Bs // per,),
        in_specs=[
            _const_spec((Ts, D)),
            _const_spec((1, D)),
            _const_spec(win.shape),
            _const_spec(wout.shape),
            _const_spec((1, RET_HEADS * dv)),
            _const_spec((Ls, dk // 2)),
            _const_spec((Ls, dk // 2)),
            pl.BlockSpec((per, RET_HEADS, dk, dv), lambda b: (b, 0, 0, 0)),
        ],
        out_specs=(
            pl.BlockSpec((Ts, D), lambda b: (0, 0)),
            pl.BlockSpec((per, RET_HEADS, dk, dv), lambda b: (b, 0, 0, 0)),
        ),
        scratch_shapes=[pltpu.VMEM((RET_HEADS, Ts, dk), F32), pltpu.VMEM((RET_HEADS, Ts, dk), F32),
                        pltpu.VMEM((RET_HEADS, Ts, dv), F32), pltpu.VMEM((RET_HEADS, Ts, dv), F32),
                        pltpu.VMEM((Ts, RET_HEADS * dv), BF16)],
        compiler_params=_params(("arbitrary",)),
        name="ret_sample",
    )(x_s, g, win, wout, ng, cos, sin, s0)


def _gelu_tanh(x):
    return 0.5 * x * (1.0 + jnp.tanh(math.sqrt(2.0 / math.pi) * (x + 0.044715 * (x * x * x))))


def _softplus(z):
    return jnp.maximum(z, 0.0) + jnp.log1p(jnp.exp(-jnp.abs(z)))


def _lru_gates(conv, wa_ref, wx_ref, ba, bx, lam):
    D = conv.shape[1]
    bw = D // LRU_BLOCKS
    ra, ix = [], []
    for n in range(LRU_BLOCKS):
        cb = conv[:, n * bw:(n + 1) * bw].astype(BF16)
        ra.append(_dot(cb, wa_ref[n]))
        ix.append(_dot(cb, wx_ref[n]))
    r = jax.nn.sigmoid(jnp.concatenate(ra, axis=-1) + ba)
    i = jax.nn.sigmoid(jnp.concatenate(ix, axis=-1) + bx)
    log_a = r * ((-LRU_C) * _softplus(-lam))
    a = jnp.exp(log_a)
    m2 = jnp.tanh(-log_a) * (1.0 + a * a)
    mult = jnp.where(m2 > 0.0, m2 * lax.rsqrt(m2), 0.0)
    return a, mult, i * conv


def _seg_scan(a, b, seg):
    R = a.shape[0]
    row = lax.broadcasted_iota(jnp.int32, (R, 1), 0) % seg
    k = 1
    while k < seg:
        keep = row >= k
        a_sh = jnp.where(keep, pltpu.roll(a, k, axis=0), 1.0)
        b_sh = jnp.where(keep, pltpu.roll(b, k, axis=0), 0.0)
        b = a * b_sh + b
        a = a * a_sh
        k *= 2
    return b


def _carry_scan(a, b, h_prev):
    L, D = a.shape
    G = L // SUBLANES
    a3, b3 = a.reshape(G, SUBLANES, D), b.reshape(G, SUBLANES, D)
    sub = lax.broadcasted_iota(jnp.int32, (SUBLANES, D), 0)[None]
    k = 1
    while k < SUBLANES:
        keep = sub >= k
        a_sh = jnp.where(keep, pltpu.roll(a3, k, axis=1), 1.0)
        b_sh = jnp.where(keep, pltpu.roll(b3, k, axis=1), 0.0)
        b3 = a3 * b_sh + b3
        a3 = a3 * a_sh
        k *= 2
    carry, outs = h_prev, []
    for j in range(G):
        r = a3[j] * carry + b3[j]
        outs.append(r)
        carry = r[SUBLANES - 1:SUBLANES, :]
    return jnp.concatenate(outs, axis=0)


def _lru_prompt_body(x_ref, g_ref, win_ref, cw_ref, cb_ref, wa_ref, ba_ref, wx_ref, bx_ref,
                     lam_ref, wout_ref, y_ref, conv_ref, h_ref, xc_scr, hprev_scr, *, L):
    c = pl.program_id(1)
    D = x_ref.shape[1]

    @pl.when(c == 0)
    def _():
        xc_scr[...] = jnp.zeros_like(xc_scr)
        hprev_scr[...] = jnp.zeros_like(hprev_scr)

    x = x_ref[...]
    hn = _rms(x, g_ref[...]).astype(BF16)
    gate = _gelu_tanh(_dot(hn, win_ref[:, :D]))
    x_br = _dot(hn, win_ref[:, D:])
    prev8 = xc_scr[...]
    sub8 = lax.broadcasted_iota(jnp.int32, (SUBLANES, 1), 0)

    def shifted(k):
        rolled = pltpu.roll(x_br, k, axis=0)
        first = jnp.where(sub8 < k, pltpu.roll(prev8, k, axis=0), rolled[0:SUBLANES, :])
        return jnp.concatenate([first, rolled[SUBLANES:, :]], axis=0)

    conv = cb_ref[...] + shifted(CONV_W - 1) * cw_ref[0:1, :]
    for j in range(1, CONV_W - 1):
        conv = conv + shifted(CONV_W - 1 - j) * cw_ref[j:j + 1, :]
    conv = conv + x_br * cw_ref[CONV_W - 1:CONV_W, :]
    a, mult, gi = _lru_gates(conv, wa_ref, wx_ref, ba_ref[...], bx_ref[...], lam_ref[...])
    first = jnp.where((sub8 == 0) & (c == 0), 1.0, mult[0:SUBLANES, :])
    mult = jnp.concatenate([first, mult[SUBLANES:, :]], axis=0)
    hs = _carry_scan(a, mult * gi, hprev_scr[7:8, :])
    y_ref[...] = x + _dot((hs * gate).astype(BF16), wout_ref[...])
    hprev_scr[...] = hs[L - 8:L, :]
    h_ref[0] = hs[L - 8:L, :]
    conv_ref[0] = x_br[L - 8:L, :]
    xc_scr[...] = x_br[L - 8:L, :]


def _lru_prompt(x_p, B, S, g, win, cw, cb, wa, ba, wx, bx, lam, wout):
    Tp, D = x_p.shape
    L = _pick_tile(S, LRU_TILE)
    nc = S // L
    body = functools.partial(_lru_prompt_body, L=L)
    return pl.pallas_call(
        body,
        out_shape=(jax.ShapeDtypeStruct((Tp, D), F32),
                   jax.ShapeDtypeStruct((B, 8, D), F32),
                   jax.ShapeDtypeStruct((B, 8, D), F32)),
        grid=(B, nc),
        in_specs=[
            pl.BlockSpec((L, D), lambda b, c: (b * nc + c, 0)),
            _const_spec((1, D)), _const_spec(win.shape), _const_spec(cw.shape), _const_spec((1, D)),
            _const_spec(wa.shape), _const_spec((1, D)), _const_spec(wx.shape), _const_spec((1, D)),
            _const_spec((1, D)), _const_spec(wout.shape),
        ],
        out_specs=(
            pl.BlockSpec((L, D), lambda b, c: (b * nc + c, 0)),
            pl.BlockSpec((1, 8, D), lambda b, c: (b, 0, 0)),
            pl.BlockSpec((1, 8, D), lambda b, c: (b, 0, 0)),
        ),
        scratch_shapes=[pltpu.VMEM((8, D), F32), pltpu.VMEM((8, D), F32)],
        compiler_params=_params(("arbitrary", "arbitrary")),
        name="lru_prompt",
    )(x_p, g, win, cw, cb, wa, ba, wx, bx, lam, wout)


def _lru_sample_body(x_ref, g_ref, win_ref, cw_ref, cb_ref, wa_ref, ba_ref, wx_ref, bx_ref,
                     lam_ref, wout_ref, cs_ref, h0_ref, y_ref, conv_ref, h_ref,
                     xc_scr, *, Bs, Ls):
    Ts, D = x_ref.shape
    x = x_ref[...]
    hn = _rms(x, g_ref[...]).astype(BF16)
    gate = _gelu_tanh(_dot(hn, win_ref[:, :D]))
    x_br = _dot(hn, win_ref[:, D:])
    xc_scr[:, 0:8, :] = cs_ref[...]
    xc_scr[:, 8:8 + Ls, :] = x_br.reshape(Bs, Ls, D)
    conv = cb_ref[...] + xc_scr[:, 5:5 + Ls, :] * cw_ref[0:1, :]
    for j in range(1, CONV_W):
        conv = conv + xc_scr[:, 5 + j:5 + j + Ls, :] * cw_ref[j:j + 1, :]
    conv = conv.reshape(Ts, D)
    a, mult, gi = _lru_gates(conv, wa_ref, wx_ref, ba_ref[...], bx_ref[...], lam_ref[...])
    bvec = mult * gi + a * h0_ref[...].reshape(Ts, D)
    hs = _seg_scan(a, bvec, Ls)
    y_ref[...] = x + _dot((hs * gate).astype(BF16), wout_ref[...])
    h_ref[...] = hs.reshape(Bs, Ls, D)[:, Ls - 8:Ls, :]
    conv_ref[...] = xc_scr[:, Ls:Ls + 8, :]


def _lru_sample(x_s, Bs, Ls, g, win, cw, cb, wa, ba, wx, bx, lam, wout, cs, h0):
    Ts, D = x_s.shape
    assert Ls % 8 == 0
    body = functools.partial(_lru_sample_body, Bs=Bs, Ls=Ls)
    return pl.pallas_call(
        body,
        out_shape=(jax.ShapeDtypeStruct((Ts, D), F32),
                   jax.ShapeDtypeStruct((Bs, 8, D), F32),
                   jax.ShapeDtypeStruct((Bs, 8, D), F32)),
        grid=(1,),
        in_specs=[
            _const_spec((Ts, D)),
            _const_spec((1, D)), _const_spec(win.shape), _const_spec(cw.shape), _const_spec((1, D)),
            _const_spec(wa.shape), _const_spec((1, D)), _const_spec(wx.shape), _const_spec((1, D)),
            _const_spec((1, D)), _const_spec(wout.shape),
            _const_spec((Bs, 8, D)), _const_spec((Bs, Ls, D)),
        ],
        out_specs=(
            pl.BlockSpec((Ts, D), lambda i: (0, 0)),
            pl.BlockSpec((Bs, 8, D), lambda i: (0, 0, 0)),
            pl.BlockSpec((Bs, 8, D), lambda i: (0, 0, 0)),
        ),
        scratch_shapes=[pltpu.VMEM((Bs, Ls + 8, D), F32)],
        compiler_params=_params(("arbitrary",)),
        name="lru_sample",
    )(x_s, g, win, cw, cb, wa, ba, wx, bx, lam, wout, cs, h0)


def _router_body(x_ref, g_ref, wc_ref, bias_ref, cin_ref, text_ref, meta_ref, cnt_ref, run_scr,
                 tri_scr):
    i = pl.program_id(0)
    tm, D = x_ref.shape

    @pl.when(i == 0)
    def _():
        run_scr[...] = cin_ref[...]
        ri = lax.broadcasted_iota(jnp.int32, (tm, tm), 0)
        ci = lax.broadcasted_iota(jnp.int32, (tm, tm), 1)
        tri_scr[...] = (ci < ri).astype(BF16)

    t = _rms(x_ref[...], g_ref[...])
    logits = _dot(t.astype(BF16), wc_ref[...]) + bias_ref[...]
    lane_i = lax.broadcasted_iota(jnp.int32, (tm, LANES), 1)
    lane = lane_i.astype(F32)

    def first_argmax(v):
        m = jnp.max(v, axis=-1, keepdims=True)
        return m, jnp.min(jnp.where(v == m, lane, 1e6), axis=-1, keepdims=True)

    gmask = lane_i < MOE_GROUPS
    gl = jnp.where(gmask, logits, NEG_BIG)
    mg, g_idx = first_argmax(gl)
    p_g = 1.0 / jnp.sum(jnp.where(gmask, jnp.exp(gl - mg), 0.0), axis=-1, keepdims=True)
    e_lane = lane_i - MOE_GROUPS
    lane_grp = (e_lane >> 2).astype(F32)
    emask = (e_lane >= 0) & (e_lane < MOE_EXPERTS) & (lane_grp == g_idx)
    el = jnp.where(emask, logits, NEG_BIG)
    m1, i1 = first_argmax(el)
    m2, i2 = first_argmax(jnp.where(lane == i1, NEG_BIG, el))
    e21 = jnp.exp(m2 - m1)
    w1 = p_g / (1.0 + e21)
    w2 = p_g * e21 / (1.0 + e21)
    first_lower = i1 < i2
    j_lo = jnp.minimum(i1, i2) - (MOE_GROUPS + MOE_PER_GROUP * g_idx)
    j_hi = jnp.maximum(i1, i2) - (MOE_GROUPS + MOE_PER_GROUP * g_idx)
    pair = 0.5 * (j_lo * (7.0 - j_lo)) + (j_hi - j_lo - 1.0)
    cls = g_idx * N_PAIRS + pair
    w_lo = jnp.where(first_lower, w1, w2)
    w_hi = jnp.where(first_lower, w2, w1)

    onehot = (lane == cls).astype(F32)
    prefix = _dot(tri_scr[...], onehot.astype(BF16))
    rank = jnp.sum(onehot * (prefix + run_scr[...]), axis=-1, keepdims=True)
    run_scr[...] = run_scr[...] + jnp.sum(onehot, axis=0, keepdims=True)

    text_ref[:, :D] = t
    text_ref[:, D:] = jnp.where(lane_i == 0, w_lo, jnp.where(lane_i == 1, w_hi, 0.0))
    ones8 = jnp.ones((SUBLANES, LANES), BF16)
    ids8 = lax.broadcasted_iota(jnp.int32, (SUBLANES, LANES), 1).astype(F32).astype(BF16)
    d2 = jnp.floor(rank * (1.0 / 65536.0))
    rem = rank - 65536.0 * d2
    d1 = jnp.floor(rem * (1.0 / 256.0))
    d0 = rem - 256.0 * d1

    def as_row(col):
        return _dot_nt(ones8, jnp.where(lane_i == 0, col, 0.0).astype(BF16))

    cls_row = _dot_nt(ids8, onehot.astype(BF16))
    rank_row = as_row(d0) + 256.0 * as_row(d1) + 65536.0 * as_row(d2)
    sub = lax.broadcasted_iota(jnp.int32, (SUBLANES, tm), 0)
    meta_ref[...] = jnp.where(sub == 0, cls_row, jnp.where(sub == 1, rank_row, 0.0))
    cnt_ref[...] = run_scr[...]


def _router(x, g, wc, bias, cnt_in):
    T, D = x.shape
    tm = _pick_tile(T, ROUTE_TILE)
    return pl.pallas_call(
        _router_body,
        out_shape=(jax.ShapeDtypeStruct((T, D + EXTRA), F32),
                   jax.ShapeDtypeStruct((T // tm * SUBLANES, tm), F32),
                   jax.ShapeDtypeStruct((1, LANES), F32)),
        grid=(T // tm,),
        in_specs=[
            pl.BlockSpec((tm, D), lambda i: (i, 0)),
            _const_spec((1, D)), _const_spec(wc.shape), _const_spec((1, LANES)),
            _const_spec((1, LANES)),
        ],
        out_specs=(
            pl.BlockSpec((tm, D + EXTRA), lambda i: (i, 0)),
            pl.BlockSpec((SUBLANES, tm), lambda i: (i, 0)),
            pl.BlockSpec((1, LANES), lambda i: (0, 0)),
        ),
        scratch_shapes=[pltpu.VMEM((1, LANES), F32), pltpu.VMEM((tm, tm), BF16)],
        compiler_params=_params(("arbitrary",)),
        name="moe_router",
    )(x, g, wc, bias, cnt_in)


def _start_rows(n, idx_ref, row_copy):
    assert n % SUBLANES == 0

    def body(j, carry):
        for u in range(SUBLANES):
            pos = idx_ref[0, 0, j * SUBLANES + u]
            row_copy(j, u, pos >> 3, pos & (SUBLANES - 1)).start(priority=u % 2)
        return carry

    lax.fori_loop(0, n // SUBLANES, body, 0)


def _dispatch_body(zinfo_ref, dest_ref, tp_ref, ts_ref, xs_ref, zbuf, sem, zsem, *, nbp, steps):
    i = pl.program_id(0)
    zgroups = zbuf.shape[0]

    def zero_copies():
        out = []
        for c in range(N_CLASSES):
            r0, r1 = zinfo_ref[c], zinfo_ref[N_CLASSES + c]
            g0, g1 = r0 >> 3, r1 >> 3
            for k in range(SUBLANES - 1):
                out.append((k < r1 - g1 * SUBLANES, pltpu.make_async_copy(
                    zbuf.at[0, pl.ds(0, 1), :], xs_ref.at[g1, pl.ds(k, 1), :], zsem)))
            n = g1 - g0
            b = zgroups // 2
            while b >= 1:
                out.append(((n & b) != 0, pltpu.make_async_copy(
                    zbuf.at[pl.ds(0, b)], xs_ref.at[pl.ds(g0 + (n & -(2 * b)), b)], zsem)))
                b //= 2
        for k in range(N_CLASSES):
            step = zinfo_ref[2 * N_CLASSES] + k
            out.append((step < steps, pltpu.make_async_copy(
                zbuf, xs_ref.at[pl.ds(jnp.minimum(step, steps - 1) * zgroups, zgroups)], zsem)))
        return out

    @pl.when(i == 0)
    def _():
        zbuf[...] = jnp.zeros_like(zbuf)
        for pred, cp in zero_copies():
            @pl.when(pred)
            def _():
                cp.start()

    def scatter(t_ref):
        groups = t_ref.shape[0]

        def row_copy(j, u, hi, lo):
            return pltpu.make_async_copy(t_ref.at[j, pl.ds(u, 1), :],
                                         xs_ref.at[hi, pl.ds(lo, 1), :], sem)

        _start_rows(groups * SUBLANES, dest_ref, row_copy)
        pltpu.make_async_copy(t_ref, xs_ref.at[pl.ds(0, groups)], sem).wait()

    @pl.when(i < nbp)
    def _():
        scatter(tp_ref)

    @pl.when(i >= nbp)
    def _():
        scatter(ts_ref)

    @pl.when(i == 0)
    def _():
        for pred, cp in zero_copies():
            @pl.when(pred)
            def _():
                cp.wait()


def _dispatch(t_p, t_s, dest3, zinfo, P):
    nb, _, td = dest3.shape
    W = t_p.shape[1]
    assert t_p.shape[0] % td == 0 and t_s.shape[0] % td == 0 and FFN_TILE % SUBLANES == 0
    nbp, nbs = t_p.shape[0] // td, t_s.shape[0] // td
    assert nb == nbp + nbs
    zgroups = FFN_TILE // SUBLANES
    body = functools.partial(_dispatch_body, nbp=nbp, steps=P // FFN_TILE)
    grid_spec = pltpu.PrefetchScalarGridSpec(
        num_scalar_prefetch=1,
        grid=(nb,),
        in_specs=[
            pl.BlockSpec((1, 1, td), lambda i, z: (i, 0, 0), memory_space=pltpu.SMEM),
            pl.BlockSpec((td // SUBLANES, SUBLANES, W), lambda i, z: (jnp.minimum(i, nbp - 1), 0, 0)),
            pl.BlockSpec((td // SUBLANES, SUBLANES, W), lambda i, z: (jnp.maximum(i - nbp, 0), 0, 0)),
        ],
        out_specs=pl.BlockSpec(memory_space=pl.ANY),
        scratch_shapes=[pltpu.VMEM((zgroups, SUBLANES, W), F32),
                        pltpu.SemaphoreType.DMA(()), pltpu.SemaphoreType.DMA(())],
    )
    return pl.pallas_call(
        body,
        out_shape=jax.ShapeDtypeStruct((P // SUBLANES, SUBLANES, W), F32),
        grid_spec=grid_spec,
        compiler_params=_params(("arbitrary",)),
        name="moe_dispatch",
    )(zinfo, dest3, t_p.reshape(-1, SUBLANES, W), t_s.reshape(-1, SUBLANES, W))


def _ffn_body(ea_ref, eb_ref, halves_ref, xs_ref, wga_ref, wua_ref, wda_ref,
              wgb_ref, wub_ref, wdb_ref, ys_ref, wg_scr, wu_scr, wd_scr):
    i = pl.program_id(0)
    tg, D = ys_ref.shape
    nh = halves_ref[i]
    prev = jnp.maximum(i - 1, 0)
    new_class = (i == 0) | (ea_ref[i] != ea_ref[prev]) | (eb_ref[i] != eb_ref[prev])

    @pl.when((nh > 0) & new_class)
    def _():
        for slot, (wg, wu, wd) in enumerate(((wga_ref, wua_ref, wda_ref), (wgb_ref, wub_ref, wdb_ref))):
            wg_scr[slot] = wg[0, 0].astype(BF16)
            wu_scr[slot] = wu[0, 0].astype(BF16)
            wd_scr[slot] = wd[0, 0].astype(BF16)

    def experts(rows):
        x = xs_ref[rows, :D].astype(BF16)
        w_a = xs_ref[rows, D:D + 1]
        w_b = xs_ref[rows, D + 1:D + 2]

        def expert(slot):
            gt = _dot(x, wg_scr[slot])
            hdn = (gt * jax.nn.sigmoid(gt)) * _dot(x, wu_scr[slot])
            return _dot(hdn.astype(BF16), wd_scr[slot])

        return w_a * expert(0) + w_b * expert(1)

    @pl.when(nh == 2)
    def _():
        ys_ref[...] = experts(slice(0, tg))

    @pl.when(nh == 1)
    def _():
        ys_ref[:tg - FFN_HALF, :] = jnp.zeros((tg - FFN_HALF, D), F32)
        ys_ref[tg - FFN_HALF:, :] = experts(slice(tg - FFN_HALF, tg))

    @pl.when(nh == 0)
    def _():
        ys_ref[...] = jnp.zeros_like(ys_ref)


def _ffn(xs, ea, eb, halves, layer, wg, wu, wd):
    P, W = xs.shape
    D = W - EXTRA
    ff = wg.shape[3]
    tg = FFN_TILE
    wspec_a = lambda shape: pl.BlockSpec(shape, lambda i, ea, eb, nh: (layer, ea[i], 0, 0))
    wspec_b = lambda shape: pl.BlockSpec(shape, lambda i, ea, eb, nh: (layer, eb[i], 0, 0))
    grid_spec = pltpu.PrefetchScalarGridSpec(
        num_scalar_prefetch=3,
        grid=(P // tg,),
        in_specs=[
            pl.BlockSpec((tg, W), lambda i, ea, eb, nh: (i, 0)),
            wspec_a((1, 1, D, ff)), wspec_a((1, 1, D, ff)), wspec_a((1, 1, ff, D)),
            wspec_b((1, 1, D, ff)), wspec_b((1, 1, D, ff)), wspec_b((1, 1, ff, D)),
        ],
        out_specs=pl.BlockSpec((tg, D), lambda i, ea, eb, nh: (i, 0)),
        scratch_shapes=[pltpu.VMEM((2, D, ff), BF16), pltpu.VMEM((2, D, ff), BF16),
                        pltpu.VMEM((2, ff, D), BF16)],
    )
    return pl.pallas_call(
        _ffn_body,
        out_shape=jax.ShapeDtypeStruct((P, D), F32),
        grid_spec=grid_spec,
        compiler_params=_params(("arbitrary",)),
        name="moe_ffn",
    )(ea, eb, halves, xs, wg, wu, wd, wg, wu, wd)


def _combine_body(dest_ref, dnext_ref, x_ref, gfin_ref, ys_ref, o_ref, ybuf, sems, *, final_norm):
    i, n = pl.program_id(0), pl.num_programs(0)
    tc, D = x_ref.shape
    groups = tc // SUBLANES

    def gather(idx_ref, slot):
        def row_copy(j, u, hi, lo):
            return pltpu.make_async_copy(ys_ref.at[hi, pl.ds(lo, 1), :],
                                         ybuf.at[slot, j, pl.ds(u, 1), :], sems.at[slot])
        _start_rows(tc, idx_ref, row_copy)

    @pl.when(i == 0)
    def _():
        gather(dest_ref, 0)

    for slot in range(2):
        @pl.when((i + 1 < n) & ((i + 1) % 2 == slot))
        def _():
            gather(dnext_ref, slot)

    for slot in range(2):
        @pl.when(i % 2 == slot)
        def _():
            pltpu.make_async_copy(ys_ref.at[pl.ds(0, groups)], ybuf.at[slot], sems.at[slot]).wait()
            y = x_ref[...] + ybuf[slot].reshape(tc, D)
            if final_norm:
                y = _rms(y, gfin_ref[...])
            o_ref[...] = y


def _combine(x, dest3, ys, gfin, final_norm):
    T, D = x.shape
    nb, _, tc = dest3.shape
    body = functools.partial(_combine_body, final_norm=final_norm)
    return pl.pallas_call(
        body,
        out_shape=jax.ShapeDtypeStruct((T, D), F32),
        grid=(nb,),
        in_specs=[
            pl.BlockSpec((1, 1, tc), lambda i: (i, 0, 0), memory_space=pltpu.SMEM),
            pl.BlockSpec((1, 1, tc), lambda i: (jnp.minimum(i + 1, nb - 1), 0, 0),
                         memory_space=pltpu.SMEM),
            pl.BlockSpec((tc, D), lambda i: (i, 0)),
            _const_spec((1, D)),
            pl.BlockSpec(memory_space=pl.ANY),
        ],
        out_specs=pl.BlockSpec((tc, D), lambda i: (i, 0)),
        scratch_shapes=[pltpu.VMEM((2, tc // SUBLANES, SUBLANES, D), F32),
                        pltpu.SemaphoreType.DMA((2,))],
        compiler_params=_params(("arbitrary",)),
        name="moe_combine",
    )(dest3, dest3, x, gfin, ys.reshape(-1, SUBLANES, D))


def _moe_tables(metas, counts, P):
    tg = FFN_TILE
    cnt = counts[0, :N_CLASSES].astype(jnp.int32)
    padded = ((cnt + tg - 1) // tg) * tg
    ends = jnp.cumsum(padded)
    offs = ends - padded
    front = padded - cnt
    dests = []
    for m in metas:
        m = m.reshape(-1, SUBLANES, m.shape[1])
        cls, rank = m[:, 0, :].reshape(-1).astype(jnp.int32), m[:, 1, :].reshape(-1).astype(jnp.int32)
        base = jnp.sum(jnp.where(cls[:, None] == jnp.arange(N_CLASSES, dtype=jnp.int32)[None, :],
                                 (offs + front)[None, :], 0), axis=1)
        dests.append(base + rank)
    tile_start = jnp.arange(P // tg, dtype=jnp.int32) * tg
    tile_cls = jnp.minimum(jnp.sum((tile_start[:, None] >= ends[None, :]).astype(jnp.int32), axis=1),
                           N_CLASSES - 1)
    rows = tg - jnp.clip(jnp.take(offs + front, tile_cls) - tile_start, 0, tg)
    halves = jnp.where(tile_start < ends[-1], (rows + FFN_HALF - 1) // FFN_HALF, 0).astype(jnp.int32)
    grp = tile_cls // N_PAIRS
    pair = tile_cls % N_PAIRS
    ea = grp * MOE_PER_GROUP + jnp.take(jnp.asarray(_PAIR_LO), pair)
    eb = grp * MOE_PER_GROUP + jnp.take(jnp.asarray(_PAIR_HI), pair)
    zinfo = jnp.concatenate([offs, offs + front, ends[-1:] // tg]).astype(jnp.int32)
    return dests, ea, eb, halves, zinfo


def _moe(xs_in, g, w_group, b_group, w_router, b_router, layer, wg, wu, wd, gfin, final_norm):
    assert len(xs_in) == 2
    D = xs_in[0].shape[1]
    T = sum(x.shape[0] for x in xs_in)
    wcat = jnp.zeros((D, LANES), F32)
    wcat = wcat.at[:, :MOE_GROUPS].set(w_group).at[:, MOE_GROUPS:MOE_GROUPS + MOE_EXPERTS].set(w_router)
    bias = jnp.zeros((1, LANES), F32)
    bias = bias.at[0, :MOE_GROUPS].set(b_group).at[0, MOE_GROUPS:MOE_GROUPS + MOE_EXPERTS].set(b_router)
    wc = wcat.astype(BF16)

    counts = jnp.zeros((1, LANES), F32)
    texts, metas = [], []
    for x in xs_in:
        t_ext, meta, counts = _router(x, g, wc, bias, counts)
        texts.append(t_ext)
        metas.append(meta)

    tg = FFN_TILE
    P = ((T + tg - 1) // tg + N_CLASSES) * tg
    dests, ea, eb, halves, zinfo = _moe_tables(metas, counts, P)
    tmove = _pick_tile(math.gcd(*(x.shape[0] for x in xs_in)), MOVE_TILE)
    dest3 = jnp.concatenate(dests).reshape(-1, 1, tmove)
    xs = _dispatch(texts[0], texts[1], dest3, zinfo, P)
    ys = _ffn(xs.reshape(P, D + EXTRA), ea, eb, halves, layer, wg, wu, wd)
    return [_combine(x, dest.reshape(-1, 1, _pick_tile(x.shape[0], COMBINE_TILE)), ys, gfin, final_norm)
            for x, dest in zip(xs_in, dests)]


def _rope_tables(pos, dk):
    half = dk // 2
    inv = 1.0 / (ROPE_BASE ** jnp.linspace(0.0, 1.0, half, dtype=F32))
    ang = pos.astype(F32)[:, None] * inv[None, :]
    return jnp.cos(ang), jnp.sin(ang)


def kernel(x_prompt, x_sample, state_ret, state_conv, state_lru, norm_mix_g, norm_ffn_g, norm_final_g, ret_w_in, ret_w_out, ret_norm_g, lru_w_in, lru_conv_w, lru_conv_b, lru_w_a, lru_b_a, lru_w_x, lru_b_x, lru_lambda, lru_w_out, moe_w_group, moe_b_group, moe_w_router, moe_b_router, moe_w_gate, moe_w_up, moe_w_down):
    B, S, D = x_prompt.shape
    Bs, Ls, _ = x_sample.shape
    Tp, Ts = B * S, Bs * Ls
    assert norm_mix_g.shape[0] == 2
    dk = D // RET_HEADS
    row = lambda v: v.reshape(1, -1)
    gfin = row(norm_final_g)

    cos_p, sin_p = _rope_tables(jnp.arange(S), dk)
    cos_s, sin_s = _rope_tables(PAST_LEN + jnp.arange(Ls), dk)
    ret_args = (row(norm_mix_g[0]), ret_w_in[0].astype(BF16), ret_w_out[0].astype(BF16),
                row(ret_norm_g[0]))
    xp, ret_p = _ret_prompt(x_prompt.reshape(Tp, D), B, S, *ret_args, cos_p, sin_p)
    xs, ret_s = _ret_sample(x_sample.reshape(Ts, D), Bs, Ls, *ret_args, cos_s, sin_s, state_ret[0])
    xp, xs = _moe([xp, xs], row(norm_ffn_g[0]), moe_w_group[0], moe_b_group[0], moe_w_router[0],
                  moe_b_router[0], 0, moe_w_gate, moe_w_up, moe_w_down, gfin, False)

    lru_args = (row(norm_mix_g[1]), lru_w_in[0].astype(BF16), lru_conv_w[0], row(lru_conv_b[0]),
                lru_w_a[0].astype(BF16), row(lru_b_a[0]), lru_w_x[0].astype(BF16), row(lru_b_x[0]),
                row(lru_lambda[0]), lru_w_out[0].astype(BF16))
    cs8 = jnp.pad(state_conv[0], ((0, 0), (8 - (CONV_W - 1), 0), (0, 0)))
    h0_rows = jnp.pad(state_lru[0].reshape(Bs, 1, D), ((0, 0), (0, Ls - 1), (0, 0)))
    xp, conv_p, lru_p = _lru_prompt(xp, B, S, *lru_args)
    xs, conv_s, lru_s = _lru_sample(xs, Bs, Ls, *lru_args, cs8, h0_rows)
    y_p, y_s = _moe([xp, xs], row(norm_ffn_g[1]), moe_w_group[1], moe_b_group[1], moe_w_router[1],
                    moe_b_router[1], 1, moe_w_gate, moe_w_up, moe_w_down, gfin, True)

    return (y_p.reshape(B, S, D), y_s.reshape(Bs, Ls, D),
            ret_p[None], conv_p[None, :, 8 - (CONV_W - 1):], lru_p[None, :, 7],
            ret_s[None], conv_s[None, :, 8 - (CONV_W - 1):], lru_s[None, :, 7])
```

```python
import functools
import math

import numpy as np
import jax
import jax.numpy as jnp
from jax import lax
from jax.experimental import pallas as pl
from jax.experimental.pallas import tpu as pltpu

F32 = jnp.float32
BF16 = jnp.bfloat16

RET_HEADS = 4
RET_CHUNK = 64
ROPE_BASE = 10000.0
CONV_W = 4
LRU_BLOCKS = 4
LRU_C = 8.0
MOE_GROUPS = 4
MOE_PER_GROUP = 4
MOE_EXPERTS = MOE_GROUPS * MOE_PER_GROUP
NORM_EPS = 1e-6
PAST_LEN = 2048

LANES = 128
SUBLANES = 8
SUBLANE_SHIFT = 3
VMEM_LIMIT_BYTES = 56 * 1024 * 1024

RET_TILE = 512
RET_SCORE_BLOCK = 256
RET_SAMPLE_STREAMS = 2
LRU_TILE = 512
ROUTE_TILE = 1024
MOVE_TILE = 512
COMBINE_TILE = 1024
FFN_TILE = 512
FFN_HALF = 256
N_PAIRS = 6
N_CLASSES = MOE_GROUPS * N_PAIRS
EXTRA = LANES
NEG_BIG = -1e30

_PAIR_LO = np.array([0, 0, 0, 1, 1, 2], np.int32)
_PAIR_HI = np.array([1, 2, 3, 2, 3, 3], np.int32)


def _log_gammas():
    return [math.log1p(-(2.0 ** (-5.0 - h))) for h in range(RET_HEADS)]


def _rms(x, g):
    return x * lax.rsqrt(jnp.mean(x * x, axis=-1, keepdims=True) + NORM_EPS) * g


def _dot(a, b):
    return jnp.dot(a, b, preferred_element_type=F32)


def _dot_nt(a, b):
    return lax.dot_general(a, b, (((1,), (1,)), ((), ())), preferred_element_type=F32)


def _dot_tn(a, b):
    return lax.dot_general(a, b, (((0,), (0,)), ((), ())), preferred_element_type=F32)


def _const_spec(shape):
    nd = len(shape)
    return pl.BlockSpec(shape, lambda *_: (0,) * nd, pipeline_mode=pl.Buffered(1))


def _params(sem):
    return pltpu.CompilerParams(dimension_semantics=sem, vmem_limit_bytes=VMEM_LIMIT_BYTES)


def _pick_tile(n, pref):
    t = pref
    while n % t:
        t //= 2
    assert t >= SUBLANES, (n, pref)
    return t


def _decay_matrix(L, chunk, lg):
    ii = lax.broadcasted_iota(jnp.int32, (L, L), 0)
    jj = lax.broadcasted_iota(jnp.int32, (L, L), 1)
    diff = (ii - jj).astype(F32)
    same_chunk = (ii // chunk) == (jj // chunk)
    return jnp.where(same_chunk & (ii >= jj), jnp.exp(jnp.maximum(diff, 0.0) * lg), 0.0)


def _rotary(t, cos, sin):
    half = t.shape[-1] // 2
    t1, t2 = t[:, :half], t[:, half:]
    return jnp.concatenate([t1 * cos - t2 * sin, t1 * sin + t2 * cos], axis=-1)


def _ret_proj(hn, win_ref, hd, dk, dv):
    qk, vv = RET_HEADS * dk, RET_HEADS * dv
    q = _dot(hn, win_ref[:, hd * dk:(hd + 1) * dk])
    k = _dot(hn, win_ref[:, qk + hd * dk:qk + (hd + 1) * dk])
    v = _dot(hn, win_ref[:, 2 * qk + hd * dv:2 * qk + (hd + 1) * dv])
    gt = _dot(hn, win_ref[:, 2 * qk + vv + hd * dv:2 * qk + vv + (hd + 1) * dv])
    return q, k, v, gt


def _ret_head(q, k, v, gt, cos, sin, decay, s0, ng, chunk, lg, kv_scr=None):
    L, dk = q.shape
    qr = _rotary(q, cos, sin)
    kr = _rotary(k, cos, sin) * (dk ** -0.5)
    idx = (lax.broadcasted_iota(jnp.int32, (L, 1), 0) % chunk).astype(F32)
    qd = (qr * jnp.exp((idx + 1.0) * lg)).astype(BF16)
    kd = (kr * jnp.exp((chunk - 1.0 - idx) * lg)).astype(BF16)
    qb, kb, vb = qr.astype(BF16), kr.astype(BF16), v.astype(BF16)
    core = decay.shape[0]
    chunks = [slice(c * chunk, (c + 1) * chunk) for c in range(L // chunk)]
    if kv_scr is not None:
        for c, rows in enumerate(chunks):
            kv_scr[c] = _dot_tn(kd[rows], vb[rows])
    s, parts = s0, []
    for r0 in range(0, L, core):
        blk = slice(r0, r0 + core)
        scores = _dot_nt(qb[blk], kb[blk]) * decay
        intra = _dot(scores.astype(BF16), vb[blk])
        for c in range(r0 // chunk, (r0 + core) // chunk):
            rows = chunks[c]
            parts.append(intra[rows.start - r0:rows.stop - r0] + _dot(qd[rows], s.astype(BF16)))
            kv = kv_scr[c] if kv_scr is not None else _dot_tn(kd[rows], vb[rows])
            s = math.exp(chunk * lg) * s + kv
    o = parts[0] if len(parts) == 1 else jnp.concatenate(parts, axis=0)
    o = o * lax.rsqrt(jnp.mean(o * o, axis=-1, keepdims=True) + NORM_EPS)
    og = (gt * jax.nn.sigmoid(gt)) * (o * ng)
    return og, s


def _ret_prompt_body(x_ref, g_ref, win_ref, wout_ref, ng_ref, cos_ref, sin_ref,
                     y_ref, s_ref, decay_scr, og_scr, kv_scr, *, L, chunk, dk, dv):
    b, c = pl.program_id(0), pl.program_id(1)
    lgs = _log_gammas()

    @pl.when((b == 0) & (c == 0))
    def _():
        for hd in range(RET_HEADS):
            decay_scr[hd] = _decay_matrix(decay_scr.shape[1], chunk, lgs[hd])

    @pl.when(c == 0)
    def _():
        s_ref[...] = jnp.zeros_like(s_ref)

    x = x_ref[...]
    hn = _rms(x, g_ref[...]).astype(BF16)
    cos, sin = cos_ref[...], sin_ref[...]
    for hd in range(RET_HEADS):
        og, s_new = _ret_head(*_ret_proj(hn, win_ref, hd, dk, dv), cos, sin, decay_scr[hd],
                              s_ref[0, hd], ng_ref[:, hd * dv:(hd + 1) * dv], chunk, lgs[hd],
                              kv_scr)
        s_ref[0, hd] = s_new
        og_scr[:, hd * dv:(hd + 1) * dv] = og.astype(BF16)
    y_ref[...] = x + _dot(og_scr[...], wout_ref[...])


def _ret_prompt(x_p, B, S, g, win, wout, ng, cos, sin):
    Tp, D = x_p.shape
    dk, dv = D // RET_HEADS, 2 * D // RET_HEADS
    chunk = min(S, RET_CHUNK)
    L = _pick_tile(S, RET_TILE)
    core = min(L, RET_SCORE_BLOCK)
    assert L % core == 0 and core % chunk == 0
    nc = S // L
    body = functools.partial(_ret_prompt_body, L=L, chunk=chunk, dk=dk, dv=dv)
    return pl.pallas_call(
        body,
        out_shape=(jax.ShapeDtypeStruct((Tp, D), F32),
                   jax.ShapeDtypeStruct((B, RET_HEADS, dk, dv), F32)),
        grid=(B, nc),
        in_specs=[
            pl.BlockSpec((L, D), lambda b, c: (b * nc + c, 0)),
            _const_spec((1, D)),
            _const_spec(win.shape),
            _const_spec(wout.shape),
            _const_spec((1, RET_HEADS * dv)),
            pl.BlockSpec((L, dk // 2), lambda b, c: (c, 0)),
            pl.BlockSpec((L, dk // 2), lambda b, c: (c, 0)),
        ],
        out_specs=(
            pl.BlockSpec((L, D), lambda b, c: (b * nc + c, 0)),
            pl.BlockSpec((1, RET_HEADS, dk, dv), lambda b, c: (b, 0, 0, 0)),
        ),
        scratch_shapes=[pltpu.VMEM((RET_HEADS, core, core), F32),
                        pltpu.VMEM((L, RET_HEADS * dv), BF16),
                        pltpu.VMEM((L // chunk, dk, dv), F32)],
        compiler_params=_params(("arbitrary", "arbitrary")),
        name="ret_prompt",
    )(x_p, g, win, wout, ng, cos, sin)


def _ret_sample_body(x_ref, g_ref, win_ref, wout_ref, ng_ref, cos_ref, sin_ref, s0_ref,
                     y_ref, s_ref, q_scr, k_scr, v_scr, gt_scr, og_scr, *, L, dk, dv):
    b, nb = pl.program_id(0), pl.num_programs(0)
    per = s0_ref.shape[0]
    lgs = _log_gammas()

    @pl.when(b == 0)
    def _():
        hn = _rms(x_ref[...], g_ref[...]).astype(BF16)
        for hd in range(RET_HEADS):
            q, k, v, gt = _ret_proj(hn, win_ref, hd, dk, dv)
            q_scr[hd], k_scr[hd], v_scr[hd], gt_scr[hd] = q, k, v, gt

    cos, sin = cos_ref[...], sin_ref[...]
    for j in range(per):
        rows = pl.ds(pl.multiple_of((b * per + j) * L, L), L)
        for hd in range(RET_HEADS):
            og, s_new = _ret_head(q_scr[hd, rows, :], k_scr[hd, rows, :], v_scr[hd, rows, :],
                                  gt_scr[hd, rows, :], cos, sin, _decay_matrix(L, L, lgs[hd]),
                                  s0_ref[j, hd], ng_ref[:, hd * dv:(hd + 1) * dv], L, lgs[hd])
            s_ref[j, hd] = s_new
            og_scr[rows, hd * dv:(hd + 1) * dv] = og.astype(BF16)

    @pl.when(b == nb - 1)
    def _():
        y_ref[...] = x_ref[...] + _dot(og_scr[...], wout_ref[...])


def _ret_sample(x_s, Bs, Ls, g, win, wout, ng, cos, sin, s0):
    Ts, D = x_s.shape
    dk, dv = D // RET_HEADS, 2 * D // RET_HEADS
    assert Ls <= RET_CHUNK
    per = RET_SAMPLE_STREAMS if Bs % RET_SAMPLE_STREAMS == 0 else 1
    body = functools.partial(_ret_sample_body, L=Ls, dk=dk, dv=dv)
    return pl.pallas_call(
        body,
        out_shape=(jax.ShapeDtypeStruct((Ts, D), F32),
                   jax.ShapeDtypeStruct((Bs, RET_HEADS, dk, dv), F32)),
        grid=(Bs // per,),
        in_specs=[
            _const_spec((Ts, D)),
            _const_spec((1, D)),
            _const_spec(win.shape),
            _const_spec(wout.shape),
            _const_spec((1, RET_HEADS * dv)),
            _const_spec((Ls, dk // 2)),
            _const_spec((Ls, dk // 2)),
            pl.BlockSpec((per, RET_HEADS, dk, dv), lambda b: (b, 0, 0, 0)),
        ],
        out_specs=(
            pl.BlockSpec((Ts, D), lambda b: (0, 0)),
            pl.BlockSpec((per, RET_HEADS, dk, dv), lambda b: (b, 0, 0, 0)),
        ),
        scratch_shapes=[pltpu.VMEM((RET_HEADS, Ts, dk), F32), pltpu.VMEM((RET_HEADS, Ts, dk), F32),
                        pltpu.VMEM((RET_HEADS, Ts, dv), F32), pltpu.VMEM((RET_HEADS, Ts, dv), F32),
                        pltpu.VMEM((Ts, RET_HEADS * dv), BF16)],
        compiler_params=_params(("arbitrary",)),
        name="ret_sample",
    )(x_s, g, win, wout, ng, cos, sin, s0)


def _gelu_tanh(x):
    return 0.5 * x * (1.0 + jnp.tanh(math.sqrt(2.0 / math.pi) * (x + 0.044715 * (x * x * x))))


def _softplus(z):
    return jnp.maximum(z, 0.0) + jnp.log1p(jnp.exp(-jnp.abs(z)))


def _lru_gates(conv, wa_ref, wx_ref, ba, bx, lam):
    D = conv.shape[1]
    bw = D // LRU_BLOCKS
    ra, ix = [], []
    for n in range(LRU_BLOCKS):
        cb = conv[:, n * bw:(n + 1) * bw].astype(BF16)
        ra.append(_dot(cb, wa_ref[n]))
        ix.append(_dot(cb, wx_ref[n]))
    r = jax.nn.sigmoid(jnp.concatenate(ra, axis=-1) + ba)
    i = jax.nn.sigmoid(jnp.concatenate(ix, axis=-1) + bx)
    log_a = r * ((-LRU_C) * _softplus(-lam))
    a = jnp.exp(log_a)
    m2 = jnp.tanh(-log_a) * (1.0 + a * a)
    mult = jnp.where(m2 > 0.0, m2 * lax.rsqrt(m2), 0.0)
    return a, mult, i * conv


def _seg_scan(a, b, seg):
    R = a.shape[0]
    row = lax.broadcasted_iota(jnp.int32, (R, 1), 0) % seg
    k = 1
    while k < seg:
        keep = row >= k
        a_sh = jnp.where(keep, pltpu.roll(a, k, axis=0), 1.0)
        b_sh = jnp.where(keep, pltpu.roll(b, k, axis=0), 0.0)
        b = a * b_sh + b
        a = a * a_sh
        k *= 2
    return b


def _carry_scan(a, b, h_prev):
    L, D = a.shape
    G = L // SUBLANES
    a3, b3 = a.reshape(G, SUBLANES, D), b.reshape(G, SUBLANES, D)
    sub = lax.broadcasted_iota(jnp.int32, (SUBLANES, D), 0)[None]
    k = 1
    while k < SUBLANES:
        keep = sub >= k
        a_sh = jnp.where(keep, pltpu.roll(a3, k, axis=1), 1.0)
        b_sh = jnp.where(keep, pltpu.roll(b3, k, axis=1), 0.0)
        b3 = a3 * b_sh + b3
        a3 = a3 * a_sh
        k *= 2
    carry, outs = h_prev, []
    for j in range(G):
        r = a3[j] * carry + b3[j]
        outs.append(r)
        carry = r[SUBLANES - 1:SUBLANES, :]
    return jnp.concatenate(outs, axis=0)


def _lru_prompt_body(x_ref, g_ref, win_ref, cw_ref, cb_ref, wa_ref, ba_ref, wx_ref, bx_ref,
                     lam_ref, wout_ref, y_ref, conv_ref, h_ref, xc_scr, hprev_scr, *, L):
    c = pl.program_id(1)
    D = x_ref.shape[1]

    @pl.when(c == 0)
    def _():
        xc_scr[...] = jnp.zeros_like(xc_scr)
        hprev_scr[...] = jnp.zeros_like(hprev_scr)

    x = x_ref[...]
    hn = _rms(x, g_ref[...]).astype(BF16)
    gate = _gelu_tanh(_dot(hn, win_ref[:, :D]))
    x_br = _dot(hn, win_ref[:, D:])
    prev8 = xc_scr[...]
    sub8 = lax.broadcasted_iota(jnp.int32, (SUBLANES, 1), 0)

    def shifted(k):
        rolled = pltpu.roll(x_br, k, axis=0)
        first = jnp.where(sub8 < k, pltpu.roll(prev8, k, axis=0), rolled[0:SUBLANES, :])
        return jnp.concatenate([first, rolled[SUBLANES:, :]], axis=0)

    conv = cb_ref[...] + shifted(CONV_W - 1) * cw_ref[0:1, :]
    for j in range(1, CONV_W - 1):
        conv = conv + shifted(CONV_W - 1 - j) * cw_ref[j:j + 1, :]
    conv = conv + x_br * cw_ref[CONV_W - 1:CONV_W, :]
    a, mult, gi = _lru_gates(conv, wa_ref, wx_ref, ba_ref[...], bx_ref[...], lam_ref[...])
    first = jnp.where((sub8 == 0) & (c == 0), 1.0, mult[0:SUBLANES, :])
    mult = jnp.concatenate([first, mult[SUBLANES:, :]], axis=0)
    hs = _carry_scan(a, mult * gi, hprev_scr[SUBLANES - 1:SUBLANES, :])
    y_ref[...] = x + _dot((hs * gate).astype(BF16), wout_ref[...])
    hprev_scr[...] = hs[L - SUBLANES:L, :]
    h_ref[0] = hs[L - SUBLANES:L, :]
    conv_ref[0] = x_br[L - SUBLANES:L, :]
    xc_scr[...] = x_br[L - SUBLANES:L, :]


def _lru_prompt(x_p, B, S, g, win, cw, cb, wa, ba, wx, bx, lam, wout):
    Tp, D = x_p.shape
    L = _pick_tile(S, LRU_TILE)
    nc = S // L
    body = functools.partial(_lru_prompt_body, L=L)
    return pl.pallas_call(
        body,
        out_shape=(jax.ShapeDtypeStruct((Tp, D), F32),
                   jax.ShapeDtypeStruct((B, SUBLANES, D), F32),
                   jax.ShapeDtypeStruct((B, SUBLANES, D), F32)),
        grid=(B, nc),
        in_specs=[
            pl.BlockSpec((L, D), lambda b, c: (b * nc + c, 0)),
            _const_spec((1, D)), _const_spec(win.shape), _const_spec(cw.shape), _const_spec((1, D)),
            _const_spec(wa.shape), _const_spec((1, D)), _const_spec(wx.shape), _const_spec((1, D)),
            _const_spec((1, D)), _const_spec(wout.shape),
        ],
        out_specs=(
            pl.BlockSpec((L, D), lambda b, c: (b * nc + c, 0)),
            pl.BlockSpec((1, SUBLANES, D), lambda b, c: (b, 0, 0)),
            pl.BlockSpec((1, SUBLANES, D), lambda b, c: (b, 0, 0)),
        ),
        scratch_shapes=[pltpu.VMEM((SUBLANES, D), F32), pltpu.VMEM((SUBLANES, D), F32)],
        compiler_params=_params(("arbitrary", "arbitrary")),
        name="lru_prompt",
    )(x_p, g, win, cw, cb, wa, ba, wx, bx, lam, wout)


def _lru_sample_body(x_ref, g_ref, win_ref, cw_ref, cb_ref, wa_ref, ba_ref, wx_ref, bx_ref,
                     lam_ref, wout_ref, cs_ref, h0_ref, y_ref, conv_ref, h_ref,
                     xc_scr, *, Bs, Ls):
    Ts, D = x_ref.shape
    x = x_ref[...]
    hn = _rms(x, g_ref[...]).astype(BF16)
    gate = _gelu_tanh(_dot(hn, win_ref[:, :D]))
    x_br = _dot(hn, win_ref[:, D:])
    xc_scr[:, 0:SUBLANES, :] = cs_ref[...]
    xc_scr[:, SUBLANES:SUBLANES + Ls, :] = x_br.reshape(Bs, Ls, D)
    first = SUBLANES - (CONV_W - 1)
    conv = cb_ref[...] + xc_scr[:, first:first + Ls, :] * cw_ref[0:1, :]
    for j in range(1, CONV_W):
        conv = conv + xc_scr[:, first + j:first + j + Ls, :] * cw_ref[j:j + 1, :]
    conv = conv.reshape(Ts, D)
    a, mult, gi = _lru_gates(conv, wa_ref, wx_ref, ba_ref[...], bx_ref[...], lam_ref[...])
    bvec = mult * gi + a * h0_ref[...].reshape(Ts, D)
    hs = _seg_scan(a, bvec, Ls)
    y_ref[...] = x + _dot((hs * gate).astype(BF16), wout_ref[...])
    h_ref[...] = hs.reshape(Bs, Ls, D)[:, Ls - SUBLANES:Ls, :]
    conv_ref[...] = xc_scr[:, Ls:Ls + SUBLANES, :]


def _lru_sample(x_s, Bs, Ls, g, win, cw, cb, wa, ba, wx, bx, lam, wout, cs, h0):
    Ts, D = x_s.shape
    assert Ls % SUBLANES == 0
    body = functools.partial(_lru_sample_body, Bs=Bs, Ls=Ls)
    return pl.pallas_call(
        body,
        out_shape=(jax.ShapeDtypeStruct((Ts, D), F32),
                   jax.ShapeDtypeStruct((Bs, SUBLANES, D), F32),
                   jax.ShapeDtypeStruct((Bs, SUBLANES, D), F32)),
        grid=(1,),
        in_specs=[
            _const_spec((Ts, D)),
            _const_spec((1, D)), _const_spec(win.shape), _const_spec(cw.shape), _const_spec((1, D)),
            _const_spec(wa.shape), _const_spec((1, D)), _const_spec(wx.shape), _const_spec((1, D)),
            _const_spec((1, D)), _const_spec(wout.shape),
            _const_spec((Bs, SUBLANES, D)), _const_spec((Bs, Ls, D)),
        ],
        out_specs=(
            pl.BlockSpec((Ts, D), lambda i: (0, 0)),
            pl.BlockSpec((Bs, SUBLANES, D), lambda i: (0, 0, 0)),
            pl.BlockSpec((Bs, SUBLANES, D), lambda i: (0, 0, 0)),
        ),
        scratch_shapes=[pltpu.VMEM((Bs, Ls + SUBLANES, D), F32)],
        compiler_params=_params(("arbitrary",)),
        name="lru_sample",
    )(x_s, g, win, cw, cb, wa, ba, wx, bx, lam, wout, cs, h0)


def _router_body(x_ref, g_ref, wc_ref, bias_ref, cin_ref, text_ref, meta_ref, cnt_ref, run_scr,
                 tri_scr):
    i = pl.program_id(0)
    tm, D = x_ref.shape

    @pl.when(i == 0)
    def _():
        run_scr[...] = cin_ref[...]
        ri = lax.broadcasted_iota(jnp.int32, (tm, tm), 0)
        ci = lax.broadcasted_iota(jnp.int32, (tm, tm), 1)
        tri_scr[...] = (ci < ri).astype(BF16)

    t = _rms(x_ref[...], g_ref[...])
    logits = _dot(t.astype(BF16), wc_ref[...]) + bias_ref[...]
    lane_i = lax.broadcasted_iota(jnp.int32, (tm, LANES), 1)
    lane = lane_i.astype(F32)

    def first_argmax(v):
        m = jnp.max(v, axis=-1, keepdims=True)
        return m, jnp.min(jnp.where(v == m, lane, 1e6), axis=-1, keepdims=True)

    gmask = lane_i < MOE_GROUPS
    gl = jnp.where(gmask, logits, NEG_BIG)
    mg, g_idx = first_argmax(gl)
    p_g = 1.0 / jnp.sum(jnp.where(gmask, jnp.exp(gl - mg), 0.0), axis=-1, keepdims=True)
    e_lane = lane_i - MOE_GROUPS
    lane_grp = (e_lane >> 2).astype(F32)
    emask = (e_lane >= 0) & (e_lane < MOE_EXPERTS) & (lane_grp == g_idx)
    el = jnp.where(emask, logits, NEG_BIG)
    m1, i1 = first_argmax(el)
    m2, i2 = first_argmax(jnp.where(lane == i1, NEG_BIG, el))
    e21 = jnp.exp(m2 - m1)
    w1 = p_g / (1.0 + e21)
    w2 = p_g * e21 / (1.0 + e21)
    first_lower = i1 < i2
    j_lo = jnp.minimum(i1, i2) - (MOE_GROUPS + MOE_PER_GROUP * g_idx)
    j_hi = jnp.maximum(i1, i2) - (MOE_GROUPS + MOE_PER_GROUP * g_idx)
    pair = 0.5 * (j_lo * (7.0 - j_lo)) + (j_hi - j_lo - 1.0)
    cls = g_idx * N_PAIRS + pair
    w_lo = jnp.where(first_lower, w1, w2)
    w_hi = jnp.where(first_lower, w2, w1)

    onehot = (lane == cls).astype(F32)
    prefix = _dot(tri_scr[...], onehot.astype(BF16))
    rank = jnp.sum(onehot * (prefix + run_scr[...]), axis=-1, keepdims=True)
    run_scr[...] = run_scr[...] + jnp.sum(onehot, axis=0, keepdims=True)

    text_ref[:, :D] = t
    text_ref[:, D:] = jnp.where(lane_i == 0, w_lo, jnp.where(lane_i == 1, w_hi, 0.0))
    ones8 = jnp.ones((SUBLANES, LANES), BF16)
    ids8 = lax.broadcasted_iota(jnp.int32, (SUBLANES, LANES), 1).astype(F32).astype(BF16)
    d2 = jnp.floor(rank * (1.0 / 65536.0))
    rem = rank - 65536.0 * d2
    d1 = jnp.floor(rem * (1.0 / 256.0))
    d0 = rem - 256.0 * d1

    def as_row(col):
        return _dot_nt(ones8, jnp.where(lane_i == 0, col, 0.0).astype(BF16))

    cls_row = _dot_nt(ids8, onehot.astype(BF16))
    rank_row = as_row(d0) + 256.0 * as_row(d1) + 65536.0 * as_row(d2)
    sub = lax.broadcasted_iota(jnp.int32, (SUBLANES, tm), 0)
    meta_ref[...] = jnp.where(sub == 0, cls_row, jnp.where(sub == 1, rank_row, 0.0))
    cnt_ref[...] = run_scr[...]


def _router(x, g, wc, bias, cnt_in):
    T, D = x.shape
    tm = _pick_tile(T, ROUTE_TILE)
    return pl.pallas_call(
        _router_body,
        out_shape=(jax.ShapeDtypeStruct((T, D + EXTRA), F32),
                   jax.ShapeDtypeStruct((T // tm * SUBLANES, tm), F32),
                   jax.ShapeDtypeStruct((1, LANES), F32)),
        grid=(T // tm,),
        in_specs=[
            pl.BlockSpec((tm, D), lambda i: (i, 0)),
            _const_spec((1, D)), _const_spec(wc.shape), _const_spec((1, LANES)),
            _const_spec((1, LANES)),
        ],
        out_specs=(
            pl.BlockSpec((tm, D + EXTRA), lambda i: (i, 0)),
            pl.BlockSpec((SUBLANES, tm), lambda i: (i, 0)),
            pl.BlockSpec((1, LANES), lambda i: (0, 0)),
        ),
        scratch_shapes=[pltpu.VMEM((1, LANES), F32), pltpu.VMEM((tm, tm), BF16)],
        compiler_params=_params(("arbitrary",)),
        name="moe_router",
    )(x, g, wc, bias, cnt_in)


def _start_rows(n, idx_ref, row_copy):
    assert n % SUBLANES == 0

    def body(j, carry):
        for u in range(SUBLANES):
            pos = idx_ref[0, 0, j * SUBLANES + u]
            row_copy(j, u, pos >> SUBLANE_SHIFT, pos & (SUBLANES - 1)).start(priority=u % 2)
        return carry

    lax.fori_loop(0, n // SUBLANES, body, 0)


def _dispatch_body(zinfo_ref, dest_ref, tp_ref, ts_ref, xs_ref, zbuf, sem, zsem, *, nbp, steps):
    i = pl.program_id(0)
    zgroups = zbuf.shape[0]

    def zero_copies():
        out = []
        for c in range(N_CLASSES):
            r0, r1 = zinfo_ref[c], zinfo_ref[N_CLASSES + c]
            g0, g1 = r0 >> SUBLANE_SHIFT, r1 >> SUBLANE_SHIFT
            for k in range(SUBLANES - 1):
                out.append((k < r1 - g1 * SUBLANES, pltpu.make_async_copy(
                    zbuf.at[0, pl.ds(0, 1), :], xs_ref.at[g1, pl.ds(k, 1), :], zsem)))
            n = g1 - g0
            b = zgroups // 2
            while b >= 1:
                out.append(((n & b) != 0, pltpu.make_async_copy(
                    zbuf.at[pl.ds(0, b)], xs_ref.at[pl.ds(g0 + (n & -(2 * b)), b)], zsem)))
                b //= 2
        for k in range(N_CLASSES):
            step = zinfo_ref[2 * N_CLASSES] + k
            out.append((step < steps, pltpu.make_async_copy(
                zbuf, xs_ref.at[pl.ds(jnp.minimum(step, steps - 1) * zgroups, zgroups)], zsem)))
        return out

    @pl.when(i == 0)
    def _():
        zbuf[...] = jnp.zeros_like(zbuf)
        for pred, cp in zero_copies():
            @pl.when(pred)
            def _():
                cp.start()

    def scatter(t_ref):
        groups = t_ref.shape[0]

        def row_copy(j, u, hi, lo):
            return pltpu.make_async_copy(t_ref.at[j, pl.ds(u, 1), :],
                                         xs_ref.at[hi, pl.ds(lo, 1), :], sem)

        _start_rows(groups * SUBLANES, dest_ref, row_copy)
        pltpu.make_async_copy(t_ref, xs_ref.at[pl.ds(0, groups)], sem).wait()

    @pl.when(i < nbp)
    def _():
        scatter(tp_ref)

    @pl.when(i >= nbp)
    def _():
        scatter(ts_ref)

    @pl.when(i == 0)
    def _():
        for pred, cp in zero_copies():
            @pl.when(pred)
            def _():
                cp.wait()


def _dispatch(t_p, t_s, dest3, zinfo, P):
    nb, _, td = dest3.shape
    W = t_p.shape[1]
    assert t_p.shape[0] % td == 0 and t_s.shape[0] % td == 0 and FFN_TILE % SUBLANES == 0
    nbp, nbs = t_p.shape[0] // td, t_s.shape[0] // td
    assert nb == nbp + nbs
    zgroups = FFN_TILE // SUBLANES
    body = functools.partial(_dispatch_body, nbp=nbp, steps=P // FFN_TILE)
    grid_spec = pltpu.PrefetchScalarGridSpec(
        num_scalar_prefetch=1,
        grid=(nb,),
        in_specs=[
            pl.BlockSpec((1, 1, td), lambda i, z: (i, 0, 0), memory_space=pltpu.SMEM),
            pl.BlockSpec((td // SUBLANES, SUBLANES, W), lambda i, z: (jnp.minimum(i, nbp - 1), 0, 0)),
            pl.BlockSpec((td // SUBLANES, SUBLANES, W), lambda i, z: (jnp.maximum(i - nbp, 0), 0, 0)),
        ],
        out_specs=pl.BlockSpec(memory_space=pl.ANY),
        scratch_shapes=[pltpu.VMEM((zgroups, SUBLANES, W), F32),
                        pltpu.SemaphoreType.DMA(()), pltpu.SemaphoreType.DMA(())],
    )
    return pl.pallas_call(
        body,
        out_shape=jax.ShapeDtypeStruct((P // SUBLANES, SUBLANES, W), F32),
        grid_spec=grid_spec,
        compiler_params=_params(("arbitrary",)),
        name="moe_dispatch",
    )(zinfo, dest3, t_p.reshape(-1, SUBLANES, W), t_s.reshape(-1, SUBLANES, W))


def _ffn_body(ea_ref, eb_ref, halves_ref, xs_ref, wga_ref, wua_ref, wda_ref,
              wgb_ref, wub_ref, wdb_ref, ys_ref, wg_scr, wu_scr, wd_scr):
    i = pl.program_id(0)
    tg, D = ys_ref.shape
    nh = halves_ref[i]
    prev = jnp.maximum(i - 1, 0)
    new_class = (i == 0) | (ea_ref[i] != ea_ref[prev]) | (eb_ref[i] != eb_ref[prev])

    @pl.when((nh > 0) & new_class)
    def _():
        for slot, (wg, wu, wd) in enumerate(((wga_ref, wua_ref, wda_ref), (wgb_ref, wub_ref, wdb_ref))):
            wg_scr[slot] = wg[0, 0].astype(BF16)
            wu_scr[slot] = wu[0, 0].astype(BF16)
            wd_scr[slot] = wd[0, 0].astype(BF16)

    def experts(rows):
        x = xs_ref[rows, :D].astype(BF16)
        w_a = xs_ref[rows, D:D + 1]
        w_b = xs_ref[rows, D + 1:D + 2]

        def expert(slot):
            gt = _dot(x, wg_scr[slot])
            hdn = (gt * jax.nn.sigmoid(gt)) * _dot(x, wu_scr[slot])
            return _dot(hdn.astype(BF16), wd_scr[slot])

        return w_a * expert(0) + w_b * expert(1)

    @pl.when(nh == 2)
    def _():
        ys_ref[...] = experts(slice(0, tg))

    @pl.when(nh == 1)
    def _():
        ys_ref[:tg - FFN_HALF, :] = jnp.zeros((tg - FFN_HALF, D), F32)
        ys_ref[tg - FFN_HALF:, :] = experts(slice(tg - FFN_HALF, tg))

    @pl.when(nh == 0)
    def _():
        ys_ref[...] = jnp.zeros_like(ys_ref)


def _ffn(xs, ea, eb, halves, layer, wg, wu, wd):
    P, W = xs.shape
    D = W - EXTRA
    ff = wg.shape[3]
    tg = FFN_TILE
    wspec_a = lambda shape: pl.BlockSpec(shape, lambda i, ea, eb, nh: (layer, ea[i], 0, 0))
    wspec_b = lambda shape: pl.BlockSpec(shape, lambda i, ea, eb, nh: (layer, eb[i], 0, 0))
    grid_spec = pltpu.PrefetchScalarGridSpec(
        num_scalar_prefetch=3,
        grid=(P // tg,),
        in_specs=[
            pl.BlockSpec((tg, W), lambda i, ea, eb, nh: (i, 0)),
            wspec_a((1, 1, D, ff)), wspec_a((1, 1, D, ff)), wspec_a((1, 1, ff, D)),
            wspec_b((1, 1, D, ff)), wspec_b((1, 1, D, ff)), wspec_b((1, 1, ff, D)),
        ],
        out_specs=pl.BlockSpec((tg, D), lambda i, ea, eb, nh: (i, 0)),
        scratch_shapes=[pltpu.VMEM((2, D, ff), BF16), pltpu.VMEM((2, D, ff), BF16),
                        pltpu.VMEM((2, ff, D), BF16)],
    )
    return pl.pallas_call(
        _ffn_body,
        out_shape=jax.ShapeDtypeStruct((P, D), F32),
        grid_spec=grid_spec,
        compiler_params=_params(("arbitrary",)),
        name="moe_ffn",
    )(ea, eb, halves, xs, wg, wu, wd, wg, wu, wd)


def _combine_body(dest_ref, dnext_ref, x_ref, gfin_ref, ys_ref, o_ref, ybuf, sems, *, final_norm):
    i, n = pl.program_id(0), pl.num_programs(0)
    tc, D = x_ref.shape
    groups = tc // SUBLANES

    def gather(idx_ref, slot):
        def row_copy(j, u, hi, lo):
            return pltpu.make_async_copy(ys_ref.at[hi, pl.ds(lo, 1), :],
                                         ybuf.at[slot, j, pl.ds(u, 1), :], sems.at[slot])
        _start_rows(tc, idx_ref, row_copy)

    @pl.when(i == 0)
    def _():
        gather(dest_ref, 0)

    for slot in range(2):
        @pl.when((i + 1 < n) & ((i + 1) % 2 == slot))
        def _():
            gather(dnext_ref, slot)

    for slot in range(2):
        @pl.when(i % 2 == slot)
        def _():
            pltpu.make_async_copy(ys_ref.at[pl.ds(0, groups)], ybuf.at[slot], sems.at[slot]).wait()
            y = x_ref[...] + ybuf[slot].reshape(tc, D)
            if final_norm:
                y = _rms(y, gfin_ref[...])
            o_ref[...] = y


def _combine(x, dest3, ys, gfin, final_norm):
    T, D = x.shape
    nb, _, tc = dest3.shape
    body = functools.partial(_combine_body, final_norm=final_norm)
    return pl.pallas_call(
        body,
        out_shape=jax.ShapeDtypeStruct((T, D), F32),
        grid=(nb,),
        in_specs=[
            pl.BlockSpec((1, 1, tc), lambda i: (i, 0, 0), memory_space=pltpu.SMEM),
            pl.BlockSpec((1, 1, tc), lambda i: (jnp.minimum(i + 1, nb - 1), 0, 0),
                         memory_space=pltpu.SMEM),
            pl.BlockSpec((tc, D), lambda i: (i, 0)),
            _const_spec((1, D)),
            pl.BlockSpec(memory_space=pl.ANY),
        ],
        out_specs=pl.BlockSpec((tc, D), lambda i: (i, 0)),
        scratch_shapes=[pltpu.VMEM((2, tc // SUBLANES, SUBLANES, D), F32),
                        pltpu.SemaphoreType.DMA((2,))],
        compiler_params=_params(("arbitrary",)),
        name="moe_combine",
    )(dest3, dest3, x, gfin, ys.reshape(-1, SUBLANES, D))


def _moe_tables(metas, counts, P):
    tg = FFN_TILE
    cnt = counts[0, :N_CLASSES].astype(jnp.int32)
    padded = ((cnt + tg - 1) // tg) * tg
    ends = jnp.cumsum(padded)
    offs = ends - padded
    front = padded - cnt
    dests = []
    for m in metas:
        m = m.reshape(-1, SUBLANES, m.shape[1])
        cls, rank = m[:, 0, :].reshape(-1).astype(jnp.int32), m[:, 1, :].reshape(-1).astype(jnp.int32)
        base = jnp.sum(jnp.where(cls[:, None] == jnp.arange(N_CLASSES, dtype=jnp.int32)[None, :],
                                 (offs + front)[None, :], 0), axis=1)
        dests.append(base + rank)
    tile_start = jnp.arange(P // tg, dtype=jnp.int32) * tg
    tile_cls = jnp.minimum(jnp.sum((tile_start[:, None] >= ends[None, :]).astype(jnp.int32), axis=1),
                           N_CLASSES - 1)
    rows = tg - jnp.clip(jnp.take(offs + front, tile_cls) - tile_start, 0, tg)
    halves = jnp.where(tile_start < ends[-1], (rows + FFN_HALF - 1) // FFN_HALF, 0).astype(jnp.int32)
    grp = tile_cls // N_PAIRS
    pair = tile_cls % N_PAIRS
    ea = grp * MOE_PER_GROUP + jnp.take(jnp.asarray(_PAIR_LO), pair)
    eb = grp * MOE_PER_GROUP + jnp.take(jnp.asarray(_PAIR_HI), pair)
    zinfo = jnp.concatenate([offs, offs + front, ends[-1:] // tg]).astype(jnp.int32)
    return dests, ea, eb, halves, zinfo


def _moe(xs_in, g, w_group, b_group, w_router, b_router, layer, wg, wu, wd, gfin, final_norm):
    assert len(xs_in) == 2
    D = xs_in[0].shape[1]
    T = sum(x.shape[0] for x in xs_in)
    wcat = jnp.zeros((D, LANES), F32)
    wcat = wcat.at[:, :MOE_GROUPS].set(w_group).at[:, MOE_GROUPS:MOE_GROUPS + MOE_EXPERTS].set(w_router)
    bias = jnp.zeros((1, LANES), F32)
    bias = bias.at[0, :MOE_GROUPS].set(b_group).at[0, MOE_GROUPS:MOE_GROUPS + MOE_EXPERTS].set(b_router)
    wc = wcat.astype(BF16)

    counts = jnp.zeros((1, LANES), F32)
    texts, metas = [], []
    for x in xs_in:
        t_ext, meta, counts = _router(x, g, wc, bias, counts)
        texts.append(t_ext)
        metas.append(meta)

    tg = FFN_TILE
    P = ((T + tg - 1) // tg + N_CLASSES) * tg
    dests, ea, eb, halves, zinfo = _moe_tables(metas, counts, P)
    tmove = _pick_tile(math.gcd(*(x.shape[0] for x in xs_in)), MOVE_TILE)
    dest3 = jnp.concatenate(dests).reshape(-1, 1, tmove)
    xs = _dispatch(texts[0], texts[1], dest3, zinfo, P)
    ys = _ffn(xs.reshape(P, D + EXTRA), ea, eb, halves, layer, wg, wu, wd)
    return [_combine(x, dest.reshape(-1, 1, _pick_tile(x.shape[0], COMBINE_TILE)), ys, gfin, final_norm)
            for x, dest in zip(xs_in, dests)]


def _rope_tables(pos, dk):
    half = dk // 2
    inv = 1.0 / (ROPE_BASE ** jnp.linspace(0.0, 1.0, half, dtype=F32))
    ang = pos.astype(F32)[:, None] * inv[None, :]
    return jnp.cos(ang), jnp.sin(ang)


def kernel(x_prompt, x_sample, state_ret, state_conv, state_lru, norm_mix_g, norm_ffn_g, norm_final_g, ret_w_in, ret_w_out, ret_norm_g, lru_w_in, lru_conv_w, lru_conv_b, lru_w_a, lru_b_a, lru_w_x, lru_b_x, lru_lambda, lru_w_out, moe_w_group, moe_b_group, moe_w_router, moe_b_router, moe_w_gate, moe_w_up, moe_w_down):
    B, S, D = x_prompt.shape
    Bs, Ls, _ = x_sample.shape
    Tp, Ts = B * S, Bs * Ls
    assert norm_mix_g.shape[0] == 2
    dk = D // RET_HEADS
    row = lambda v: v.reshape(1, -1)
    gfin = row(norm_final_g)

    cos_p, sin_p = _rope_tables(jnp.arange(S), dk)
    cos_s, sin_s = _rope_tables(PAST_LEN + jnp.arange(Ls), dk)
    ret_args = (row(norm_mix_g[0]), ret_w_in[0].astype(BF16), ret_w_out[0].astype(BF16),
                row(ret_norm_g[0]))
    xp, ret_p = _ret_prompt(x_prompt.reshape(Tp, D), B, S, *ret_args, cos_p, sin_p)
    xs, ret_s = _ret_sample(x_sample.reshape(Ts, D), Bs, Ls, *ret_args, cos_s, sin_s, state_ret[0])
    xp, xs = _moe([xp, xs], row(norm_ffn_g[0]), moe_w_group[0], moe_b_group[0], moe_w_router[0],
                  moe_b_router[0], 0, moe_w_gate, moe_w_up, moe_w_down, gfin, False)

    lru_args = (row(norm_mix_g[1]), lru_w_in[0].astype(BF16), lru_conv_w[0], row(lru_conv_b[0]),
                lru_w_a[0].astype(BF16), row(lru_b_a[0]), lru_w_x[0].astype(BF16), row(lru_b_x[0]),
                row(lru_lambda[0]), lru_w_out[0].astype(BF16))
    cs8 = jnp.pad(state_conv[0], ((0, 0), (SUBLANES - (CONV_W - 1), 0), (0, 0)))
    h0_rows = jnp.pad(state_lru[0].reshape(Bs, 1, D), ((0, 0), (0, Ls - 1), (0, 0)))
    xp, conv_p, lru_p = _lru_prompt(xp, B, S, *lru_args)
    xs, conv_s, lru_s = _lru_sample(xs, Bs, Ls, *lru_args, cs8, h0_rows)
    y_p, y_s = _moe([xp, xs], row(norm_ffn_g[1]), moe_w_group[1], moe_b_group[1], moe_w_router[1],
                    moe_b_router[1], 1, moe_w_gate, moe_w_up, moe_w_down, gfin, True)

    return (y_p.reshape(B, S, D), y_s.reshape(Bs, Ls, D),
            ret_p[None], conv_p[None, :, SUBLANES - (CONV_W - 1):], lru_p[None, :, SUBLANES - 1],
            ret_s[None], conv_s[None, :, SUBLANES - (CONV_W - 1):], lru_s[None, :, SUBLANES - 1])
```

```python
import functools
import math

import numpy as np
import jax
import jax.numpy as jnp
from jax import lax
from jax.experimental import pallas as pl
from jax.experimental.pallas import tpu as pltpu

F32 = jnp.float32
BF16 = jnp.bfloat16

RET_HEADS = 4
RET_CHUNK = 64
ROPE_BASE = 10000.0
CONV_W = 4
LRU_BLOCKS = 4
LRU_C = 8.0
MOE_GROUPS = 4
MOE_PER_GROUP = 4
MOE_EXPERTS = MOE_GROUPS * MOE_PER_GROUP
NORM_EPS = 1e-6
PAST_LEN = 2048

LANES = 128
SUBLANES = 8
SUBLANE_SHIFT = 3
VMEM_LIMIT_BYTES = 56 * 1024 * 1024

RET_TILE = 512
RET_SCORE_BLOCK = 256
RET_SAMPLE_STREAMS = 2
LRU_TILE = 512
ROUTE_TILE = 1024
MOVE_TILE = 512
DISPATCH_SLOTS = 3
COMBINE_TILE = 1024
FFN_TILE = 512
FFN_HALF = 256
N_PAIRS = 6
N_CLASSES = MOE_GROUPS * N_PAIRS
EXTRA = LANES
NEG_BIG = -1e30

_PAIR_LO = np.array([0, 0, 0, 1, 1, 2], np.int32)
_PAIR_HI = np.array([1, 2, 3, 2, 3, 3], np.int32)


def _log_gammas():
    return [math.log1p(-(2.0 ** (-5.0 - h))) for h in range(RET_HEADS)]


def _rms(x, g):
    return x * lax.rsqrt(jnp.mean(x * x, axis=-1, keepdims=True) + NORM_EPS) * g


def _dot(a, b):
    return jnp.dot(a, b, preferred_element_type=F32)


def _dot_nt(a, b):
    return lax.dot_general(a, b, (((1,), (1,)), ((), ())), preferred_element_type=F32)


def _dot_tn(a, b):
    return lax.dot_general(a, b, (((0,), (0,)), ((), ())), preferred_element_type=F32)


def _const_spec(shape):
    nd = len(shape)
    return pl.BlockSpec(shape, lambda *_: (0,) * nd, pipeline_mode=pl.Buffered(1))


def _params(sem):
    return pltpu.CompilerParams(dimension_semantics=sem, vmem_limit_bytes=VMEM_LIMIT_BYTES)


def _pick_tile(n, pref):
    t = pref
    while n % t:
        t //= 2
    assert t >= SUBLANES, (n, pref)
    return t


def _decay_matrix(L, chunk, lg):
    ii = lax.broadcasted_iota(jnp.int32, (L, L), 0)
    jj = lax.broadcasted_iota(jnp.int32, (L, L), 1)
    diff = (ii - jj).astype(F32)
    same_chunk = (ii // chunk) == (jj // chunk)
    return jnp.where(same_chunk & (ii >= jj), jnp.exp(jnp.maximum(diff, 0.0) * lg), 0.0)


def _rotary(t, cos, sin):
    half = t.shape[-1] // 2
    t1, t2 = t[:, :half], t[:, half:]
    return jnp.concatenate([t1 * cos - t2 * sin, t1 * sin + t2 * cos], axis=-1)


def _ret_proj(hn, win_ref, hd, dk, dv):
    qk, vv = RET_HEADS * dk, RET_HEADS * dv
    q = _dot(hn, win_ref[:, hd * dk:(hd + 1) * dk])
    k = _dot(hn, win_ref[:, qk + hd * dk:qk + (hd + 1) * dk])
    v = _dot(hn, win_ref[:, 2 * qk + hd * dv:2 * qk + (hd + 1) * dv])
    gt = _dot(hn, win_ref[:, 2 * qk + vv + hd * dv:2 * qk + vv + (hd + 1) * dv])
    return q, k, v, gt


def _ret_head(q, k, v, gt, cos, sin, decay, s0, ng, chunk, lg, kv_scr=None):
    L, dk = q.shape
    qr = _rotary(q, cos, sin)
    kr = _rotary(k, cos, sin) * (dk ** -0.5)
    idx = (lax.broadcasted_iota(jnp.int32, (L, 1), 0) % chunk).astype(F32)
    qd = (qr * jnp.exp((idx + 1.0) * lg)).astype(BF16)
    kd = (kr * jnp.exp((chunk - 1.0 - idx) * lg)).astype(BF16)
    qb, kb, vb = qr.astype(BF16), kr.astype(BF16), v.astype(BF16)
    core = decay.shape[0]
    chunks = [slice(c * chunk, (c + 1) * chunk) for c in range(L // chunk)]
    if kv_scr is not None:
        for c, rows in enumerate(chunks):
            kv_scr[c] = _dot_tn(kd[rows], vb[rows])
    s, parts = s0, []
    for r0 in range(0, L, core):
        blk = slice(r0, r0 + core)
        scores = _dot_nt(qb[blk], kb[blk]) * decay
        intra = _dot(scores.astype(BF16), vb[blk])
        for c in range(r0 // chunk, (r0 + core) // chunk):
            rows = chunks[c]
            parts.append(intra[rows.start - r0:rows.stop - r0] + _dot(qd[rows], s.astype(BF16)))
            kv = kv_scr[c] if kv_scr is not None else _dot_tn(kd[rows], vb[rows])
            s = math.exp(chunk * lg) * s + kv
    o = parts[0] if len(parts) == 1 else jnp.concatenate(parts, axis=0)
    o = o * lax.rsqrt(jnp.mean(o * o, axis=-1, keepdims=True) + NORM_EPS)
    og = (gt * jax.nn.sigmoid(gt)) * (o * ng)
    return og, s


def _ret_prompt_body(x_ref, g_ref, win_ref, wout_ref, ng_ref, cos_ref, sin_ref,
                     y_ref, s_ref, decay_scr, og_scr, kv_scr, *, L, chunk, dk, dv):
    b, c = pl.program_id(0), pl.program_id(1)
    lgs = _log_gammas()

    @pl.when((b == 0) & (c == 0))
    def _():
        for hd in range(RET_HEADS):
            decay_scr[hd] = _decay_matrix(decay_scr.shape[1], chunk, lgs[hd])

    @pl.when(c == 0)
    def _():
        s_ref[...] = jnp.zeros_like(s_ref)

    x = x_ref[...]
    hn = _rms(x, g_ref[...]).astype(BF16)
    cos, sin = cos_ref[...], sin_ref[...]
    for hd in range(RET_HEADS):
        og, s_new = _ret_head(*_ret_proj(hn, win_ref, hd, dk, dv), cos, sin, decay_scr[hd],
                              s_ref[0, hd], ng_ref[:, hd * dv:(hd + 1) * dv], chunk, lgs[hd],
                              kv_scr)
        s_ref[0, hd] = s_new
        og_scr[:, hd * dv:(hd + 1) * dv] = og.astype(BF16)
    y_ref[...] = x + _dot(og_scr[...], wout_ref[...])


def _ret_prompt(x_p, B, S, g, win, wout, ng, cos, sin):
    Tp, D = x_p.shape
    dk, dv = D // RET_HEADS, 2 * D // RET_HEADS
    chunk = min(S, RET_CHUNK)
    L = _pick_tile(S, RET_TILE)
    core = min(L, RET_SCORE_BLOCK)
    assert L % core == 0 and core % chunk == 0
    nc = S // L
    body = functools.partial(_ret_prompt_body, L=L, chunk=chunk, dk=dk, dv=dv)
    return pl.pallas_call(
        body,
        out_shape=(jax.ShapeDtypeStruct((Tp, D), F32),
                   jax.ShapeDtypeStruct((B, RET_HEADS, dk, dv), F32)),
        grid=(B, nc),
        in_specs=[
            pl.BlockSpec((L, D), lambda b, c: (b * nc + c, 0)),
            _const_spec((1, D)),
            _const_spec(win.shape),
            _const_spec(wout.shape),
            _const_spec((1, RET_HEADS * dv)),
            pl.BlockSpec((L, dk // 2), lambda b, c: (c, 0)),
            pl.BlockSpec((L, dk // 2), lambda b, c: (c, 0)),
        ],
        out_specs=(
            pl.BlockSpec((L, D), lambda b, c: (b * nc + c, 0)),
            pl.BlockSpec((1, RET_HEADS, dk, dv), lambda b, c: (b, 0, 0, 0)),
        ),
        scratch_shapes=[pltpu.VMEM((RET_HEADS, core, core), F32),
                        pltpu.VMEM((L, RET_HEADS * dv), BF16),
                        pltpu.VMEM((L // chunk, dk, dv), F32)],
        compiler_params=_params(("arbitrary", "arbitrary")),
        name="ret_prompt",
    )(x_p, g, win, wout, ng, cos, sin)


def _ret_sample_body(x_ref, g_ref, win_ref, wout_ref, ng_ref, cos_ref, sin_ref, s0_ref,
                     y_ref, s_ref, q_scr, k_scr, v_scr, gt_scr, og_scr, *, L, dk, dv):
    b, nb = pl.program_id(0), pl.num_programs(0)
    per = s0_ref.shape[0]
    lgs = _log_gammas()

    @pl.when(b == 0)
    def _():
        hn = _rms(x_ref[...], g_ref[...]).astype(BF16)
        for hd in range(RET_HEADS):
            q, k, v, gt = _ret_proj(hn, win_ref, hd, dk, dv)
            q_scr[hd], k_scr[hd], v_scr[hd], gt_scr[hd] = q, k, v, gt

    cos, sin = cos_ref[...], sin_ref[...]
    for j in range(per):
        rows = pl.ds(pl.multiple_of((b * per + j) * L, L), L)
        for hd in range(RET_HEADS):
            og, s_new = _ret_head(q_scr[hd, rows, :], k_scr[hd, rows, :], v_scr[hd, rows, :],
                                  gt_scr[hd, rows, :], cos, sin, _decay_matrix(L, L, lgs[hd]),
                                  s0_ref[j, hd], ng_ref[:, hd * dv:(hd + 1) * dv], L, lgs[hd])
            s_ref[j, hd] = s_new
            og_scr[rows, hd * dv:(hd + 1) * dv] = og.astype(BF16)

    @pl.when(b == nb - 1)
    def _():
        y_ref[...] = x_ref[...] + _dot(og_scr[...], wout_ref[...])


def _ret_sample(x_s, Bs, Ls, g, win, wout, ng, cos, sin, s0):
    Ts, D = x_s.shape
    dk, dv = D // RET_HEADS, 2 * D // RET_HEADS
    assert Ls <= RET_CHUNK
    per = RET_SAMPLE_STREAMS if Bs % RET_SAMPLE_STREAMS == 0 else 1
    body = functools.partial(_ret_sample_body, L=Ls, dk=dk, dv=dv)
    return pl.pallas_call(
        body,
        out_shape=(jax.ShapeDtypeStruct((Ts, D), F32),
                   jax.ShapeDtypeStruct((Bs, RET_HEADS, dk, dv), F32)),
        grid=(Bs // per,),
        in_specs=[
            _const_spec((Ts, D)),
            _const_spec((1, D)),
            _const_spec(win.shape),
            _const_spec(wout.shape),
            _const_spec((1, RET_HEADS * dv)),
            _const_spec((Ls, dk // 2)),
            _const_spec((Ls, dk // 2)),
            pl.BlockSpec((per, RET_HEADS, dk, dv), lambda b: (b, 0, 0, 0)),
        ],
        out_specs=(
            pl.BlockSpec((Ts, D), lambda b: (0, 0)),
            pl.BlockSpec((per, RET_HEADS, dk, dv), lambda b: (b, 0, 0, 0)),
        ),
        scratch_shapes=[pltpu.VMEM((RET_HEADS, Ts, dk), F32), pltpu.VMEM((RET_HEADS, Ts, dk), F32),
                        pltpu.VMEM((RET_HEADS, Ts, dv), F32), pltpu.VMEM((RET_HEADS, Ts, dv), F32),
                        pltpu.VMEM((Ts, RET_HEADS * dv), BF16)],
        compiler_params=_params(("arbitrary",)),
        name="ret_sample",
    )(x_s, g, win, wout, ng, cos, sin, s0)


def _gelu_tanh(x):
    return 0.5 * x * (1.0 + jnp.tanh(math.sqrt(2.0 / math.pi) * (x + 0.044715 * (x * x * x))))


def _softplus(z):
    return jnp.maximum(z, 0.0) + jnp.log1p(jnp.exp(-jnp.abs(z)))


def _lru_gates(conv, wa_ref, wx_ref, ba, bx, lam):
    D = conv.shape[1]
    bw = D // LRU_BLOCKS
    ra, ix = [], []
    for n in range(LRU_BLOCKS):
        cb = conv[:, n * bw:(n + 1) * bw].astype(BF16)
        ra.append(_dot(cb, wa_ref[n]))
        ix.append(_dot(cb, wx_ref[n]))
    r = jax.nn.sigmoid(jnp.concatenate(ra, axis=-1) + ba)
    i = jax.nn.sigmoid(jnp.concatenate(ix, axis=-1) + bx)
    log_a = r * ((-LRU_C) * _softplus(-lam))
    a = jnp.exp(log_a)
    m2 = jnp.tanh(-log_a) * (1.0 + a * a)
    mult = jnp.where(m2 > 0.0, m2 * lax.rsqrt(m2), 0.0)
    return a, mult, i * conv


def _seg_scan(a, b, seg):
    R = a.shape[0]
    row = lax.broadcasted_iota(jnp.int32, (R, 1), 0) % seg
    k = 1
    while k < seg:
        keep = row >= k
        a_sh = jnp.where(keep, pltpu.roll(a, k, axis=0), 1.0)
        b_sh = jnp.where(keep, pltpu.roll(b, k, axis=0), 0.0)
        b = a * b_sh + b
        a = a * a_sh
        k *= 2
    return b


def _carry_scan(a, b, h_prev):
    L, D = a.shape
    G = L // SUBLANES
    a3, b3 = a.reshape(G, SUBLANES, D), b.reshape(G, SUBLANES, D)
    sub = lax.broadcasted_iota(jnp.int32, (SUBLANES, D), 0)[None]
    k = 1
    while k < SUBLANES:
        keep = sub >= k
        a_sh = jnp.where(keep, pltpu.roll(a3, k, axis=1), 1.0)
        b_sh = jnp.where(keep, pltpu.roll(b3, k, axis=1), 0.0)
        b3 = a3 * b_sh + b3
        a3 = a3 * a_sh
        k *= 2
    carry, outs = h_prev, []
    for j in range(G):
        r = a3[j] * carry + b3[j]
        outs.append(r)
        carry = r[SUBLANES - 1:SUBLANES, :]
    return jnp.concatenate(outs, axis=0)


def _lru_prompt_body(x_ref, g_ref, win_ref, cw_ref, cb_ref, wa_ref, ba_ref, wx_ref, bx_ref,
                     lam_ref, wout_ref, y_ref, conv_ref, h_ref, xc_scr, hprev_scr, *, L):
    c = pl.program_id(1)
    D = x_ref.shape[1]

    @pl.when(c == 0)
    def _():
        xc_scr[...] = jnp.zeros_like(xc_scr)
        hprev_scr[...] = jnp.zeros_like(hprev_scr)

    x = x_ref[...]
    hn = _rms(x, g_ref[...]).astype(BF16)
    gate = _gelu_tanh(_dot(hn, win_ref[:, :D]))
    x_br = _dot(hn, win_ref[:, D:])
    prev8 = xc_scr[...]
    sub8 = lax.broadcasted_iota(jnp.int32, (SUBLANES, 1), 0)

    def shifted(k):
        rolled = pltpu.roll(x_br, k, axis=0)
        first = jnp.where(sub8 < k, pltpu.roll(prev8, k, axis=0), rolled[0:SUBLANES, :])
        return jnp.concatenate([first, rolled[SUBLANES:, :]], axis=0)

    conv = cb_ref[...] + shifted(CONV_W - 1) * cw_ref[0:1, :]
    for j in range(1, CONV_W - 1):
        conv = conv + shifted(CONV_W - 1 - j) * cw_ref[j:j + 1, :]
    conv = conv + x_br * cw_ref[CONV_W - 1:CONV_W, :]
    a, mult, gi = _lru_gates(conv, wa_ref, wx_ref, ba_ref[...], bx_ref[...], lam_ref[...])
    first = jnp.where((sub8 == 0) & (c == 0), 1.0, mult[0:SUBLANES, :])
    mult = jnp.concatenate([first, mult[SUBLANES:, :]], axis=0)
    hs = _carry_scan(a, mult * gi, hprev_scr[SUBLANES - 1:SUBLANES, :])
    y_ref[...] = x + _dot((hs * gate).astype(BF16), wout_ref[...])
    hprev_scr[...] = hs[L - SUBLANES:L, :]
    h_ref[0] = hs[L - SUBLANES:L, :]
    conv_ref[0] = x_br[L - SUBLANES:L, :]
    xc_scr[...] = x_br[L - SUBLANES:L, :]


def _lru_prompt(x_p, B, S, g, win, cw, cb, wa, ba, wx, bx, lam, wout):
    Tp, D = x_p.shape
    L = _pick_tile(S, LRU_TILE)
    nc = S // L
    body = functools.partial(_lru_prompt_body, L=L)
    return pl.pallas_call(
        body,
        out_shape=(jax.ShapeDtypeStruct((Tp, D), F32),
                   jax.ShapeDtypeStruct((B, SUBLANES, D), F32),
                   jax.ShapeDtypeStruct((B, SUBLANES, D), F32)),
        grid=(B, nc),
        in_specs=[
            pl.BlockSpec((L, D), lambda b, c: (b * nc + c, 0)),
            _const_spec((1, D)), _const_spec(win.shape), _const_spec(cw.shape), _const_spec((1, D)),
            _const_spec(wa.shape), _const_spec((1, D)), _const_spec(wx.shape), _const_spec((1, D)),
            _const_spec((1, D)), _const_spec(wout.shape),
        ],
        out_specs=(
            pl.BlockSpec((L, D), lambda b, c: (b * nc + c, 0)),
            pl.BlockSpec((1, SUBLANES, D), lambda b, c: (b, 0, 0)),
            pl.BlockSpec((1, SUBLANES, D), lambda b, c: (b, 0, 0)),
        ),
        scratch_shapes=[pltpu.VMEM((SUBLANES, D), F32), pltpu.VMEM((SUBLANES, D), F32)],
        compiler_params=_params(("arbitrary", "arbitrary")),
        name="lru_prompt",
    )(x_p, g, win, cw, cb, wa, ba, wx, bx, lam, wout)


def _lru_sample_body(x_ref, g_ref, win_ref, cw_ref, cb_ref, wa_ref, ba_ref, wx_ref, bx_ref,
                     lam_ref, wout_ref, cs_ref, h0_ref, y_ref, conv_ref, h_ref,
                     xc_scr, *, Bs, Ls):
    Ts, D = x_ref.shape
    x = x_ref[...]
    hn = _rms(x, g_ref[...]).astype(BF16)
    gate = _gelu_tanh(_dot(hn, win_ref[:, :D]))
    x_br = _dot(hn, win_ref[:, D:])
    xc_scr[:, 0:SUBLANES, :] = cs_ref[...]
    xc_scr[:, SUBLANES:SUBLANES + Ls, :] = x_br.reshape(Bs, Ls, D)
    first = SUBLANES - (CONV_W - 1)
    conv = cb_ref[...] + xc_scr[:, first:first + Ls, :] * cw_ref[0:1, :]
    for j in range(1, CONV_W):
        conv = conv + xc_scr[:, first + j:first + j + Ls, :] * cw_ref[j:j + 1, :]
    conv = conv.reshape(Ts, D)
    a, mult, gi = _lru_gates(conv, wa_ref, wx_ref, ba_ref[...], bx_ref[...], lam_ref[...])
    bvec = mult * gi + a * h0_ref[...].reshape(Ts, D)
    hs = _seg_scan(a, bvec, Ls)
    y_ref[...] = x + _dot((hs * gate).astype(BF16), wout_ref[...])
    h_ref[...] = hs.reshape(Bs, Ls, D)[:, Ls - SUBLANES:Ls, :]
    conv_ref[...] = xc_scr[:, Ls:Ls + SUBLANES, :]


def _lru_sample(x_s, Bs, Ls, g, win, cw, cb, wa, ba, wx, bx, lam, wout, cs, h0):
    Ts, D = x_s.shape
    assert Ls % SUBLANES == 0
    body = functools.partial(_lru_sample_body, Bs=Bs, Ls=Ls)
    return pl.pallas_call(
        body,
        out_shape=(jax.ShapeDtypeStruct((Ts, D), F32),
                   jax.ShapeDtypeStruct((Bs, SUBLANES, D), F32),
                   jax.ShapeDtypeStruct((Bs, SUBLANES, D), F32)),
        grid=(1,),
        in_specs=[
            _const_spec((Ts, D)),
            _const_spec((1, D)), _const_spec(win.shape), _const_spec(cw.shape), _const_spec((1, D)),
            _const_spec(wa.shape), _const_spec((1, D)), _const_spec(wx.shape), _const_spec((1, D)),
            _const_spec((1, D)), _const_spec(wout.shape),
            _const_spec((Bs, SUBLANES, D)), _const_spec((Bs, Ls, D)),
        ],
        out_specs=(
            pl.BlockSpec((Ts, D), lambda i: (0, 0)),
            pl.BlockSpec((Bs, SUBLANES, D), lambda i: (0, 0, 0)),
            pl.BlockSpec((Bs, SUBLANES, D), lambda i: (0, 0, 0)),
        ),
        scratch_shapes=[pltpu.VMEM((Bs, Ls + SUBLANES, D), F32)],
        compiler_params=_params(("arbitrary",)),
        name="lru_sample",
    )(x_s, g, win, cw, cb, wa, ba, wx, bx, lam, wout, cs, h0)


def _router_body(x_ref, g_ref, wc_ref, bias_ref, cin_ref, text_ref, meta_ref, cnt_ref, run_scr,
                 tri_scr):
    i = pl.program_id(0)
    tm, D = x_ref.shape

    @pl.when(i == 0)
    def _():
        run_scr[...] = cin_ref[...]
        ri = lax.broadcasted_iota(jnp.int32, (tm, tm), 0)
        ci = lax.broadcasted_iota(jnp.int32, (tm, tm), 1)
        tri_scr[...] = (ci < ri).astype(BF16)

    t = _rms(x_ref[...], g_ref[...])
    logits = _dot(t.astype(BF16), wc_ref[...]) + bias_ref[...]
    lane_i = lax.broadcasted_iota(jnp.int32, (tm, LANES), 1)
    lane = lane_i.astype(F32)

    def first_argmax(v):
        m = jnp.max(v, axis=-1, keepdims=True)
        return m, jnp.min(jnp.where(v == m, lane, 1e6), axis=-1, keepdims=True)

    gmask = lane_i < MOE_GROUPS
    gl = jnp.where(gmask, logits, NEG_BIG)
    mg, g_idx = first_argmax(gl)
    p_g = 1.0 / jnp.sum(jnp.where(gmask, jnp.exp(gl - mg), 0.0), axis=-1, keepdims=True)
    e_lane = lane_i - MOE_GROUPS
    lane_grp = (e_lane >> 2).astype(F32)
    emask = (e_lane >= 0) & (e_lane < MOE_EXPERTS) & (lane_grp == g_idx)
    el = jnp.where(emask, logits, NEG_BIG)
    m1, i1 = first_argmax(el)
    m2, i2 = first_argmax(jnp.where(lane == i1, NEG_BIG, el))
    e21 = jnp.exp(m2 - m1)
    w1 = p_g / (1.0 + e21)
    w2 = p_g * e21 / (1.0 + e21)
    first_lower = i1 < i2
    j_lo = jnp.minimum(i1, i2) - (MOE_GROUPS + MOE_PER_GROUP * g_idx)
    j_hi = jnp.maximum(i1, i2) - (MOE_GROUPS + MOE_PER_GROUP * g_idx)
    pair = 0.5 * (j_lo * (7.0 - j_lo)) + (j_hi - j_lo - 1.0)
    cls = g_idx * N_PAIRS + pair
    w_lo = jnp.where(first_lower, w1, w2)
    w_hi = jnp.where(first_lower, w2, w1)

    onehot = (lane == cls).astype(F32)
    prefix = _dot(tri_scr[...], onehot.astype(BF16))
    rank = jnp.sum(onehot * (prefix + run_scr[...]), axis=-1, keepdims=True)
    run_scr[...] = run_scr[...] + jnp.sum(onehot, axis=0, keepdims=True)

    text_ref[:, :D] = t
    text_ref[:, D:] = jnp.where(lane_i == 0, w_lo, jnp.where(lane_i == 1, w_hi, 0.0))
    ones8 = jnp.ones((SUBLANES, LANES), BF16)
    ids8 = lax.broadcasted_iota(jnp.int32, (SUBLANES, LANES), 1).astype(F32).astype(BF16)
    d2 = jnp.floor(rank * (1.0 / 65536.0))
    rem = rank - 65536.0 * d2
    d1 = jnp.floor(rem * (1.0 / 256.0))
    d0 = rem - 256.0 * d1

    def as_row(col):
        return _dot_nt(ones8, jnp.where(lane_i == 0, col, 0.0).astype(BF16))

    cls_row = _dot_nt(ids8, onehot.astype(BF16))
    rank_row = as_row(d0) + 256.0 * as_row(d1) + 65536.0 * as_row(d2)
    sub = lax.broadcasted_iota(jnp.int32, (SUBLANES, tm), 0)
    meta_ref[...] = jnp.where(sub == 0, cls_row, jnp.where(sub == 1, rank_row, 0.0))
    cnt_ref[...] = run_scr[...]


def _router(x, g, wc, bias, cnt_in):
    T, D = x.shape
    tm = _pick_tile(T, ROUTE_TILE)
    return pl.pallas_call(
        _router_body,
        out_shape=(jax.ShapeDtypeStruct((T, D + EXTRA), F32),
                   jax.ShapeDtypeStruct((T // tm * SUBLANES, tm), F32),
                   jax.ShapeDtypeStruct((1, LANES), F32)),
        grid=(T // tm,),
        in_specs=[
            pl.BlockSpec((tm, D), lambda i: (i, 0)),
            _const_spec((1, D)), _const_spec(wc.shape), _const_spec((1, LANES)),
            _const_spec((1, LANES)),
        ],
        out_specs=(
            pl.BlockSpec((tm, D + EXTRA), lambda i: (i, 0)),
            pl.BlockSpec((SUBLANES, tm), lambda i: (i, 0)),
            pl.BlockSpec((1, LANES), lambda i: (0, 0)),
        ),
        scratch_shapes=[pltpu.VMEM((1, LANES), F32), pltpu.VMEM((tm, tm), BF16)],
        compiler_params=_params(("arbitrary",)),
        name="moe_router",
    )(x, g, wc, bias, cnt_in)


def _start_rows(n, idx_ref, row_copy):
    assert n % SUBLANES == 0

    def body(j, carry):
        for u in range(SUBLANES):
            pos = idx_ref[0, 0, j * SUBLANES + u]
            row_copy(j, u, pos >> SUBLANE_SHIFT, pos & (SUBLANES - 1)).start(priority=u % 2)
        return carry

    lax.fori_loop(0, n // SUBLANES, body, 0)


def _dispatch_body(zinfo_ref, dest_ref, tp_ref, ts_ref, xs_ref, tbuf, zbuf, in_sems, row_sems, zsem,
                   *, nbp, nbs, steps):
    i = pl.program_id(0)
    nb = nbp + nbs
    groups = tbuf.shape[1]
    zgroups = zbuf.shape[0]

    def zero_copies():
        out = []
        for c in range(N_CLASSES):
            r0, r1 = zinfo_ref[c], zinfo_ref[N_CLASSES + c]
            g0, g1 = r0 >> SUBLANE_SHIFT, r1 >> SUBLANE_SHIFT
            for k in range(SUBLANES - 1):
                out.append((k < r1 - g1 * SUBLANES, pltpu.make_async_copy(
                    zbuf.at[0, pl.ds(0, 1), :], xs_ref.at[g1, pl.ds(k, 1), :], zsem)))
            n = g1 - g0
            b = zgroups // 2
            while b >= 1:
                out.append(((n & b) != 0, pltpu.make_async_copy(
                    zbuf.at[pl.ds(0, b)], xs_ref.at[pl.ds(g0 + (n & -(2 * b)), b)], zsem)))
                b //= 2
        for k in range(N_CLASSES):
            step = zinfo_ref[2 * N_CLASSES] + k
            out.append((step < steps, pltpu.make_async_copy(
                zbuf, xs_ref.at[pl.ds(jnp.minimum(step, steps - 1) * zgroups, zgroups)], zsem)))
        return out

    def tile_copies(j, slot):
        jp = jnp.clip(j, 0, nbp - 1)
        js = jnp.clip(j - nbp, 0, nbs - 1)
        return [(j < nbp, pltpu.make_async_copy(tp_ref.at[pl.ds(jp * groups, groups)],
                                                tbuf.at[slot], in_sems.at[slot])),
                (j >= nbp, pltpu.make_async_copy(ts_ref.at[pl.ds(js * groups, groups)],
                                                 tbuf.at[slot], in_sems.at[slot]))]

    def start_all(copies):
        for pred, cp in copies:
            @pl.when(pred)
            def _():
                cp.start()

    def wait_all(copies):
        for pred, cp in copies:
            @pl.when(pred)
            def _():
                cp.wait()

    def wait_rows(slot):
        pltpu.make_async_copy(tbuf.at[slot], xs_ref.at[pl.ds(0, groups)], row_sems.at[slot]).wait()

    @pl.when(i == 0)
    def _():
        zbuf[...] = jnp.zeros_like(zbuf)
        start_all(zero_copies())
        for j in range(min(DISPATCH_SLOTS - 1, nb)):
            start_all(tile_copies(jnp.int32(j), j))

    for slot in range(DISPATCH_SLOTS):
        @pl.when(i % DISPATCH_SLOTS == slot)
        def _():
            freed = (slot + DISPATCH_SLOTS - 1) % DISPATCH_SLOTS
            wait_all(tile_copies(i, slot))

            def row_copy(j, u, hi, lo):
                return pltpu.make_async_copy(tbuf.at[slot, j, pl.ds(u, 1), :],
                                             xs_ref.at[hi, pl.ds(lo, 1), :], row_sems.at[slot])

            _start_rows(groups * SUBLANES, dest_ref, row_copy)

            @pl.when(i >= 1)
            def _():
                wait_rows(freed)

            @pl.when(i + DISPATCH_SLOTS - 1 < nb)
            def _():
                start_all(tile_copies(i + DISPATCH_SLOTS - 1, freed))

            @pl.when(i == nb - 1)
            def _():
                wait_rows(slot)

    @pl.when(i == 0)
    def _():
        wait_all(zero_copies())


def _dispatch(t_p, t_s, dest3, zinfo, P):
    nb, _, td = dest3.shape
    W = t_p.shape[1]
    assert t_p.shape[0] % td == 0 and t_s.shape[0] % td == 0 and FFN_TILE % SUBLANES == 0
    nbp, nbs = t_p.shape[0] // td, t_s.shape[0] // td
    assert nb == nbp + nbs and nbp >= 1 and nbs >= 1
    zgroups = FFN_TILE // SUBLANES
    body = functools.partial(_dispatch_body, nbp=nbp, nbs=nbs, steps=P // FFN_TILE)
    grid_spec = pltpu.PrefetchScalarGridSpec(
        num_scalar_prefetch=1,
        grid=(nb,),
        in_specs=[
            pl.BlockSpec((1, 1, td), lambda i, z: (i, 0, 0), memory_space=pltpu.SMEM),
            pl.BlockSpec(memory_space=pl.ANY),
            pl.BlockSpec(memory_space=pl.ANY),
        ],
        out_specs=pl.BlockSpec(memory_space=pl.ANY),
        scratch_shapes=[pltpu.VMEM((DISPATCH_SLOTS, td // SUBLANES, SUBLANES, W), F32),
                        pltpu.VMEM((zgroups, SUBLANES, W), F32),
                        pltpu.SemaphoreType.DMA((DISPATCH_SLOTS,)),
                        pltpu.SemaphoreType.DMA((DISPATCH_SLOTS,)),
                        pltpu.SemaphoreType.DMA(())],
    )
    return pl.pallas_call(
        body,
        out_shape=jax.ShapeDtypeStruct((P // SUBLANES, SUBLANES, W), F32),
        grid_spec=grid_spec,
        compiler_params=_params(("arbitrary",)),
        name="moe_dispatch",
    )(zinfo, dest3, t_p.reshape(-1, SUBLANES, W), t_s.reshape(-1, SUBLANES, W))


def _ffn_body(ea_ref, eb_ref, halves_ref, xs_ref, wga_ref, wua_ref, wda_ref,
              wgb_ref, wub_ref, wdb_ref, ys_ref, wg_scr, wu_scr, wd_scr):
    i = pl.program_id(0)
    tg, D = ys_ref.shape
    nh = halves_ref[i]
    prev = jnp.maximum(i - 1, 0)
    new_class = (i == 0) | (ea_ref[i] != ea_ref[prev]) | (eb_ref[i] != eb_ref[prev])

    @pl.when((nh > 0) & new_class)
    def _():
        for slot, (wg, wu, wd) in enumerate(((wga_ref, wua_ref, wda_ref), (wgb_ref, wub_ref, wdb_ref))):
            wg_scr[slot] = wg[0, 0].astype(BF16)
            wu_scr[slot] = wu[0, 0].astype(BF16)
            wd_scr[slot] = wd[0, 0].astype(BF16)

    def experts(rows):
        x = xs_ref[rows, :D].astype(BF16)
        w_a = xs_ref[rows, D:D + 1]
        w_b = xs_ref[rows, D + 1:D + 2]

        def expert(slot):
            gt = _dot(x, wg_scr[slot])
            hdn = (gt * jax.nn.sigmoid(gt)) * _dot(x, wu_scr[slot])
            return _dot(hdn.astype(BF16), wd_scr[slot])

        return w_a * expert(0) + w_b * expert(1)

    @pl.when(nh == 2)
    def _():
        ys_ref[...] = experts(slice(0, tg))

    @pl.when(nh == 1)
    def _():
        ys_ref[:tg - FFN_HALF, :] = jnp.zeros((tg - FFN_HALF, D), F32)
        ys_ref[tg - FFN_HALF:, :] = experts(slice(tg - FFN_HALF, tg))

    @pl.when(nh == 0)
    def _():
        ys_ref[...] = jnp.zeros_like(ys_ref)


def _ffn(xs, ea, eb, halves, layer, wg, wu, wd):
    P, W = xs.shape
    D = W - EXTRA
    ff = wg.shape[3]
    tg = FFN_TILE
    wspec_a = lambda shape: pl.BlockSpec(shape, lambda i, ea, eb, nh: (layer, ea[i], 0, 0))
    wspec_b = lambda shape: pl.BlockSpec(shape, lambda i, ea, eb, nh: (layer, eb[i], 0, 0))
    grid_spec = pltpu.PrefetchScalarGridSpec(
        num_scalar_prefetch=3,
        grid=(P // tg,),
        in_specs=[
            pl.BlockSpec((tg, W), lambda i, ea, eb, nh: (i, 0)),
            wspec_a((1, 1, D, ff)), wspec_a((1, 1, D, ff)), wspec_a((1, 1, ff, D)),
            wspec_b((1, 1, D, ff)), wspec_b((1, 1, D, ff)), wspec_b((1, 1, ff, D)),
        ],
        out_specs=pl.BlockSpec((tg, D), lambda i, ea, eb, nh: (i, 0)),
        scratch_shapes=[pltpu.VMEM((2, D, ff), BF16), pltpu.VMEM((2, D, ff), BF16),
                        pltpu.VMEM((2, ff, D), BF16)],
    )
    return pl.pallas_call(
        _ffn_body,
        out_shape=jax.ShapeDtypeStruct((P, D), F32),
        grid_spec=grid_spec,
        compiler_params=_params(("arbitrary",)),
        name="moe_ffn",
    )(ea, eb, halves, xs, wg, wu, wd, wg, wu, wd)


def _combine_body(dest_ref, dnext_ref, x_ref, gfin_ref, ys_ref, o_ref, ybuf, sems, *, final_norm):
    i, n = pl.program_id(0), pl.num_programs(0)
    tc, D = x_ref.shape
    groups = tc // SUBLANES

    def gather(idx_ref, slot):
        def row_copy(j, u, hi, lo):
            return pltpu.make_async_copy(ys_ref.at[hi, pl.ds(lo, 1), :],
                                         ybuf.at[slot, j, pl.ds(u, 1), :], sems.at[slot])
        _start_rows(tc, idx_ref, row_copy)

    @pl.when(i == 0)
    def _():
        gather(dest_ref, 0)

    for slot in range(2):
        @pl.when((i + 1 < n) & ((i + 1) % 2 == slot))
        def _():
            gather(dnext_ref, slot)

    for slot in range(2):
        @pl.when(i % 2 == slot)
        def _():
            pltpu.make_async_copy(ys_ref.at[pl.ds(0, groups)], ybuf.at[slot], sems.at[slot]).wait()
            y = x_ref[...] + ybuf[slot].reshape(tc, D)
            if final_norm:
                y = _rms(y, gfin_ref[...])
            o_ref[...] = y


def _combine(x, dest3, ys, gfin, final_norm):
    T, D = x.shape
    nb, _, tc = dest3.shape
    body = functools.partial(_combine_body, final_norm=final_norm)
    return pl.pallas_call(
        body,
        out_shape=jax.ShapeDtypeStruct((T, D), F32),
        grid=(nb,),
        in_specs=[
            pl.BlockSpec((1, 1, tc), lambda i: (i, 0, 0), memory_space=pltpu.SMEM),
            pl.BlockSpec((1, 1, tc), lambda i: (jnp.minimum(i + 1, nb - 1), 0, 0),
                         memory_space=pltpu.SMEM),
            pl.BlockSpec((tc, D), lambda i: (i, 0)),
            _const_spec((1, D)),
            pl.BlockSpec(memory_space=pl.ANY),
        ],
        out_specs=pl.BlockSpec((tc, D), lambda i: (i, 0)),
        scratch_shapes=[pltpu.VMEM((2, tc // SUBLANES, SUBLANES, D), F32),
                        pltpu.SemaphoreType.DMA((2,))],
        compiler_params=_params(("arbitrary",)),
        name="moe_combine",
    )(dest3, dest3, x, gfin, ys.reshape(-1, SUBLANES, D))


def _moe_tables(metas, counts, P):
    tg = FFN_TILE
    cnt = counts[0, :N_CLASSES].astype(jnp.int32)
    padded = ((cnt + tg - 1) // tg) * tg
    ends = jnp.cumsum(padded)
    offs = ends - padded
    front = padded - cnt
    dests = []
    for m in metas:
        m = m.reshape(-1, SUBLANES, m.shape[1])
        cls, rank = m[:, 0, :].reshape(-1).astype(jnp.int32), m[:, 1, :].reshape(-1).astype(jnp.int32)
        base = jnp.sum(jnp.where(cls[:, None] == jnp.arange(N_CLASSES, dtype=jnp.int32)[None, :],
                                 (offs + front)[None, :], 0), axis=1)
        dests.append(base + rank)
    tile_start = jnp.arange(P // tg, dtype=jnp.int32) * tg
    tile_cls = jnp.minimum(jnp.sum((tile_start[:, None] >= ends[None, :]).astype(jnp.int32), axis=1),
                           N_CLASSES - 1)
    rows = tg - jnp.clip(jnp.take(offs + front, tile_cls) - tile_start, 0, tg)
    halves = jnp.where(tile_start < ends[-1], (rows + FFN_HALF - 1) // FFN_HALF, 0).astype(jnp.int32)
    grp = tile_cls // N_PAIRS
    pair = tile_cls % N_PAIRS
    ea = grp * MOE_PER_GROUP + jnp.take(jnp.asarray(_PAIR_LO), pair)
    eb = grp * MOE_PER_GROUP + jnp.take(jnp.asarray(_PAIR_HI), pair)
    zinfo = jnp.concatenate([offs, offs + front, ends[-1:] // tg]).astype(jnp.int32)
    return dests, ea, eb, halves, zinfo


def _moe(xs_in, g, w_group, b_group, w_router, b_router, layer, wg, wu, wd, gfin, final_norm):
    assert len(xs_in) == 2
    D = xs_in[0].shape[1]
    T = sum(x.shape[0] for x in xs_in)
    wcat = jnp.zeros((D, LANES), F32)
    wcat = wcat.at[:, :MOE_GROUPS].set(w_group).at[:, MOE_GROUPS:MOE_GROUPS + MOE_EXPERTS].set(w_router)
    bias = jnp.zeros((1, LANES), F32)
    bias = bias.at[0, :MOE_GROUPS].set(b_group).at[0, MOE_GROUPS:MOE_GROUPS + MOE_EXPERTS].set(b_router)
    wc = wcat.astype(BF16)

    counts = jnp.zeros((1, LANES), F32)
    texts, metas = [], []
    for x in xs_in:
        t_ext, meta, counts = _router(x, g, wc, bias, counts)
        texts.append(t_ext)
        metas.append(meta)

    tg = FFN_TILE
    P = ((T + tg - 1) // tg + N_CLASSES) * tg
    dests, ea, eb, halves, zinfo = _moe_tables(metas, counts, P)
    tmove = _pick_tile(math.gcd(*(x.shape[0] for x in xs_in)), MOVE_TILE)
    dest3 = jnp.concatenate(dests).reshape(-1, 1, tmove)
    xs = _dispatch(texts[0], texts[1], dest3, zinfo, P)
    ys = _ffn(xs.reshape(P, D + EXTRA), ea, eb, halves, layer, wg, wu, wd)
    return [_combine(x, dest.reshape(-1, 1, _pick_tile(x.shape[0], COMBINE_TILE)), ys, gfin, final_norm)
            for x, dest in zip(xs_in, dests)]


def _rope_tables(pos, dk):
    half = dk // 2
    inv = 1.0 / (ROPE_BASE ** jnp.linspace(0.0, 1.0, half, dtype=F32))
    ang = pos.astype(F32)[:, None] * inv[None, :]
    return jnp.cos(ang), jnp.sin(ang)


def kernel(x_prompt, x_sample, state_ret, state_conv, state_lru, norm_mix_g, norm_ffn_g, norm_final_g, ret_w_in, ret_w_out, ret_norm_g, lru_w_in, lru_conv_w, lru_conv_b, lru_w_a, lru_b_a, lru_w_x, lru_b_x, lru_lambda, lru_w_out, moe_w_group, moe_b_group, moe_w_router, moe_b_router, moe_w_gate, moe_w_up, moe_w_down):
    B, S, D = x_prompt.shape
    Bs, Ls, _ = x_sample.shape
    Tp, Ts = B * S, Bs * Ls
    assert norm_mix_g.shape[0] == 2
    dk = D // RET_HEADS
    row = lambda v: v.reshape(1, -1)
    gfin = row(norm_final_g)

    cos_p, sin_p = _rope_tables(jnp.arange(S), dk)
    cos_s, sin_s = _rope_tables(PAST_LEN + jnp.arange(Ls), dk)
    ret_args = (row(norm_mix_g[0]), ret_w_in[0].astype(BF16), ret_w_out[0].astype(BF16),
                row(ret_norm_g[0]))
    xp, ret_p = _ret_prompt(x_prompt.reshape(Tp, D), B, S, *ret_args, cos_p, sin_p)
    xs, ret_s = _ret_sample(x_sample.reshape(Ts, D), Bs, Ls, *ret_args, cos_s, sin_s, state_ret[0])
    xp, xs = _moe([xp, xs], row(norm_ffn_g[0]), moe_w_group[0], moe_b_group[0], moe_w_router[0],
                  moe_b_router[0], 0, moe_w_gate, moe_w_up, moe_w_down, gfin, False)

    lru_args = (row(norm_mix_g[1]), lru_w_in[0].astype(BF16), lru_conv_w[0], row(lru_conv_b[0]),
                lru_w_a[0].astype(BF16), row(lru_b_a[0]), lru_w_x[0].astype(BF16), row(lru_b_x[0]),
                row(lru_lambda[0]), lru_w_out[0].astype(BF16))
    cs8 = jnp.pad(state_conv[0], ((0, 0), (SUBLANES - (CONV_W - 1), 0), (0, 0)))
    h0_rows = jnp.pad(state_lru[0].reshape(Bs, 1, D), ((0, 0), (0, Ls - 1), (0, 0)))
    xp, conv_p, lru_p = _lru_prompt(xp, B, S, *lru_args)
    xs, conv_s, lru_s = _lru_sample(xs, Bs, Ls, *lru_args, cs8, h0_rows)
    y_p, y_s = _moe([xp, xs], row(norm_ffn_g[1]), moe_w_group[1], moe_b_group[1], moe_w_router[1],
                    moe_b_router[1], 1, moe_w_gate, moe_w_up, moe_w_down, gfin, True)

    return (y_p.reshape(B, S, D), y_s.reshape(Bs, Ls, D),
            ret_p[None], conv_p[None, :, SUBLANES - (CONV_W - 1):], lru_p[None, :, SUBLANES - 1],
            ret_s[None], conv_s[None, :, SUBLANES - (CONV_W - 1):], lru_s[None, :, SUBLANES - 1])
```

```python
import functools
import math

import numpy as np
import jax
import jax.numpy as jnp
from jax import lax
from jax.experimental import pallas as pl
from jax.experimental.pallas import tpu as pltpu

F32 = jnp.float32
BF16 = jnp.bfloat16

RET_HEADS = 4
RET_CHUNK = 64
ROPE_BASE = 10000.0
CONV_W = 4
LRU_BLOCKS = 4
LRU_C = 8.0
MOE_GROUPS = 4
MOE_PER_GROUP = 4
MOE_EXPERTS = MOE_GROUPS * MOE_PER_GROUP
NORM_EPS = 1e-6
PAST_LEN = 2048

LANES = 128
SUBLANES = 8
SUBLANE_SHIFT = 3
VMEM_LIMIT_BYTES = 56 * 1024 * 1024

RET_TILE = 512
RET_SCORE_BLOCK = 256
RET_SAMPLE_STREAMS = 2
LRU_TILE = 512
ROUTE_TILE = 1024
MOVE_TILE = 512
DISPATCH_SLOTS = 3
COMBINE_TILE = 1024
FFN_TILE = 512
FFN_HALF = 256
N_PAIRS = 6
N_CLASSES = MOE_GROUPS * N_PAIRS
EXTRA = LANES
NEG_BIG = -1e30

_PAIR_LO = np.array([0, 0, 0, 1, 1, 2], np.int32)
_PAIR_HI = np.array([1, 2, 3, 2, 3, 3], np.int32)


def _log_gammas():
    return [math.log1p(-(2.0 ** (-5.0 - h))) for h in range(RET_HEADS)]


def _rms(x, g):
    return x * lax.rsqrt(jnp.mean(x * x, axis=-1, keepdims=True) + NORM_EPS) * g


def _dot(a, b):
    return jnp.dot(a, b, preferred_element_type=F32)


def _dot_nt(a, b):
    return lax.dot_general(a, b, (((1,), (1,)), ((), ())), preferred_element_type=F32)


def _dot_tn(a, b):
    return lax.dot_general(a, b, (((0,), (0,)), ((), ())), preferred_element_type=F32)


def _const_spec(shape):
    nd = len(shape)
    return pl.BlockSpec(shape, lambda *_: (0,) * nd, pipeline_mode=pl.Buffered(1))


def _params(sem):
    return pltpu.CompilerParams(dimension_semantics=sem, vmem_limit_bytes=VMEM_LIMIT_BYTES)


def _pick_tile(n, pref):
    t = pref
    while n % t:
        t //= 2
    assert t >= SUBLANES, (n, pref)
    return t


def _decay_matrix(L, chunk, lg):
    ii = lax.broadcasted_iota(jnp.int32, (L, L), 0)
    jj = lax.broadcasted_iota(jnp.int32, (L, L), 1)
    diff = (ii - jj).astype(F32)
    same_chunk = (ii // chunk) == (jj // chunk)
    return jnp.where(same_chunk & (ii >= jj), jnp.exp(jnp.maximum(diff, 0.0) * lg), 0.0)


def _rotary(t, cos, sin):
    half = t.shape[-1] // 2
    t1, t2 = t[:, :half], t[:, half:]
    return jnp.concatenate([t1 * cos - t2 * sin, t1 * sin + t2 * cos], axis=-1)


def _ret_proj(hn, win_ref, hd, dk, dv):
    qk, vv = RET_HEADS * dk, RET_HEADS * dv
    q = _dot(hn, win_ref[:, hd * dk:(hd + 1) * dk])
    k = _dot(hn, win_ref[:, qk + hd * dk:qk + (hd + 1) * dk])
    v = _dot(hn, win_ref[:, 2 * qk + hd * dv:2 * qk + (hd + 1) * dv])
    gt = _dot(hn, win_ref[:, 2 * qk + vv + hd * dv:2 * qk + vv + (hd + 1) * dv])
    return q, k, v, gt


def _ret_head(q, k, v, gt, cos, sin, decay, s0, ng, chunk, lg, kv_scr=None):
    L, dk = q.shape
    qr = _rotary(q, cos, sin)
    kr = _rotary(k, cos, sin) * (dk ** -0.5)
    idx = (lax.broadcasted_iota(jnp.int32, (L, 1), 0) % chunk).astype(F32)
    qd = (qr * jnp.exp((idx + 1.0) * lg)).astype(BF16)
    kd = (kr * jnp.exp((chunk - 1.0 - idx) * lg)).astype(BF16)
    qb, kb, vb = qr.astype(BF16), kr.astype(BF16), v.astype(BF16)
    core = decay.shape[0]
    chunks = [slice(c * chunk, (c + 1) * chunk) for c in range(L // chunk)]
    if kv_scr is not None:
        for c, rows in enumerate(chunks):
            kv_scr[c] = _dot_tn(kd[rows], vb[rows])
    s, parts = s0, []
    for r0 in range(0, L, core):
        blk = slice(r0, r0 + core)
        scores = _dot_nt(qb[blk], kb[blk]) * decay
        intra = _dot(scores.astype(BF16), vb[blk])
        for c in range(r0 // chunk, (r0 + core) // chunk):
            rows = chunks[c]
            parts.append(intra[rows.start - r0:rows.stop - r0] + _dot(qd[rows], s.astype(BF16)))
            kv = kv_scr[c] if kv_scr is not None else _dot_tn(kd[rows], vb[rows])
            s = math.exp(chunk * lg) * s + kv
    o = parts[0] if len(parts) == 1 else jnp.concatenate(parts, axis=0)
    o = o * lax.rsqrt(jnp.mean(o * o, axis=-1, keepdims=True) + NORM_EPS)
    og = (gt * jax.nn.sigmoid(gt)) * (o * ng)
    return og, s


def _ret_prompt_body(x_ref, g_ref, win_ref, wout_ref, ng_ref, cos_ref, sin_ref,
                     y_ref, s_ref, decay_scr, og_scr, kv_scr, *, L, chunk, dk, dv):
    b, c = pl.program_id(0), pl.program_id(1)
    lgs = _log_gammas()

    @pl.when((b == 0) & (c == 0))
    def _():
        for hd in range(RET_HEADS):
            decay_scr[hd] = _decay_matrix(decay_scr.shape[1], chunk, lgs[hd])

    @pl.when(c == 0)
    def _():
        s_ref[...] = jnp.zeros_like(s_ref)

    x = x_ref[...]
    hn = _rms(x, g_ref[...]).astype(BF16)
    cos, sin = cos_ref[...], sin_ref[...]
    for hd in range(RET_HEADS):
        og, s_new = _ret_head(*_ret_proj(hn, win_ref, hd, dk, dv), cos, sin, decay_scr[hd],
                              s_ref[0, hd], ng_ref[:, hd * dv:(hd + 1) * dv], chunk, lgs[hd],
                              kv_scr)
        s_ref[0, hd] = s_new
        og_scr[:, hd * dv:(hd + 1) * dv] = og.astype(BF16)
    y_ref[...] = x + _dot(og_scr[...], wout_ref[...])


def _ret_prompt(x_p, B, S, g, win, wout, ng, cos, sin):
    Tp, D = x_p.shape
    dk, dv = D // RET_HEADS, 2 * D // RET_HEADS
    chunk = min(S, RET_CHUNK)
    L = _pick_tile(S, RET_TILE)
    core = min(L, RET_SCORE_BLOCK)
    assert L % core == 0 and core % chunk == 0
    nc = S // L
    body = functools.partial(_ret_prompt_body, L=L, chunk=chunk, dk=dk, dv=dv)
    return pl.pallas_call(
        body,
        out_shape=(jax.ShapeDtypeStruct((Tp, D), F32),
                   jax.ShapeDtypeStruct((B, RET_HEADS, dk, dv), F32)),
        grid=(B, nc),
        in_specs=[
            pl.BlockSpec((L, D), lambda b, c: (b * nc + c, 0)),
            _const_spec((1, D)),
            _const_spec(win.shape),
            _const_spec(wout.shape),
            _const_spec((1, RET_HEADS * dv)),
            pl.BlockSpec((L, dk // 2), lambda b, c: (c, 0)),
            pl.BlockSpec((L, dk // 2), lambda b, c: (c, 0)),
        ],
        out_specs=(
            pl.BlockSpec((L, D), lambda b, c: (b * nc + c, 0)),
            pl.BlockSpec((1, RET_HEADS, dk, dv), lambda b, c: (b, 0, 0, 0)),
        ),
        scratch_shapes=[pltpu.VMEM((RET_HEADS, core, core), F32),
                        pltpu.VMEM((L, RET_HEADS * dv), BF16),
                        pltpu.VMEM((L // chunk, dk, dv), F32)],
        compiler_params=_params(("arbitrary", "arbitrary")),
        name="ret_prompt",
    )(x_p, g, win, wout, ng, cos, sin)


def _ret_sample_body(x_ref, g_ref, win_ref, wout_ref, ng_ref, cos_ref, sin_ref, s0_ref,
                     y_ref, s_ref, q_scr, k_scr, v_scr, gt_scr, og_scr, *, L, dk, dv):
    b, nb = pl.program_id(0), pl.num_programs(0)
    per = s0_ref.shape[0]
    lgs = _log_gammas()

    @pl.when(b == 0)
    def _():
        hn = _rms(x_ref[...], g_ref[...]).astype(BF16)
        for hd in range(RET_HEADS):
            q, k, v, gt = _ret_proj(hn, win_ref, hd, dk, dv)
            q_scr[hd], k_scr[hd], v_scr[hd], gt_scr[hd] = q, k, v, gt

    cos, sin = cos_ref[...], sin_ref[...]
    for j in range(per):
        rows = pl.ds(pl.multiple_of((b * per + j) * L, L), L)
        for hd in range(RET_HEADS):
            og, s_new = _ret_head(q_scr[hd, rows, :], k_scr[hd, rows, :], v_scr[hd, rows, :],
                                  gt_scr[hd, rows, :], cos, sin, _decay_matrix(L, L, lgs[hd]),
                                  s0_ref[j, hd], ng_ref[:, hd * dv:(hd + 1) * dv], L, lgs[hd])
            s_ref[j, hd] = s_new
            og_scr[rows, hd * dv:(hd + 1) * dv] = og.astype(BF16)

    @pl.when(b == nb - 1)
    def _():
        y_ref[...] = x_ref[...] + _dot(og_scr[...], wout_ref[...])


def _ret_sample(x_s, Bs, Ls, g, win, wout, ng, cos, sin, s0):
    Ts, D = x_s.shape
    dk, dv = D // RET_HEADS, 2 * D // RET_HEADS
    assert Ls <= RET_CHUNK
    per = RET_SAMPLE_STREAMS if Bs % RET_SAMPLE_STREAMS == 0 else 1
    body = functools.partial(_ret_sample_body, L=Ls, dk=dk, dv=dv)
    return pl.pallas_call(
        body,
        out_shape=(jax.ShapeDtypeStruct((Ts, D), F32),
                   jax.ShapeDtypeStruct((Bs, RET_HEADS, dk, dv), F32)),
        grid=(Bs // per,),
        in_specs=[
            _const_spec((Ts, D)),
            _const_spec((1, D)),
            _const_spec(win.shape),
            _const_spec(wout.shape),
            _const_spec((1, RET_HEADS * dv)),
            _const_spec((Ls, dk // 2)),
            _const_spec((Ls, dk // 2)),
            pl.BlockSpec((per, RET_HEADS, dk, dv), lambda b: (b, 0, 0, 0)),
        ],
        out_specs=(
            pl.BlockSpec((Ts, D), lambda b: (0, 0)),
            pl.BlockSpec((per, RET_HEADS, dk, dv), lambda b: (b, 0, 0, 0)),
        ),
        scratch_shapes=[pltpu.VMEM((RET_HEADS, Ts, dk), F32), pltpu.VMEM((RET_HEADS, Ts, dk), F32),
                        pltpu.VMEM((RET_HEADS, Ts, dv), F32), pltpu.VMEM((RET_HEADS, Ts, dv), F32),
                        pltpu.VMEM((Ts, RET_HEADS * dv), BF16)],
        compiler_params=_params(("arbitrary",)),
        name="ret_sample",
    )(x_s, g, win, wout, ng, cos, sin, s0)


def _gelu_tanh(x):
    return 0.5 * x * (1.0 + jnp.tanh(math.sqrt(2.0 / math.pi) * (x + 0.044715 * (x * x * x))))


def _softplus(z):
    return jnp.maximum(z, 0.0) + jnp.log1p(jnp.exp(-jnp.abs(z)))


def _lru_gates(conv, wa_ref, wx_ref, ba, bx, lam):
    D = conv.shape[1]
    bw = D // LRU_BLOCKS
    ra, ix = [], []
    for n in range(LRU_BLOCKS):
        cb = conv[:, n * bw:(n + 1) * bw].astype(BF16)
        ra.append(_dot(cb, wa_ref[n]))
        ix.append(_dot(cb, wx_ref[n]))
    r = jax.nn.sigmoid(jnp.concatenate(ra, axis=-1) + ba)
    i = jax.nn.sigmoid(jnp.concatenate(ix, axis=-1) + bx)
    log_a = r * ((-LRU_C) * _softplus(-lam))
    a = jnp.exp(log_a)
    m2 = jnp.tanh(-log_a) * (1.0 + a * a)
    mult = jnp.where(m2 > 0.0, m2 * lax.rsqrt(m2), 0.0)
    return a, mult, i * conv


def _seg_scan(a, b, seg):
    R = a.shape[0]
    row = lax.broadcasted_iota(jnp.int32, (R, 1), 0) % seg
    k = 1
    while k < seg:
        keep = row >= k
        a_sh = jnp.where(keep, pltpu.roll(a, k, axis=0), 1.0)
        b_sh = jnp.where(keep, pltpu.roll(b, k, axis=0), 0.0)
        b = a * b_sh + b
        a = a * a_sh
        k *= 2
    return b


def _carry_scan(a, b, h_prev):
    L, D = a.shape
    G = L // SUBLANES
    a3, b3 = a.reshape(G, SUBLANES, D), b.reshape(G, SUBLANES, D)
    sub = lax.broadcasted_iota(jnp.int32, (SUBLANES, D), 0)[None]
    k = 1
    while k < SUBLANES:
        keep = sub >= k
        a_sh = jnp.where(keep, pltpu.roll(a3, k, axis=1), 1.0)
        b_sh = jnp.where(keep, pltpu.roll(b3, k, axis=1), 0.0)
        b3 = a3 * b_sh + b3
        a3 = a3 * a_sh
        k *= 2
    carry, outs = h_prev, []
    for j in range(G):
        r = a3[j] * carry + b3[j]
        outs.append(r)
        carry = r[SUBLANES - 1:SUBLANES, :]
    return jnp.concatenate(outs, axis=0)


def _lru_prompt_body(x_ref, g_ref, win_ref, cw_ref, cb_ref, wa_ref, ba_ref, wx_ref, bx_ref,
                     lam_ref, wout_ref, y_ref, conv_ref, h_ref, xc_scr, hprev_scr, *, L):
    c = pl.program_id(1)
    D = x_ref.shape[1]

    @pl.when(c == 0)
    def _():
        xc_scr[...] = jnp.zeros_like(xc_scr)
        hprev_scr[...] = jnp.zeros_like(hprev_scr)

    x = x_ref[...]
    hn = _rms(x, g_ref[...]).astype(BF16)
    gate = _gelu_tanh(_dot(hn, win_ref[:, :D]))
    x_br = _dot(hn, win_ref[:, D:])
    prev8 = xc_scr[...]
    sub8 = lax.broadcasted_iota(jnp.int32, (SUBLANES, 1), 0)

    def shifted(k):
        rolled = pltpu.roll(x_br, k, axis=0)
        first = jnp.where(sub8 < k, pltpu.roll(prev8, k, axis=0), rolled[0:SUBLANES, :])
        return jnp.concatenate([first, rolled[SUBLANES:, :]], axis=0)

    conv = cb_ref[...] + shifted(CONV_W - 1) * cw_ref[0:1, :]
    for j in range(1, CONV_W - 1):
        conv = conv + shifted(CONV_W - 1 - j) * cw_ref[j:j + 1, :]
    conv = conv + x_br * cw_ref[CONV_W - 1:CONV_W, :]
    a, mult, gi = _lru_gates(conv, wa_ref, wx_ref, ba_ref[...], bx_ref[...], lam_ref[...])
    first = jnp.where((sub8 == 0) & (c == 0), 1.0, mult[0:SUBLANES, :])
    mult = jnp.concatenate([first, mult[SUBLANES:, :]], axis=0)
    hs = _carry_scan(a, mult * gi, hprev_scr[SUBLANES - 1:SUBLANES, :])
    y_ref[...] = x + _dot((hs * gate).astype(BF16), wout_ref[...])
    hprev_scr[...] = hs[L - SUBLANES:L, :]
    h_ref[0] = hs[L - SUBLANES:L, :]
    conv_ref[0] = x_br[L - SUBLANES:L, :]
    xc_scr[...] = x_br[L - SUBLANES:L, :]


def _lru_prompt(x_p, B, S, g, win, cw, cb, wa, ba, wx, bx, lam, wout):
    Tp, D = x_p.shape
    L = _pick_tile(S, LRU_TILE)
    nc = S // L
    body = functools.partial(_lru_prompt_body, L=L)
    return pl.pallas_call(
        body,
        out_shape=(jax.ShapeDtypeStruct((Tp, D), F32),
                   jax.ShapeDtypeStruct((B, SUBLANES, D), F32),
                   jax.ShapeDtypeStruct((B, SUBLANES, D), F32)),
        grid=(B, nc),
        in_specs=[
            pl.BlockSpec((L, D), lambda b, c: (b * nc + c, 0)),
            _const_spec((1, D)), _const_spec(win.shape), _const_spec(cw.shape), _const_spec((1, D)),
            _const_spec(wa.shape), _const_spec((1, D)), _const_spec(wx.shape), _const_spec((1, D)),
            _const_spec((1, D)), _const_spec(wout.shape),
        ],
        out_specs=(
            pl.BlockSpec((L, D), lambda b, c: (b * nc + c, 0)),
            pl.BlockSpec((1, SUBLANES, D), lambda b, c: (b, 0, 0)),
            pl.BlockSpec((1, SUBLANES, D), lambda b, c: (b, 0, 0)),
        ),
        scratch_shapes=[pltpu.VMEM((SUBLANES, D), F32), pltpu.VMEM((SUBLANES, D), F32)],
        compiler_params=_params(("arbitrary", "arbitrary")),
        name="lru_prompt",
    )(x_p, g, win, cw, cb, wa, ba, wx, bx, lam, wout)


def _lru_sample_body(x_ref, g_ref, win_ref, cw_ref, cb_ref, wa_ref, ba_ref, wx_ref, bx_ref,
                     lam_ref, wout_ref, cs_ref, h0_ref, y_ref, conv_ref, h_ref,
                     xc_scr, *, Bs, Ls):
    Ts, D = x_ref.shape
    x = x_ref[...]
    hn = _rms(x, g_ref[...]).astype(BF16)
    gate = _gelu_tanh(_dot(hn, win_ref[:, :D]))
    x_br = _dot(hn, win_ref[:, D:])
    xc_scr[:, 0:SUBLANES, :] = cs_ref[...]
    xc_scr[:, SUBLANES:SUBLANES + Ls, :] = x_br.reshape(Bs, Ls, D)
    first = SUBLANES - (CONV_W - 1)
    conv = cb_ref[...] + xc_scr[:, first:first + Ls, :] * cw_ref[0:1, :]
    for j in range(1, CONV_W):
        conv = conv + xc_scr[:, first + j:first + j + Ls, :] * cw_ref[j:j + 1, :]
    conv = conv.reshape(Ts, D)
    a, mult, gi = _lru_gates(conv, wa_ref, wx_ref, ba_ref[...], bx_ref[...], lam_ref[...])
    bvec = mult * gi + a * h0_ref[...].reshape(Ts, D)
    hs = _seg_scan(a, bvec, Ls)
    y_ref[...] = x + _dot((hs * gate).astype(BF16), wout_ref[...])
    h_ref[...] = hs.reshape(Bs, Ls, D)[:, Ls - SUBLANES:Ls, :]
    conv_ref[...] = xc_scr[:, Ls:Ls + SUBLANES, :]


def _lru_sample(x_s, Bs, Ls, g, win, cw, cb, wa, ba, wx, bx, lam, wout, cs, h0):
    Ts, D = x_s.shape
    assert Ls % SUBLANES == 0
    body = functools.partial(_lru_sample_body, Bs=Bs, Ls=Ls)
    return pl.pallas_call(
        body,
        out_shape=(jax.ShapeDtypeStruct((Ts, D), F32),
                   jax.ShapeDtypeStruct((Bs, SUBLANES, D), F32),
                   jax.ShapeDtypeStruct((Bs, SUBLANES, D), F32)),
        grid=(1,),
        in_specs=[
            _const_spec((Ts, D)),
            _const_spec((1, D)), _const_spec(win.shape), _const_spec(cw.shape), _const_spec((1, D)),
            _const_spec(wa.shape), _const_spec((1, D)), _const_spec(wx.shape), _const_spec((1, D)),
            _const_spec((1, D)), _const_spec(wout.shape),
            _const_spec((Bs, SUBLANES, D)), _const_spec((Bs, Ls, D)),
        ],
        out_specs=(
            pl.BlockSpec((Ts, D), lambda i: (0, 0)),
            pl.BlockSpec((Bs, SUBLANES, D), lambda i: (0, 0, 0)),
            pl.BlockSpec((Bs, SUBLANES, D), lambda i: (0, 0, 0)),
        ),
        scratch_shapes=[pltpu.VMEM((Bs, Ls + SUBLANES, D), F32)],
        compiler_params=_params(("arbitrary",)),
        name="lru_sample",
    )(x_s, g, win, cw, cb, wa, ba, wx, bx, lam, wout, cs, h0)


def _router_body(x_ref, g_ref, wc_ref, bias_ref, cin_ref, text_ref, meta_ref, cnt_ref, run_scr,
                 tri_scr):
    i = pl.program_id(0)
    tm, D = x_ref.shape

    @pl.when(i == 0)
    def _():
        run_scr[...] = cin_ref[...]
        ri = lax.broadcasted_iota(jnp.int32, (tm, tm), 0)
        ci = lax.broadcasted_iota(jnp.int32, (tm, tm), 1)
        tri_scr[...] = (ci < ri).astype(BF16)

    t = _rms(x_ref[...], g_ref[...])
    logits = _dot(t.astype(BF16), wc_ref[...]) + bias_ref[...]
    lane_i = lax.broadcasted_iota(jnp.int32, (tm, LANES), 1)
    lane = lane_i.astype(F32)

    def first_argmax(v):
        m = jnp.max(v, axis=-1, keepdims=True)
        return m, jnp.min(jnp.where(v == m, lane, 1e6), axis=-1, keepdims=True)

    gmask = lane_i < MOE_GROUPS
    gl = jnp.where(gmask, logits, NEG_BIG)
    mg, g_idx = first_argmax(gl)
    p_g = 1.0 / jnp.sum(jnp.where(gmask, jnp.exp(gl - mg), 0.0), axis=-1, keepdims=True)
    e_lane = lane_i - MOE_GROUPS
    lane_grp = (e_lane >> 2).astype(F32)
    emask = (e_lane >= 0) & (e_lane < MOE_EXPERTS) & (lane_grp == g_idx)
    el = jnp.where(emask, logits, NEG_BIG)
    m1, i1 = first_argmax(el)
    m2, i2 = first_argmax(jnp.where(lane == i1, NEG_BIG, el))
    e21 = jnp.exp(m2 - m1)
    w1 = p_g / (1.0 + e21)
    w2 = p_g * e21 / (1.0 + e21)
    first_lower = i1 < i2
    j_lo = jnp.minimum(i1, i2) - (MOE_GROUPS + MOE_PER_GROUP * g_idx)
    j_hi = jnp.maximum(i1, i2) - (MOE_GROUPS + MOE_PER_GROUP * g_idx)
    pair = 0.5 * (j_lo * (7.0 - j_lo)) + (j_hi - j_lo - 1.0)
    cls = g_idx * N_PAIRS + pair
    w_lo = jnp.where(first_lower, w1, w2)
    w_hi = jnp.where(first_lower, w2, w1)

    onehot = (lane == cls).astype(F32)
    prefix = _dot(tri_scr[...], onehot.astype(BF16))
    rank = jnp.sum(onehot * (prefix + run_scr[...]), axis=-1, keepdims=True)
    run_scr[...] = run_scr[...] + jnp.sum(onehot, axis=0, keepdims=True)

    text_ref[:, :D] = t
    text_ref[:, D:] = jnp.where(lane_i == 0, w_lo, jnp.where(lane_i == 1, w_hi, 0.0))
    ones8 = jnp.ones((SUBLANES, LANES), BF16)
    ids8 = lax.broadcasted_iota(jnp.int32, (SUBLANES, LANES), 1).astype(F32).astype(BF16)
    d2 = jnp.floor(rank * (1.0 / 65536.0))
    rem = rank - 65536.0 * d2
    d1 = jnp.floor(rem * (1.0 / 256.0))
    d0 = rem - 256.0 * d1

    def as_row(col):
        return _dot_nt(ones8, jnp.where(lane_i == 0, col, 0.0).astype(BF16))

    cls_row = _dot_nt(ids8, onehot.astype(BF16))
    rank_row = as_row(d0) + 256.0 * as_row(d1) + 65536.0 * as_row(d2)
    sub = lax.broadcasted_iota(jnp.int32, (SUBLANES, tm), 0)
    meta_ref[...] = jnp.where(sub == 0, cls_row, jnp.where(sub == 1, rank_row, 0.0))
    cnt_ref[...] = run_scr[...]


def _router(x, g, wc, bias, cnt_in):
    T, D = x.shape
    tm = _pick_tile(T, ROUTE_TILE)
    return pl.pallas_call(
        _router_body,
        out_shape=(jax.ShapeDtypeStruct((T, D + EXTRA), F32),
                   jax.ShapeDtypeStruct((T // tm * SUBLANES, tm), F32),
                   jax.ShapeDtypeStruct((1, LANES), F32)),
        grid=(T // tm,),
        in_specs=[
            pl.BlockSpec((tm, D), lambda i: (i, 0)),
            _const_spec((1, D)), _const_spec(wc.shape), _const_spec((1, LANES)),
            _const_spec((1, LANES)),
        ],
        out_specs=(
            pl.BlockSpec((tm, D + EXTRA), lambda i: (i, 0)),
            pl.BlockSpec((SUBLANES, tm), lambda i: (i, 0)),
            pl.BlockSpec((1, LANES), lambda i: (0, 0)),
        ),
        scratch_shapes=[pltpu.VMEM((1, LANES), F32), pltpu.VMEM((tm, tm), BF16)],
        compiler_params=_params(("arbitrary",)),
        name="moe_router",
    )(x, g, wc, bias, cnt_in)


def _start_rows(n, idx_ref, row_copy):
    assert n % SUBLANES == 0

    def body(j, carry):
        for u in range(SUBLANES):
            pos = idx_ref[0, 0, j * SUBLANES + u]
            row_copy(j, u, pos >> SUBLANE_SHIFT, pos & (SUBLANES - 1)).start(priority=u % 2)
        return carry

    lax.fori_loop(0, n // SUBLANES, body, 0)


def _dispatch_body(zinfo_ref, dest_ref, tp_ref, ts_ref, xs_ref, tbuf, zbuf, in_sems, row_sems, zsem,
                   *, nbp, nbs, steps):
    i = pl.program_id(0)
    nb = nbp + nbs
    groups = tbuf.shape[1]
    zgroups = zbuf.shape[0]

    def zero_copies():
        out = []
        for c in range(N_CLASSES):
            r0, r1 = zinfo_ref[c], zinfo_ref[N_CLASSES + c]
            g0, g1 = r0 >> SUBLANE_SHIFT, r1 >> SUBLANE_SHIFT
            for k in range(SUBLANES - 1):
                out.append((k < r1 - g1 * SUBLANES, pltpu.make_async_copy(
                    zbuf.at[0, pl.ds(0, 1), :], xs_ref.at[g1, pl.ds(k, 1), :], zsem)))
            n = g1 - g0
            b = zgroups // 2
            while b >= 1:
                out.append(((n & b) != 0, pltpu.make_async_copy(
                    zbuf.at[pl.ds(0, b)], xs_ref.at[pl.ds(g0 + (n & -(2 * b)), b)], zsem)))
                b //= 2
        for k in range(N_CLASSES):
            step = zinfo_ref[2 * N_CLASSES] + k
            out.append((step < steps, pltpu.make_async_copy(
                zbuf, xs_ref.at[pl.ds(jnp.minimum(step, steps - 1) * zgroups, zgroups)], zsem)))
        return out

    def tile_copies(j, slot):
        jp = jnp.clip(j, 0, nbp - 1)
        js = jnp.clip(j - nbp, 0, nbs - 1)
        return [(j < nbp, pltpu.make_async_copy(tp_ref.at[pl.ds(jp * groups, groups)],
                                                tbuf.at[slot], in_sems.at[slot])),
                (j >= nbp, pltpu.make_async_copy(ts_ref.at[pl.ds(js * groups, groups)],
                                                 tbuf.at[slot], in_sems.at[slot]))]

    def start_all(copies):
        for pred, cp in copies:
            @pl.when(pred)
            def _():
                cp.start()

    def wait_all(copies):
        for pred, cp in copies:
            @pl.when(pred)
            def _():
                cp.wait()

    def wait_rows(slot):
        pltpu.make_async_copy(tbuf.at[slot], xs_ref.at[pl.ds(0, groups)], row_sems.at[slot]).wait()

    @pl.when(i == 0)
    def _():
        for j in range(min(DISPATCH_SLOTS - 1, nb)):
            start_all(tile_copies(jnp.int32(j), j))
        zbuf[...] = jnp.zeros_like(zbuf)
        start_all(zero_copies())

    for slot in range(DISPATCH_SLOTS):
        @pl.when(i % DISPATCH_SLOTS == slot)
        def _():
            freed = (slot + DISPATCH_SLOTS - 1) % DISPATCH_SLOTS
            wait_all(tile_copies(i, slot))

            def row_copy(j, u, hi, lo):
                return pltpu.make_async_copy(tbuf.at[slot, j, pl.ds(u, 1), :],
                                             xs_ref.at[hi, pl.ds(lo, 1), :], row_sems.at[slot])

            _start_rows(groups * SUBLANES, dest_ref, row_copy)

            @pl.when(i >= 1)
            def _():
                wait_rows(freed)

            @pl.when(i + DISPATCH_SLOTS - 1 < nb)
            def _():
                start_all(tile_copies(i + DISPATCH_SLOTS - 1, freed))

            @pl.when(i == nb - 1)
            def _():
                wait_rows(slot)

    @pl.when(i == nb - 1)
    def _():
        wait_all(zero_copies())


def _dispatch(t_p, t_s, dest3, zinfo, P):
    nb, _, td = dest3.shape
    W = t_p.shape[1]
    assert t_p.shape[0] % td == 0 and t_s.shape[0] % td == 0 and FFN_TILE % SUBLANES == 0
    nbp, nbs = t_p.shape[0] // td, t_s.shape[0] // td
    assert nb == nbp + nbs and nbp >= 1 and nbs >= 1
    zgroups = FFN_TILE // SUBLANES
    body = functools.partial(_dispatch_body, nbp=nbp, nbs=nbs, steps=P // FFN_TILE)
    grid_spec = pltpu.PrefetchScalarGridSpec(
        num_scalar_prefetch=1,
        grid=(nb,),
        in_specs=[
            pl.BlockSpec((1, 1, td), lambda i, z: (i, 0, 0), memory_space=pltpu.SMEM),
            pl.BlockSpec(memory_space=pl.ANY),
            pl.BlockSpec(memory_space=pl.ANY),
        ],
        out_specs=pl.BlockSpec(memory_space=pl.ANY),
        scratch_shapes=[pltpu.VMEM((DISPATCH_SLOTS, td // SUBLANES, SUBLANES, W), F32),
                        pltpu.VMEM((zgroups, SUBLANES, W), F32),
                        pltpu.SemaphoreType.DMA((DISPATCH_SLOTS,)),
                        pltpu.SemaphoreType.DMA((DISPATCH_SLOTS,)),
                        pltpu.SemaphoreType.DMA(())],
    )
    return pl.pallas_call(
        body,
        out_shape=jax.ShapeDtypeStruct((P // SUBLANES, SUBLANES, W), F32),
        grid_spec=grid_spec,
        compiler_params=_params(("arbitrary",)),
        name="moe_dispatch",
    )(zinfo, dest3, t_p.reshape(-1, SUBLANES, W), t_s.reshape(-1, SUBLANES, W))


def _ffn_body(ea_ref, eb_ref, halves_ref, xs_ref, wga_ref, wua_ref, wda_ref,
              wgb_ref, wub_ref, wdb_ref, ys_ref, wg_scr, wu_scr, wd_scr):
    i = pl.program_id(0)
    tg, D = ys_ref.shape
    nh = halves_ref[i]
    prev = jnp.maximum(i - 1, 0)
    new_class = (i == 0) | (ea_ref[i] != ea_ref[prev]) | (eb_ref[i] != eb_ref[prev])

    @pl.when((nh > 0) & new_class)
    def _():
        for slot, (wg, wu, wd) in enumerate(((wga_ref, wua_ref, wda_ref), (wgb_ref, wub_ref, wdb_ref))):
            wg_scr[slot] = wg[0, 0].astype(BF16)
            wu_scr[slot] = wu[0, 0].astype(BF16)
            wd_scr[slot] = wd[0, 0].astype(BF16)

    def experts(rows):
        x = xs_ref[rows, :D].astype(BF16)
        w_a = xs_ref[rows, D:D + 1]
        w_b = xs_ref[rows, D + 1:D + 2]

        def expert(slot):
            gt = _dot(x, wg_scr[slot])
            hdn = (gt * jax.nn.sigmoid(gt)) * _dot(x, wu_scr[slot])
            return _dot(hdn.astype(BF16), wd_scr[slot])

        return w_a * expert(0) + w_b * expert(1)

    @pl.when(nh == 2)
    def _():
        ys_ref[...] = experts(slice(0, tg))

    @pl.when(nh == 1)
    def _():
        ys_ref[:tg - FFN_HALF, :] = jnp.zeros((tg - FFN_HALF, D), F32)
        ys_ref[tg - FFN_HALF:, :] = experts(slice(tg - FFN_HALF, tg))

    @pl.when(nh == 0)
    def _():
        ys_ref[...] = jnp.zeros_like(ys_ref)


def _ffn(xs, ea, eb, halves, layer, wg, wu, wd):
    P, W = xs.shape
    D = W - EXTRA
    ff = wg.shape[3]
    tg = FFN_TILE
    wspec_a = lambda shape: pl.BlockSpec(shape, lambda i, ea, eb, nh: (layer, ea[i], 0, 0))
    wspec_b = lambda shape: pl.BlockSpec(shape, lambda i, ea, eb, nh: (layer, eb[i], 0, 0))
    grid_spec = pltpu.PrefetchScalarGridSpec(
        num_scalar_prefetch=3,
        grid=(P // tg,),
        in_specs=[
            pl.BlockSpec((tg, W), lambda i, ea, eb, nh: (i, 0)),
            wspec_a((1, 1, D, ff)), wspec_a((1, 1, D, ff)), wspec_a((1, 1, ff, D)),
            wspec_b((1, 1, D, ff)), wspec_b((1, 1, D, ff)), wspec_b((1, 1, ff, D)),
        ],
        out_specs=pl.BlockSpec((tg, D), lambda i, ea, eb, nh: (i, 0)),
        scratch_shapes=[pltpu.VMEM((2, D, ff), BF16), pltpu.VMEM((2, D, ff), BF16),
                        pltpu.VMEM((2, ff, D), BF16)],
    )
    return pl.pallas_call(
        _ffn_body,
        out_shape=jax.ShapeDtypeStruct((P, D), F32),
        grid_spec=grid_spec,
        compiler_params=_params(("arbitrary",)),
        name="moe_ffn",
    )(ea, eb, halves, xs, wg, wu, wd, wg, wu, wd)


def _combine_body(dest_ref, dnext_ref, x_ref, gfin_ref, ys_ref, o_ref, ybuf, sems, *, final_norm):
    i, n = pl.program_id(0), pl.num_programs(0)
    tc, D = x_ref.shape
    groups = tc // SUBLANES

    def gather(idx_ref, slot):
        def row_copy(j, u, hi, lo):
            return pltpu.make_async_copy(ys_ref.at[hi, pl.ds(lo, 1), :],
                                         ybuf.at[slot, j, pl.ds(u, 1), :], sems.at[slot])
        _start_rows(tc, idx_ref, row_copy)

    @pl.when(i == 0)
    def _():
        gather(dest_ref, 0)

    for slot in range(2):
        @pl.when((i + 1 < n) & ((i + 1) % 2 == slot))
        def _():
            gather(dnext_ref, slot)

    for slot in range(2):
        @pl.when(i % 2 == slot)
        def _():
            pltpu.make_async_copy(ys_ref.at[pl.ds(0, groups)], ybuf.at[slot], sems.at[slot]).wait()
            y = x_ref[...] + ybuf[slot].reshape(tc, D)
            if final_norm:
                y = _rms(y, gfin_ref[...])
            o_ref[...] = y


def _combine(x, dest3, ys, gfin, final_norm):
    T, D = x.shape
    nb, _, tc = dest3.shape
    body = functools.partial(_combine_body, final_norm=final_norm)
    return pl.pallas_call(
        body,
        out_shape=jax.ShapeDtypeStruct((T, D), F32),
        grid=(nb,),
        in_specs=[
            pl.BlockSpec((1, 1, tc), lambda i: (i, 0, 0), memory_space=pltpu.SMEM),
            pl.BlockSpec((1, 1, tc), lambda i: (jnp.minimum(i + 1, nb - 1), 0, 0),
                         memory_space=pltpu.SMEM),
            pl.BlockSpec((tc, D), lambda i: (i, 0)),
            _const_spec((1, D)),
            pl.BlockSpec(memory_space=pl.ANY),
        ],
        out_specs=pl.BlockSpec((tc, D), lambda i: (i, 0)),
        scratch_shapes=[pltpu.VMEM((2, tc // SUBLANES, SUBLANES, D), F32),
                        pltpu.SemaphoreType.DMA((2,))],
        compiler_params=_params(("arbitrary",)),
        name="moe_combine",
    )(dest3, dest3, x, gfin, ys.reshape(-1, SUBLANES, D))


def _moe_tables(metas, counts, P):
    tg = FFN_TILE
    cnt = counts[0, :N_CLASSES].astype(jnp.int32)
    padded = ((cnt + tg - 1) // tg) * tg
    ends = jnp.cumsum(padded)
    offs = ends - padded
    front = padded - cnt
    dests = []
    for m in metas:
        m = m.reshape(-1, SUBLANES, m.shape[1])
        cls, rank = m[:, 0, :].reshape(-1).astype(jnp.int32), m[:, 1, :].reshape(-1).astype(jnp.int32)
        base = jnp.sum(jnp.where(cls[:, None] == jnp.arange(N_CLASSES, dtype=jnp.int32)[None, :],
                                 (offs + front)[None, :], 0), axis=1)
        dests.append(base + rank)
    tile_start = jnp.arange(P // tg, dtype=jnp.int32) * tg
    tile_cls = jnp.minimum(jnp.sum((tile_start[:, None] >= ends[None, :]).astype(jnp.int32), axis=1),
                           N_CLASSES - 1)
    rows = tg - jnp.clip(jnp.take(offs + front, tile_cls) - tile_start, 0, tg)
    halves = jnp.where(tile_start < ends[-1], (rows + FFN_HALF - 1) // FFN_HALF, 0).astype(jnp.int32)
    grp = tile_cls // N_PAIRS
    pair = tile_cls % N_PAIRS
    ea = grp * MOE_PER_GROUP + jnp.take(jnp.asarray(_PAIR_LO), pair)
    eb = grp * MOE_PER_GROUP + jnp.take(jnp.asarray(_PAIR_HI), pair)
    zinfo = jnp.concatenate([offs, offs + front, ends[-1:] // tg]).astype(jnp.int32)
    return dests, ea, eb, halves, zinfo


def _moe(xs_in, g, w_group, b_group, w_router, b_router, layer, wg, wu, wd, gfin, final_norm):
    assert len(xs_in) == 2
    D = xs_in[0].shape[1]
    T = sum(x.shape[0] for x in xs_in)
    wcat = jnp.zeros((D, LANES), F32)
    wcat = wcat.at[:, :MOE_GROUPS].set(w_group).at[:, MOE_GROUPS:MOE_GROUPS + MOE_EXPERTS].set(w_router)
    bias = jnp.zeros((1, LANES), F32)
    bias = bias.at[0, :MOE_GROUPS].set(b_group).at[0, MOE_GROUPS:MOE_GROUPS + MOE_EXPERTS].set(b_router)
    wc = wcat.astype(BF16)

    counts = jnp.zeros((1, LANES), F32)
    texts, metas = [], []
    for x in xs_in:
        t_ext, meta, counts = _router(x, g, wc, bias, counts)
        texts.append(t_ext)
        metas.append(meta)

    tg = FFN_TILE
    P = ((T + tg - 1) // tg + N_CLASSES) * tg
    dests, ea, eb, halves, zinfo = _moe_tables(metas, counts, P)
    tmove = _pick_tile(math.gcd(*(x.shape[0] for x in xs_in)), MOVE_TILE)
    dest3 = jnp.concatenate(dests).reshape(-1, 1, tmove)
    xs = _dispatch(texts[0], texts[1], dest3, zinfo, P)
    ys = _ffn(xs.reshape(P, D + EXTRA), ea, eb, halves, layer, wg, wu, wd)
    return [_combine(x, dest.reshape(-1, 1, _pick_tile(x.shape[0], COMBINE_TILE)), ys, gfin, final_norm)
            for x, dest in zip(xs_in, dests)]


def _rope_tables(pos, dk):
    half = dk // 2
    inv = 1.0 / (ROPE_BASE ** jnp.linspace(0.0, 1.0, half, dtype=F32))
    ang = pos.astype(F32)[:, None] * inv[None, :]
    return jnp.cos(ang), jnp.sin(ang)


def kernel(x_prompt, x_sample, state_ret, state_conv, state_lru, norm_mix_g, norm_ffn_g, norm_final_g, ret_w_in, ret_w_out, ret_norm_g, lru_w_in, lru_conv_w, lru_conv_b, lru_w_a, lru_b_a, lru_w_x, lru_b_x, lru_lambda, lru_w_out, moe_w_group, moe_b_group, moe_w_router, moe_b_router, moe_w_gate, moe_w_up, moe_w_down):
    B, S, D = x_prompt.shape
    Bs, Ls, _ = x_sample.shape
    Tp, Ts = B * S, Bs * Ls
    assert norm_mix_g.shape[0] == 2
    dk = D // RET_HEADS
    row = lambda v: v.reshape(1, -1)
    gfin = row(norm_final_g)

    cos_p, sin_p = _rope_tables(jnp.arange(S), dk)
    cos_s, sin_s = _rope_tables(PAST_LEN + jnp.arange(Ls), dk)
    ret_args = (row(norm_mix_g[0]), ret_w_in[0].astype(BF16), ret_w_out[0].astype(BF16),
                row(ret_norm_g[0]))
    xp, ret_p = _ret_prompt(x_prompt.reshape(Tp, D), B, S, *ret_args, cos_p, sin_p)
    xs, ret_s = _ret_sample(x_sample.reshape(Ts, D), Bs, Ls, *ret_args, cos_s, sin_s, state_ret[0])
    xp, xs = _moe([xp, xs], row(norm_ffn_g[0]), moe_w_group[0], moe_b_group[0], moe_w_router[0],
                  moe_b_router[0], 0, moe_w_gate, moe_w_up, moe_w_down, gfin, False)

    lru_args = (row(norm_mix_g[1]), lru_w_in[0].astype(BF16), lru_conv_w[0], row(lru_conv_b[0]),
                lru_w_a[0].astype(BF16), row(lru_b_a[0]), lru_w_x[0].astype(BF16), row(lru_b_x[0]),
                row(lru_lambda[0]), lru_w_out[0].astype(BF16))
    cs8 = jnp.pad(state_conv[0], ((0, 0), (SUBLANES - (CONV_W - 1), 0), (0, 0)))
    h0_rows = jnp.pad(state_lru[0].reshape(Bs, 1, D), ((0, 0), (0, Ls - 1), (0, 0)))
    xp, conv_p, lru_p = _lru_prompt(xp, B, S, *lru_args)
    xs, conv_s, lru_s = _lru_sample(xs, Bs, Ls, *lru_args, cs8, h0_rows)
    y_p, y_s = _moe([xp, xs], row(norm_ffn_g[1]), moe_w_group[1], moe_b_group[1], moe_w_router[1],
                    moe_b_router[1], 1, moe_w_gate, moe_w_up, moe_w_down, gfin, True)

    return (y_p.reshape(B, S, D), y_s.reshape(Bs, Ls, D),
            ret_p[None], conv_p[None, :, SUBLANES - (CONV_W - 1):], lru_p[None, :, SUBLANES - 1],
            ret_s[None], conv_s[None, :, SUBLANES - (CONV_W - 1):], lru_s[None, :, SUBLANES - 1])
```

```python
import functools
import math

import numpy as np
import jax
import jax.numpy as jnp
from jax import lax
from jax.experimental import pallas as pl
from jax.experimental.pallas import tpu as pltpu

F32 = jnp.float32
BF16 = jnp.bfloat16

RET_HEADS = 4
RET_CHUNK = 64
ROPE_BASE = 10000.0
CONV_W = 4
LRU_BLOCKS = 4
LRU_C = 8.0
MOE_GROUPS = 4
MOE_PER_GROUP = 4
MOE_EXPERTS = MOE_GROUPS * MOE_PER_GROUP
NORM_EPS = 1e-6
PAST_LEN = 2048

LANES = 128
SUBLANES = 8
SUBLANE_SHIFT = 3
VMEM_LIMIT_BYTES = 56 * 1024 * 1024

RET_TILE = 512
RET_SCORE_BLOCK = 256
RET_SAMPLE_STREAMS = 2
LRU_TILE = 512
ROUTE_TILE = 1024
MOVE_TILE = 512
DISPATCH_SLOTS = 3
ROW_GROUPS_PER_ITER = 4
COMBINE_TILE = 1024
FFN_TILE = 512
FFN_HALF = 256
N_PAIRS = 6
N_CLASSES = MOE_GROUPS * N_PAIRS
EXTRA = LANES
NEG_BIG = -1e30

_PAIR_LO = np.array([0, 0, 0, 1, 1, 2], np.int32)
_PAIR_HI = np.array([1, 2, 3, 2, 3, 3], np.int32)


def _log_gammas():
    return [math.log1p(-(2.0 ** (-5.0 - h))) for h in range(RET_HEADS)]


def _rms(x, g):
    return x * lax.rsqrt(jnp.mean(x * x, axis=-1, keepdims=True) + NORM_EPS) * g


def _dot(a, b):
    return jnp.dot(a, b, preferred_element_type=F32)


def _dot_nt(a, b):
    return lax.dot_general(a, b, (((1,), (1,)), ((), ())), preferred_element_type=F32)


def _dot_tn(a, b):
    return lax.dot_general(a, b, (((0,), (0,)), ((), ())), preferred_element_type=F32)


def _const_spec(shape):
    nd = len(shape)
    return pl.BlockSpec(shape, lambda *_: (0,) * nd, pipeline_mode=pl.Buffered(1))


def _params(sem):
    return pltpu.CompilerParams(dimension_semantics=sem, vmem_limit_bytes=VMEM_LIMIT_BYTES)


def _pick_tile(n, pref):
    t = pref
    while n % t:
        t //= 2
    assert t >= SUBLANES, (n, pref)
    return t


def _decay_matrix(L, chunk, lg):
    ii = lax.broadcasted_iota(jnp.int32, (L, L), 0)
    jj = lax.broadcasted_iota(jnp.int32, (L, L), 1)
    diff = (ii - jj).astype(F32)
    same_chunk = (ii // chunk) == (jj // chunk)
    return jnp.where(same_chunk & (ii >= jj), jnp.exp(jnp.maximum(diff, 0.0) * lg), 0.0)


def _rotary(t, cos, sin):
    half = t.shape[-1] // 2
    t1, t2 = t[:, :half], t[:, half:]
    return jnp.concatenate([t1 * cos - t2 * sin, t1 * sin + t2 * cos], axis=-1)


def _ret_proj(hn, win_ref, hd, dk, dv):
    qk, vv = RET_HEADS * dk, RET_HEADS * dv
    q = _dot(hn, win_ref[:, hd * dk:(hd + 1) * dk])
    k = _dot(hn, win_ref[:, qk + hd * dk:qk + (hd + 1) * dk])
    v = _dot(hn, win_ref[:, 2 * qk + hd * dv:2 * qk + (hd + 1) * dv])
    gt = _dot(hn, win_ref[:, 2 * qk + vv + hd * dv:2 * qk + vv + (hd + 1) * dv])
    return q, k, v, gt


def _ret_head(q, k, v, gt, cos, sin, decay, s0, ng, chunk, lg, kv_scr=None):
    L, dk = q.shape
    qr = _rotary(q, cos, sin)
    kr = _rotary(k, cos, sin) * (dk ** -0.5)
    idx = (lax.broadcasted_iota(jnp.int32, (L, 1), 0) % chunk).astype(F32)
    qd = (qr * jnp.exp((idx + 1.0) * lg)).astype(BF16)
    kd = (kr * jnp.exp((chunk - 1.0 - idx) * lg)).astype(BF16)
    qb, kb, vb = qr.astype(BF16), kr.astype(BF16), v.astype(BF16)
    core = decay.shape[0]
    chunks = [slice(c * chunk, (c + 1) * chunk) for c in range(L // chunk)]
    if kv_scr is not None:
        for c, rows in enumerate(chunks):
            kv_scr[c] = _dot_tn(kd[rows], vb[rows])
    s, parts = s0, []
    for r0 in range(0, L, core):
        blk = slice(r0, r0 + core)
        scores = _dot_nt(qb[blk], kb[blk]) * decay
        intra = _dot(scores.astype(BF16), vb[blk])
        for c in range(r0 // chunk, (r0 + core) // chunk):
            rows = chunks[c]
            parts.append(intra[rows.start - r0:rows.stop - r0] + _dot(qd[rows], s.astype(BF16)))
            kv = kv_scr[c] if kv_scr is not None else _dot_tn(kd[rows], vb[rows])
            s = math.exp(chunk * lg) * s + kv
    o = parts[0] if len(parts) == 1 else jnp.concatenate(parts, axis=0)
    o = o * lax.rsqrt(jnp.mean(o * o, axis=-1, keepdims=True) + NORM_EPS)
    og = (gt * jax.nn.sigmoid(gt)) * (o * ng)
    return og, s


def _ret_prompt_body(x_ref, g_ref, win_ref, wout_ref, ng_ref, cos_ref, sin_ref,
                     y_ref, s_ref, decay_scr, og_scr, kv_scr, *, L, chunk, dk, dv):
    b, c = pl.program_id(0), pl.program_id(1)
    lgs = _log_gammas()

    @pl.when((b == 0) & (c == 0))
    def _():
        for hd in range(RET_HEADS):
            decay_scr[hd] = _decay_matrix(decay_scr.shape[1], chunk, lgs[hd])

    @pl.when(c == 0)
    def _():
        s_ref[...] = jnp.zeros_like(s_ref)

    x = x_ref[...]
    hn = _rms(x, g_ref[...]).astype(BF16)
    cos, sin = cos_ref[...], sin_ref[...]
    for hd in range(RET_HEADS):
        og, s_new = _ret_head(*_ret_proj(hn, win_ref, hd, dk, dv), cos, sin, decay_scr[hd],
                              s_ref[0, hd], ng_ref[:, hd * dv:(hd + 1) * dv], chunk, lgs[hd],
                              kv_scr)
        s_ref[0, hd] = s_new
        og_scr[:, hd * dv:(hd + 1) * dv] = og.astype(BF16)
    y_ref[...] = x + _dot(og_scr[...], wout_ref[...])


def _ret_prompt(x_p, B, S, g, win, wout, ng, cos, sin):
    Tp, D = x_p.shape
    dk, dv = D // RET_HEADS, 2 * D // RET_HEADS
    chunk = min(S, RET_CHUNK)
    L = _pick_tile(S, RET_TILE)
    core = min(L, RET_SCORE_BLOCK)
    assert L % core == 0 and core % chunk == 0
    nc = S // L
    body = functools.partial(_ret_prompt_body, L=L, chunk=chunk, dk=dk, dv=dv)
    return pl.pallas_call(
        body,
        out_shape=(jax.ShapeDtypeStruct((Tp, D), F32),
                   jax.ShapeDtypeStruct((B, RET_HEADS, dk, dv), F32)),
        grid=(B, nc),
        in_specs=[
            pl.BlockSpec((L, D), lambda b, c: (b * nc + c, 0)),
            _const_spec((1, D)),
            _const_spec(win.shape),
            _const_spec(wout.shape),
            _const_spec((1, RET_HEADS * dv)),
            pl.BlockSpec((L, dk // 2), lambda b, c: (c, 0)),
            pl.BlockSpec((L, dk // 2), lambda b, c: (c, 0)),
        ],
        out_specs=(
            pl.BlockSpec((L, D), lambda b, c: (b * nc + c, 0)),
            pl.BlockSpec((1, RET_HEADS, dk, dv), lambda b, c: (b, 0, 0, 0)),
        ),
        scratch_shapes=[pltpu.VMEM((RET_HEADS, core, core), F32),
                        pltpu.VMEM((L, RET_HEADS * dv), BF16),
                        pltpu.VMEM((L // chunk, dk, dv), F32)],
        compiler_params=_params(("arbitrary", "arbitrary")),
        name="ret_prompt",
    )(x_p, g, win, wout, ng, cos, sin)


def _ret_sample_body(x_ref, g_ref, win_ref, wout_ref, ng_ref, cos_ref, sin_ref, s0_ref,
                     y_ref, s_ref, q_scr, k_scr, v_scr, gt_scr, og_scr, *, L, dk, dv):
    b, nb = pl.program_id(0), pl.num_programs(0)
    per = s0_ref.shape[0]
    lgs = _log_gammas()

    @pl.when(b == 0)
    def _():
        hn = _rms(x_ref[...], g_ref[...]).astype(BF16)
        for hd in range(RET_HEADS):
            q, k, v, gt = _ret_proj(hn, win_ref, hd, dk, dv)
            q_scr[hd], k_scr[hd], v_scr[hd], gt_scr[hd] = q, k, v, gt

    cos, sin = cos_ref[...], sin_ref[...]
    for j in range(per):
        rows = pl.ds(pl.multiple_of((b * per + j) * L, L), L)
        for hd in range(RET_HEADS):
            og, s_new = _ret_head(q_scr[hd, rows, :], k_scr[hd, rows, :], v_scr[hd, rows, :],
                                  gt_scr[hd, rows, :], cos, sin, _decay_matrix(L, L, lgs[hd]),
                                  s0_ref[j, hd], ng_ref[:, hd * dv:(hd + 1) * dv], L, lgs[hd])
            s_ref[j, hd] = s_new
            og_scr[rows, hd * dv:(hd + 1) * dv] = og.astype(BF16)

    @pl.when(b == nb - 1)
    def _():
        y_ref[...] = x_ref[...] + _dot(og_scr[...], wout_ref[...])


def _ret_sample(x_s, Bs, Ls, g, win, wout, ng, cos, sin, s0):
    Ts, D = x_s.shape
    dk, dv = D // RET_HEADS, 2 * D // RET_HEADS
    assert Ls <= RET_CHUNK
    per = RET_SAMPLE_STREAMS if Bs % RET_SAMPLE_STREAMS == 0 else 1
    body = functools.partial(_ret_sample_body, L=Ls, dk=dk, dv=dv)
    return pl.pallas_call(
        body,
        out_shape=(jax.ShapeDtypeStruct((Ts, D), F32),
                   jax.ShapeDtypeStruct((Bs, RET_HEADS, dk, dv), F32)),
        grid=(Bs // per,),
        in_specs=[
            _const_spec((Ts, D)),
            _const_spec((1, D)),
            _const_spec(win.shape),
            _const_spec(wout.shape),
            _const_spec((1, RET_HEADS * dv)),
            _const_spec((Ls, dk // 2)),
            _const_spec((Ls, dk // 2)),
            pl.BlockSpec((per, RET_HEADS, dk, dv), lambda b: (b, 0, 0, 0)),
        ],
        out_specs=(
            pl.BlockSpec((Ts, D), lambda b: (0, 0)),
            pl.BlockSpec((per, RET_HEADS, dk, dv), lambda b: (b, 0, 0, 0)),
        ),
        scratch_shapes=[pltpu.VMEM((RET_HEADS, Ts, dk), F32), pltpu.VMEM((RET_HEADS, Ts, dk), F32),
                        pltpu.VMEM((RET_HEADS, Ts, dv), F32), pltpu.VMEM((RET_HEADS, Ts, dv), F32),
                        pltpu.VMEM((Ts, RET_HEADS * dv), BF16)],
        compiler_params=_params(("arbitrary",)),
        name="ret_sample",
    )(x_s, g, win, wout, ng, cos, sin, s0)


def _gelu_tanh(x):
    return 0.5 * x * (1.0 + jnp.tanh(math.sqrt(2.0 / math.pi) * (x + 0.044715 * (x * x * x))))


def _softplus(z):
    return jnp.maximum(z, 0.0) + jnp.log1p(jnp.exp(-jnp.abs(z)))


def _lru_gates(conv, wa_ref, wx_ref, ba, bx, lam):
    D = conv.shape[1]
    bw = D // LRU_BLOCKS
    ra, ix = [], []
    for n in range(LRU_BLOCKS):
        cb = conv[:, n * bw:(n + 1) * bw].astype(BF16)
        ra.append(_dot(cb, wa_ref[n]))
        ix.append(_dot(cb, wx_ref[n]))
    r = jax.nn.sigmoid(jnp.concatenate(ra, axis=-1) + ba)
    i = jax.nn.sigmoid(jnp.concatenate(ix, axis=-1) + bx)
    log_a = r * ((-LRU_C) * _softplus(-lam))
    a = jnp.exp(log_a)
    m2 = jnp.tanh(-log_a) * (1.0 + a * a)
    mult = jnp.where(m2 > 0.0, m2 * lax.rsqrt(m2), 0.0)
    return a, mult, i * conv


def _seg_scan(a, b, seg):
    R = a.shape[0]
    row = lax.broadcasted_iota(jnp.int32, (R, 1), 0) % seg
    k = 1
    while k < seg:
        keep = row >= k
        a_sh = jnp.where(keep, pltpu.roll(a, k, axis=0), 1.0)
        b_sh = jnp.where(keep, pltpu.roll(b, k, axis=0), 0.0)
        b = a * b_sh + b
        a = a * a_sh
        k *= 2
    return b


def _carry_scan(a, b, h_prev):
    L, D = a.shape
    G = L // SUBLANES
    a3, b3 = a.reshape(G, SUBLANES, D), b.reshape(G, SUBLANES, D)
    sub = lax.broadcasted_iota(jnp.int32, (SUBLANES, D), 0)[None]
    k = 1
    while k < SUBLANES:
        keep = sub >= k
        a_sh = jnp.where(keep, pltpu.roll(a3, k, axis=1), 1.0)
        b_sh = jnp.where(keep, pltpu.roll(b3, k, axis=1), 0.0)
        b3 = a3 * b_sh + b3
        a3 = a3 * a_sh
        k *= 2
    carry, outs = h_prev, []
    for j in range(G):
        r = a3[j] * carry + b3[j]
        outs.append(r)
        carry = r[SUBLANES - 1:SUBLANES, :]
    return jnp.concatenate(outs, axis=0)


def _lru_prompt_body(x_ref, g_ref, win_ref, cw_ref, cb_ref, wa_ref, ba_ref, wx_ref, bx_ref,
                     lam_ref, wout_ref, y_ref, conv_ref, h_ref, xc_scr, hprev_scr, *, L):
    c = pl.program_id(1)
    D = x_ref.shape[1]

    @pl.when(c == 0)
    def _():
        xc_scr[...] = jnp.zeros_like(xc_scr)
        hprev_scr[...] = jnp.zeros_like(hprev_scr)

    x = x_ref[...]
    hn = _rms(x, g_ref[...]).astype(BF16)
    gate = _gelu_tanh(_dot(hn, win_ref[:, :D]))
    x_br = _dot(hn, win_ref[:, D:])
    prev8 = xc_scr[...]
    sub8 = lax.broadcasted_iota(jnp.int32, (SUBLANES, 1), 0)

    def shifted(k):
        rolled = pltpu.roll(x_br, k, axis=0)
        first = jnp.where(sub8 < k, pltpu.roll(prev8, k, axis=0), rolled[0:SUBLANES, :])
        return jnp.concatenate([first, rolled[SUBLANES:, :]], axis=0)

    conv = cb_ref[...] + shifted(CONV_W - 1) * cw_ref[0:1, :]
    for j in range(1, CONV_W - 1):
        conv = conv + shifted(CONV_W - 1 - j) * cw_ref[j:j + 1, :]
    conv = conv + x_br * cw_ref[CONV_W - 1:CONV_W, :]
    a, mult, gi = _lru_gates(conv, wa_ref, wx_ref, ba_ref[...], bx_ref[...], lam_ref[...])
    first = jnp.where((sub8 == 0) & (c == 0), 1.0, mult[0:SUBLANES, :])
    mult = jnp.concatenate([first, mult[SUBLANES:, :]], axis=0)
    hs = _carry_scan(a, mult * gi, hprev_scr[SUBLANES - 1:SUBLANES, :])
    y_ref[...] = x + _dot((hs * gate).astype(BF16), wout_ref[...])
    hprev_scr[...] = hs[L - SUBLANES:L, :]
    h_ref[0] = hs[L - SUBLANES:L, :]
    conv_ref[0] = x_br[L - SUBLANES:L, :]
    xc_scr[...] = x_br[L - SUBLANES:L, :]


def _lru_prompt(x_p, B, S, g, win, cw, cb, wa, ba, wx, bx, lam, wout):
    Tp, D = x_p.shape
    L = _pick_tile(S, LRU_TILE)
    nc = S // L
    body = functools.partial(_lru_prompt_body, L=L)
    return pl.pallas_call(
        body,
        out_shape=(jax.ShapeDtypeStruct((Tp, D), F32),
                   jax.ShapeDtypeStruct((B, SUBLANES, D), F32),
                   jax.ShapeDtypeStruct((B, SUBLANES, D), F32)),
        grid=(B, nc),
        in_specs=[
            pl.BlockSpec((L, D), lambda b, c: (b * nc + c, 0)),
            _const_spec((1, D)), _const_spec(win.shape), _const_spec(cw.shape), _const_spec((1, D)),
            _const_spec(wa.shape), _const_spec((1, D)), _const_spec(wx.shape), _const_spec((1, D)),
            _const_spec((1, D)), _const_spec(wout.shape),
        ],
        out_specs=(
            pl.BlockSpec((L, D), lambda b, c: (b * nc + c, 0)),
            pl.BlockSpec((1, SUBLANES, D), lambda b, c: (b, 0, 0)),
            pl.BlockSpec((1, SUBLANES, D), lambda b, c: (b, 0, 0)),
        ),
        scratch_shapes=[pltpu.VMEM((SUBLANES, D), F32), pltpu.VMEM((SUBLANES, D), F32)],
        compiler_params=_params(("arbitrary", "arbitrary")),
        name="lru_prompt",
    )(x_p, g, win, cw, cb, wa, ba, wx, bx, lam, wout)


def _lru_sample_body(x_ref, g_ref, win_ref, cw_ref, cb_ref, wa_ref, ba_ref, wx_ref, bx_ref,
                     lam_ref, wout_ref, cs_ref, h0_ref, y_ref, conv_ref, h_ref,
                     xc_scr, *, Bs, Ls):
    Ts, D = x_ref.shape
    x = x_ref[...]
    hn = _rms(x, g_ref[...]).astype(BF16)
    gate = _gelu_tanh(_dot(hn, win_ref[:, :D]))
    x_br = _dot(hn, win_ref[:, D:])
    xc_scr[:, 0:SUBLANES, :] = cs_ref[...]
    xc_scr[:, SUBLANES:SUBLANES + Ls, :] = x_br.reshape(Bs, Ls, D)
    first = SUBLANES - (CONV_W - 1)
    conv = cb_ref[...] + xc_scr[:, first:first + Ls, :] * cw_ref[0:1, :]
    for j in range(1, CONV_W):
        conv = conv + xc_scr[:, first + j:first + j + Ls, :] * cw_ref[j:j + 1, :]
    conv = conv.reshape(Ts, D)
    a, mult, gi = _lru_gates(conv, wa_ref, wx_ref, ba_ref[...], bx_ref[...], lam_ref[...])
    bvec = mult * gi + a * h0_ref[...].reshape(Ts, D)
    hs = _seg_scan(a, bvec, Ls)
    y_ref[...] = x + _dot((hs * gate).astype(BF16), wout_ref[...])
    h_ref[...] = hs.reshape(Bs, Ls, D)[:, Ls - SUBLANES:Ls, :]
    conv_ref[...] = xc_scr[:, Ls:Ls + SUBLANES, :]


def _lru_sample(x_s, Bs, Ls, g, win, cw, cb, wa, ba, wx, bx, lam, wout, cs, h0):
    Ts, D = x_s.shape
    assert Ls % SUBLANES == 0
    body = functools.partial(_lru_sample_body, Bs=Bs, Ls=Ls)
    return pl.pallas_call(
        body,
        out_shape=(jax.ShapeDtypeStruct((Ts, D), F32),
                   jax.ShapeDtypeStruct((Bs, SUBLANES, D), F32),
                   jax.ShapeDtypeStruct((Bs, SUBLANES, D), F32)),
        grid=(1,),
        in_specs=[
            _const_spec((Ts, D)),
            _const_spec((1, D)), _const_spec(win.shape), _const_spec(cw.shape), _const_spec((1, D)),
            _const_spec(wa.shape), _const_spec((1, D)), _const_spec(wx.shape), _const_spec((1, D)),
            _const_spec((1, D)), _const_spec(wout.shape),
            _const_spec((Bs, SUBLANES, D)), _const_spec((Bs, Ls, D)),
        ],
        out_specs=(
            pl.BlockSpec((Ts, D), lambda i: (0, 0)),
            pl.BlockSpec((Bs, SUBLANES, D), lambda i: (0, 0, 0)),
            pl.BlockSpec((Bs, SUBLANES, D), lambda i: (0, 0, 0)),
        ),
        scratch_shapes=[pltpu.VMEM((Bs, Ls + SUBLANES, D), F32)],
        compiler_params=_params(("arbitrary",)),
        name="lru_sample",
    )(x_s, g, win, cw, cb, wa, ba, wx, bx, lam, wout, cs, h0)


def _router_body(x_ref, g_ref, wc_ref, bias_ref, cin_ref, text_ref, meta_ref, cnt_ref, run_scr,
                 tri_scr):
    i = pl.program_id(0)
    tm, D = x_ref.shape

    @pl.when(i == 0)
    def _():
        run_scr[...] = cin_ref[...]
        ri = lax.broadcasted_iota(jnp.int32, (tm, tm), 0)
        ci = lax.broadcasted_iota(jnp.int32, (tm, tm), 1)
        tri_scr[...] = (ci < ri).astype(BF16)

    t = _rms(x_ref[...], g_ref[...])
    logits = _dot(t.astype(BF16), wc_ref[...]) + bias_ref[...]
    lane_i = lax.broadcasted_iota(jnp.int32, (tm, LANES), 1)
    lane = lane_i.astype(F32)

    def first_argmax(v):
        m = jnp.max(v, axis=-1, keepdims=True)
        return m, jnp.min(jnp.where(v == m, lane, 1e6), axis=-1, keepdims=True)

    gmask = lane_i < MOE_GROUPS
    gl = jnp.where(gmask, logits, NEG_BIG)
    mg, g_idx = first_argmax(gl)
    p_g = 1.0 / jnp.sum(jnp.where(gmask, jnp.exp(gl - mg), 0.0), axis=-1, keepdims=True)
    e_lane = lane_i - MOE_GROUPS
    lane_grp = (e_lane >> 2).astype(F32)
    emask = (e_lane >= 0) & (e_lane < MOE_EXPERTS) & (lane_grp == g_idx)
    el = jnp.where(emask, logits, NEG_BIG)
    m1, i1 = first_argmax(el)
    m2, i2 = first_argmax(jnp.where(lane == i1, NEG_BIG, el))
    e21 = jnp.exp(m2 - m1)
    w1 = p_g / (1.0 + e21)
    w2 = p_g * e21 / (1.0 + e21)
    first_lower = i1 < i2
    j_lo = jnp.minimum(i1, i2) - (MOE_GROUPS + MOE_PER_GROUP * g_idx)
    j_hi = jnp.maximum(i1, i2) - (MOE_GROUPS + MOE_PER_GROUP * g_idx)
    pair = 0.5 * (j_lo * (7.0 - j_lo)) + (j_hi - j_lo - 1.0)
    cls = g_idx * N_PAIRS + pair
    w_lo = jnp.where(first_lower, w1, w2)
    w_hi = jnp.where(first_lower, w2, w1)

    onehot = (lane == cls).astype(F32)
    prefix = _dot(tri_scr[...], onehot.astype(BF16))
    rank = jnp.sum(onehot * (prefix + run_scr[...]), axis=-1, keepdims=True)
    run_scr[...] = run_scr[...] + jnp.sum(onehot, axis=0, keepdims=True)

    text_ref[:, :D] = t
    text_ref[:, D:] = jnp.where(lane_i == 0, w_lo, jnp.where(lane_i == 1, w_hi, 0.0))
    ones8 = jnp.ones((SUBLANES, LANES), BF16)
    ids8 = lax.broadcasted_iota(jnp.int32, (SUBLANES, LANES), 1).astype(F32).astype(BF16)
    d2 = jnp.floor(rank * (1.0 / 65536.0))
    rem = rank - 65536.0 * d2
    d1 = jnp.floor(rem * (1.0 / 256.0))
    d0 = rem - 256.0 * d1

    def as_row(col):
        return _dot_nt(ones8, jnp.where(lane_i == 0, col, 0.0).astype(BF16))

    cls_row = _dot_nt(ids8, onehot.astype(BF16))
    rank_row = as_row(d0) + 256.0 * as_row(d1) + 65536.0 * as_row(d2)
    sub = lax.broadcasted_iota(jnp.int32, (SUBLANES, tm), 0)
    meta_ref[...] = jnp.where(sub == 0, cls_row, jnp.where(sub == 1, rank_row, 0.0))
    cnt_ref[...] = run_scr[...]


def _router(x, g, wc, bias, cnt_in):
    T, D = x.shape
    tm = _pick_tile(T, ROUTE_TILE)
    return pl.pallas_call(
        _router_body,
        out_shape=(jax.ShapeDtypeStruct((T, D + EXTRA), F32),
                   jax.ShapeDtypeStruct((T // tm * SUBLANES, tm), F32),
                   jax.ShapeDtypeStruct((1, LANES), F32)),
        grid=(T // tm,),
        in_specs=[
            pl.BlockSpec((tm, D), lambda i: (i, 0)),
            _const_spec((1, D)), _const_spec(wc.shape), _const_spec((1, LANES)),
            _const_spec((1, LANES)),
        ],
        out_specs=(
            pl.BlockSpec((tm, D + EXTRA), lambda i: (i, 0)),
            pl.BlockSpec((SUBLANES, tm), lambda i: (i, 0)),
            pl.BlockSpec((1, LANES), lambda i: (0, 0)),
        ),
        scratch_shapes=[pltpu.VMEM((1, LANES), F32), pltpu.VMEM((tm, tm), BF16)],
        compiler_params=_params(("arbitrary",)),
        name="moe_router",
    )(x, g, wc, bias, cnt_in)


def _start_rows(n, idx_ref, row_copy):
    assert n % (SUBLANES * ROW_GROUPS_PER_ITER) == 0

    def body(it, carry):
        for g in range(ROW_GROUPS_PER_ITER):
            j = it * ROW_GROUPS_PER_ITER + g
            for u in range(SUBLANES):
                pos = idx_ref[0, 0, j * SUBLANES + u]
                row_copy(j, u, pos >> SUBLANE_SHIFT, pos & (SUBLANES - 1)).start(priority=u % 2)
        return carry

    lax.fori_loop(0, n // (SUBLANES * ROW_GROUPS_PER_ITER), body, 0)


def _dispatch_body(zinfo_ref, dest_ref, tp_ref, ts_ref, xs_ref, tbuf, zbuf, in_sems, row_sems, zsem,
                   *, nbp, nbs, steps):
    i = pl.program_id(0)
    nb = nbp + nbs
    groups = tbuf.shape[1]
    zgroups = zbuf.shape[0]

    def zero_copies():
        out = []
        for c in range(N_CLASSES):
            r0, r1 = zinfo_ref[c], zinfo_ref[N_CLASSES + c]
            g0, g1 = r0 >> SUBLANE_SHIFT, r1 >> SUBLANE_SHIFT
            for k in range(SUBLANES - 1):
                out.append((k < r1 - g1 * SUBLANES, pltpu.make_async_copy(
                    zbuf.at[0, pl.ds(0, 1), :], xs_ref.at[g1, pl.ds(k, 1), :], zsem)))
            n = g1 - g0
            b = zgroups // 2
            while b >= 1:
                out.append(((n & b) != 0, pltpu.make_async_copy(
                    zbuf.at[pl.ds(0, b)], xs_ref.at[pl.ds(g0 + (n & -(2 * b)), b)], zsem)))
                b //= 2
        for k in range(N_CLASSES):
            step = zinfo_ref[2 * N_CLASSES] + k
            out.append((step < steps, pltpu.make_async_copy(
                zbuf, xs_ref.at[pl.ds(jnp.minimum(step, steps - 1) * zgroups, zgroups)], zsem)))
        return out

    def tile_copies(j, slot):
        jp = jnp.clip(j, 0, nbp - 1)
        js = jnp.clip(j - nbp, 0, nbs - 1)
        return [(j < nbp, pltpu.make_async_copy(tp_ref.at[pl.ds(jp * groups, groups)],
                                                tbuf.at[slot], in_sems.at[slot])),
                (j >= nbp, pltpu.make_async_copy(ts_ref.at[pl.ds(js * groups, groups)],
                                                 tbuf.at[slot], in_sems.at[slot]))]

    def start_all(copies):
        for pred, cp in copies:
            @pl.when(pred)
            def _():
                cp.start()

    def wait_all(copies):
        for pred, cp in copies:
            @pl.when(pred)
            def _():
                cp.wait()

    def wait_rows(slot):
        pltpu.make_async_copy(tbuf.at[slot], xs_ref.at[pl.ds(0, groups)], row_sems.at[slot]).wait()

    @pl.when(i == 0)
    def _():
        for j in range(min(DISPATCH_SLOTS - 1, nb)):
            start_all(tile_copies(jnp.int32(j), j))
        zbuf[...] = jnp.zeros_like(zbuf)
        start_all(zero_copies())

    for slot in range(DISPATCH_SLOTS):
        @pl.when(i % DISPATCH_SLOTS == slot)
        def _():
            freed = (slot + DISPATCH_SLOTS - 1) % DISPATCH_SLOTS
            wait_all(tile_copies(i, slot))

            def row_copy(j, u, hi, lo):
                return pltpu.make_async_copy(tbuf.at[slot, j, pl.ds(u, 1), :],
                                             xs_ref.at[hi, pl.ds(lo, 1), :], row_sems.at[slot])

            _start_rows(groups * SUBLANES, dest_ref, row_copy)

            @pl.when(i >= 1)
            def _():
                wait_rows(freed)

            @pl.when(i + DISPATCH_SLOTS - 1 < nb)
            def _():
                start_all(tile_copies(i + DISPATCH_SLOTS - 1, freed))

            @pl.when(i == nb - 1)
            def _():
                wait_rows(slot)

    @pl.when(i == nb - 1)
    def _():
        wait_all(zero_copies())


def _dispatch(t_p, t_s, dest3, zinfo, P):
    nb, _, td = dest3.shape
    W = t_p.shape[1]
    assert t_p.shape[0] % td == 0 and t_s.shape[0] % td == 0 and FFN_TILE % SUBLANES == 0
    nbp, nbs = t_p.shape[0] // td, t_s.shape[0] // td
    assert nb == nbp + nbs and nbp >= 1 and nbs >= 1
    zgroups = FFN_TILE // SUBLANES
    body = functools.partial(_dispatch_body, nbp=nbp, nbs=nbs, steps=P // FFN_TILE)
    grid_spec = pltpu.PrefetchScalarGridSpec(
        num_scalar_prefetch=1,
        grid=(nb,),
        in_specs=[
            pl.BlockSpec((1, 1, td), lambda i, z: (i, 0, 0), memory_space=pltpu.SMEM),
            pl.BlockSpec(memory_space=pl.ANY),
            pl.BlockSpec(memory_space=pl.ANY),
        ],
        out_specs=pl.BlockSpec(memory_space=pl.ANY),
        scratch_shapes=[pltpu.VMEM((DISPATCH_SLOTS, td // SUBLANES, SUBLANES, W), F32),
                        pltpu.VMEM((zgroups, SUBLANES, W), F32),
                        pltpu.SemaphoreType.DMA((DISPATCH_SLOTS,)),
                        pltpu.SemaphoreType.DMA((DISPATCH_SLOTS,)),
                        pltpu.SemaphoreType.DMA(())],
    )
    return pl.pallas_call(
        body,
        out_shape=jax.ShapeDtypeStruct((P // SUBLANES, SUBLANES, W), F32),
        grid_spec=grid_spec,
        compiler_params=_params(("arbitrary",)),
        name="moe_dispatch",
    )(zinfo, dest3, t_p.reshape(-1, SUBLANES, W), t_s.reshape(-1, SUBLANES, W))


def _ffn_body(ea_ref, eb_ref, halves_ref, xs_ref, wga_ref, wua_ref, wda_ref,
              wgb_ref, wub_ref, wdb_ref, ys_ref, wg_scr, wu_scr, wd_scr):
    i = pl.program_id(0)
    tg, D = ys_ref.shape
    nh = halves_ref[i]
    prev = jnp.maximum(i - 1, 0)
    new_class = (i == 0) | (ea_ref[i] != ea_ref[prev]) | (eb_ref[i] != eb_ref[prev])

    @pl.when((nh > 0) & new_class)
    def _():
        for slot, (wg, wu, wd) in enumerate(((wga_ref, wua_ref, wda_ref), (wgb_ref, wub_ref, wdb_ref))):
            wg_scr[slot] = wg[0, 0].astype(BF16)
            wu_scr[slot] = wu[0, 0].astype(BF16)
            wd_scr[slot] = wd[0, 0].astype(BF16)

    def experts(rows):
        x = xs_ref[rows, :D].astype(BF16)
        w_a = xs_ref[rows, D:D + 1]
        w_b = xs_ref[rows, D + 1:D + 2]

        def expert(slot):
            gt = _dot(x, wg_scr[slot])
            hdn = (gt * jax.nn.sigmoid(gt)) * _dot(x, wu_scr[slot])
            return _dot(hdn.astype(BF16), wd_scr[slot])

        return w_a * expert(0) + w_b * expert(1)

    @pl.when(nh == 2)
    def _():
        ys_ref[...] = experts(slice(0, tg))

    @pl.when(nh == 1)
    def _():
        ys_ref[:tg - FFN_HALF, :] = jnp.zeros((tg - FFN_HALF, D), F32)
        ys_ref[tg - FFN_HALF:, :] = experts(slice(tg - FFN_HALF, tg))

    @pl.when(nh == 0)
    def _():
        ys_ref[...] = jnp.zeros_like(ys_ref)


def _ffn(xs, ea, eb, halves, layer, wg, wu, wd):
    P, W = xs.shape
    D = W - EXTRA
    ff = wg.shape[3]
    tg = FFN_TILE
    wspec_a = lambda shape: pl.BlockSpec(shape, lambda i, ea, eb, nh: (layer, ea[i], 0, 0))
    wspec_b = lambda shape: pl.BlockSpec(shape, lambda i, ea, eb, nh: (layer, eb[i], 0, 0))
    grid_spec = pltpu.PrefetchScalarGridSpec(
        num_scalar_prefetch=3,
        grid=(P // tg,),
        in_specs=[
            pl.BlockSpec((tg, W), lambda i, ea, eb, nh: (i, 0)),
            wspec_a((1, 1, D, ff)), wspec_a((1, 1, D, ff)), wspec_a((1, 1, ff, D)),
            wspec_b((1, 1, D, ff)), wspec_b((1, 1, D, ff)), wspec_b((1, 1, ff, D)),
        ],
        out_specs=pl.BlockSpec((tg, D), lambda i, ea, eb, nh: (i, 0)),
        scratch_shapes=[pltpu.VMEM((2, D, ff), BF16), pltpu.VMEM((2, D, ff), BF16),
                        pltpu.VMEM((2, ff, D), BF16)],
    )
    return pl.pallas_call(
        _ffn_body,
        out_shape=jax.ShapeDtypeStruct((P, D), F32),
        grid_spec=grid_spec,
        compiler_params=_params(("arbitrary",)),
        name="moe_ffn",
    )(ea, eb, halves, xs, wg, wu, wd, wg, wu, wd)


def _combine_body(dest_ref, dnext_ref, x_ref, gfin_ref, ys_ref, o_ref, ybuf, sems, *, final_norm):
    i, n = pl.program_id(0), pl.num_programs(0)
    tc, D = x_ref.shape
    groups = tc // SUBLANES

    def gather(idx_ref, slot):
        def row_copy(j, u, hi, lo):
            return pltpu.make_async_copy(ys_ref.at[hi, pl.ds(lo, 1), :],
                                         ybuf.at[slot, j, pl.ds(u, 1), :], sems.at[slot])
        _start_rows(tc, idx_ref, row_copy)

    @pl.when(i == 0)
    def _():
        gather(dest_ref, 0)

    for slot in range(2):
        @pl.when((i + 1 < n) & ((i + 1) % 2 == slot))
        def _():
            gather(dnext_ref, slot)

    for slot in range(2):
        @pl.when(i % 2 == slot)
        def _():
            pltpu.make_async_copy(ys_ref.at[pl.ds(0, groups)], ybuf.at[slot], sems.at[slot]).wait()
            y = x_ref[...] + ybuf[slot].reshape(tc, D)
            if final_norm:
                y = _rms(y, gfin_ref[...])
            o_ref[...] = y


def _combine(x, dest3, ys, gfin, final_norm):
    T, D = x.shape
    nb, _, tc = dest3.shape
    body = functools.partial(_combine_body, final_norm=final_norm)
    return pl.pallas_call(
        body,
        out_shape=jax.ShapeDtypeStruct((T, D), F32),
        grid=(nb,),
        in_specs=[
            pl.BlockSpec((1, 1, tc), lambda i: (i, 0, 0), memory_space=pltpu.SMEM),
            pl.BlockSpec((1, 1, tc), lambda i: (jnp.minimum(i + 1, nb - 1), 0, 0),
                         memory_space=pltpu.SMEM),
            pl.BlockSpec((tc, D), lambda i: (i, 0)),
            _const_spec((1, D)),
            pl.BlockSpec(memory_space=pl.ANY),
        ],
        out_specs=pl.BlockSpec((tc, D), lambda i: (i, 0)),
        scratch_shapes=[pltpu.VMEM((2, tc // SUBLANES, SUBLANES, D), F32),
                        pltpu.SemaphoreType.DMA((2,))],
        compiler_params=_params(("arbitrary",)),
        name="moe_combine",
    )(dest3, dest3, x, gfin, ys.reshape(-1, SUBLANES, D))


def _moe_tables(metas, counts, P):
    tg = FFN_TILE
    cnt = counts[0, :N_CLASSES].astype(jnp.int32)
    padded = ((cnt + tg - 1) // tg) * tg
    ends = jnp.cumsum(padded)
    offs = ends - padded
    front = padded - cnt
    dests = []
    for m in metas:
        m = m.reshape(-1, SUBLANES, m.shape[1])
        cls, rank = m[:, 0, :].reshape(-1).astype(jnp.int32), m[:, 1, :].reshape(-1).astype(jnp.int32)
        base = jnp.sum(jnp.where(cls[:, None] == jnp.arange(N_CLASSES, dtype=jnp.int32)[None, :],
                                 (offs + front)[None, :], 0), axis=1)
        dests.append(base + rank)
    tile_start = jnp.arange(P // tg, dtype=jnp.int32) * tg
    tile_cls = jnp.minimum(jnp.sum((tile_start[:, None] >= ends[None, :]).astype(jnp.int32), axis=1),
                           N_CLASSES - 1)
    rows = tg - jnp.clip(jnp.take(offs + front, tile_cls) - tile_start, 0, tg)
    halves = jnp.where(tile_start < ends[-1], (rows + FFN_HALF - 1) // FFN_HALF, 0).astype(jnp.int32)
    grp = tile_cls // N_PAIRS
    pair = tile_cls % N_PAIRS
    ea = grp * MOE_PER_GROUP + jnp.take(jnp.asarray(_PAIR_LO), pair)
    eb = grp * MOE_PER_GROUP + jnp.take(jnp.asarray(_PAIR_HI), pair)
    zinfo = jnp.concatenate([offs, offs + front, ends[-1:] // tg]).astype(jnp.int32)
    return dests, ea, eb, halves, zinfo


def _moe(xs_in, g, w_group, b_group, w_router, b_router, layer, wg, wu, wd, gfin, final_norm):
    assert len(xs_in) == 2
    D = xs_in[0].shape[1]
    T = sum(x.shape[0] for x in xs_in)
    wcat = jnp.zeros((D, LANES), F32)
    wcat = wcat.at[:, :MOE_GROUPS].set(w_group).at[:, MOE_GROUPS:MOE_GROUPS + MOE_EXPERTS].set(w_router)
    bias = jnp.zeros((1, LANES), F32)
    bias = bias.at[0, :MOE_GROUPS].set(b_group).at[0, MOE_GROUPS:MOE_GROUPS + MOE_EXPERTS].set(b_router)
    wc = wcat.astype(BF16)

    counts = jnp.zeros((1, LANES), F32)
    texts, metas = [], []
    for x in xs_in:
        t_ext, meta, counts = _router(x, g, wc, bias, counts)
        texts.append(t_ext)
        metas.append(meta)

    tg = FFN_TILE
    P = ((T + tg - 1) // tg + N_CLASSES) * tg
    dests, ea, eb, halves, zinfo = _moe_tables(metas, counts, P)
    tmove = _pick_tile(math.gcd(*(x.shape[0] for x in xs_in)), MOVE_TILE)
    dest3 = jnp.concatenate(dests).reshape(-1, 1, tmove)
    xs = _dispatch(texts[0], texts[1], dest3, zinfo, P)
    ys = _ffn(xs.reshape(P, D + EXTRA), ea, eb, halves, layer, wg, wu, wd)
    return [_combine(x, dest.reshape(-1, 1, _pick_tile(x.shape[0], COMBINE_TILE)), ys, gfin, final_norm)
            for x, dest in zip(xs_in, dests)]


def _rope_tables(pos, dk):
    half = dk // 2
    inv = 1.0 / (ROPE_BASE ** jnp.linspace(0.0, 1.0, half, dtype=F32))
    ang = pos.astype(F32)[:, None] * inv[None, :]
    return jnp.cos(ang), jnp.sin(ang)


def kernel(x_prompt, x_sample, state_ret, state_conv, state_lru, norm_mix_g, norm_ffn_g, norm_final_g, ret_w_in, ret_w_out, ret_norm_g, lru_w_in, lru_conv_w, lru_conv_b, lru_w_a, lru_b_a, lru_w_x, lru_b_x, lru_lambda, lru_w_out, moe_w_group, moe_b_group, moe_w_router, moe_b_router, moe_w_gate, moe_w_up, moe_w_down):
    B, S, D = x_prompt.shape
    Bs, Ls, _ = x_sample.shape
    Tp, Ts = B * S, Bs * Ls
    assert norm_mix_g.shape[0] == 2
    dk = D // RET_HEADS
    row = lambda v: v.reshape(1, -1)
    gfin = row(norm_final_g)

    cos_p, sin_p = _rope_tables(jnp.arange(S), dk)
    cos_s, sin_s = _rope_tables(PAST_LEN + jnp.arange(Ls), dk)
    ret_args = (row(norm_mix_g[0]), ret_w_in[0].astype(BF16), ret_w_out[0].astype(BF16),
                row(ret_norm_g[0]))
    xp, ret_p = _ret_prompt(x_prompt.reshape(Tp, D), B, S, *ret_args, cos_p, sin_p)
    xs, ret_s = _ret_sample(x_sample.reshape(Ts, D), Bs, Ls, *ret_args, cos_s, sin_s, state_ret[0])
    xp, xs = _moe([xp, xs], row(norm_ffn_g[0]), moe_w_group[0], moe_b_group[0], moe_w_router[0],
                  moe_b_router[0], 0, moe_w_gate, moe_w_up, moe_w_down, gfin, False)

    lru_args = (row(norm_mix_g[1]), lru_w_in[0].astype(BF16), lru_conv_w[0], row(lru_conv_b[0]),
                lru_w_a[0].astype(BF16), row(lru_b_a[0]), lru_w_x[0].astype(BF16), row(lru_b_x[0]),
                row(lru_lambda[0]), lru_w_out[0].astype(BF16))
    cs8 = jnp.pad(state_conv[0], ((0, 0), (SUBLANES - (CONV_W - 1), 0), (0, 0)))
    h0_rows = jnp.pad(state_lru[0].reshape(Bs, 1, D), ((0, 0), (0, Ls - 1), (0, 0)))
    xp, conv_p, lru_p = _lru_prompt(xp, B, S, *lru_args)
    xs, conv_s, lru_s = _lru_sample(xs, Bs, Ls, *lru_args, cs8, h0_rows)
    y_p, y_s = _moe([xp, xs], row(norm_ffn_g[1]), moe_w_group[1], moe_b_group[1], moe_w_router[1],
                    moe_b_router[1], 1, moe_w_gate, moe_w_up, moe_w_down, gfin, True)

    return (y_p.reshape(B, S, D), y_s.reshape(Bs, Ls, D),
            ret_p[None], conv_p[None, :, SUBLANES - (CONV_W - 1):], lru_p[None, :, SUBLANES - 1],
            ret_s[None], conv_s[None, :, SUBLANES - (CONV_W - 1):], lru_s[None, :, SUBLANES - 1])
```
